```python
import math
import jax, jax.numpy as jnp
from jax import lax
import numpy as np

D_MODEL = 1024
BATCH = 4
SEQ = 4096
DEPTH = 4

GRID_W = 64
CTX_LEN = 256
N_EVEN = (DEPTH + 1) // 2
N_ODD = DEPTH // 2
EPS = 1e-6

CHUNK = 128
A_WIDTH = D_MODEL // 2
A_GROUPS = 4
A_GROUP_W = A_WIDTH // A_GROUPS
B_WIDTH = D_MODEL // 2
B_CH = 16
B_GROUPS = B_WIDTH // B_CH
B_STATE = 64
EVEN_IN = 3 * A_WIDTH + 2 * B_WIDTH
EVEN_MIX = A_WIDTH + B_WIDTH
HEAD_DIM = 128
N_Q = D_MODEL // HEAD_DIM
N_KV = 2
Q_PER_KV = N_Q // N_KV
C_WIDTH = N_Q * HEAD_DIM
KV_WIDTH = N_KV * HEAD_DIM
ODD_IN = 2 * C_WIDTH + 2 * KV_WIDTH
Q_BLOCK = 128
ROPE_THETA = 10000.0
ROPE_PAIRS = HEAD_DIM // 4

kernel_name = "hybrid_gmlp_s5_gqa_prefix_trunk"


def _rms_norm(x, g):
    xf = x.astype(jnp.float32)
    y = xf * lax.rsqrt(jnp.mean(xf * xf, axis=-1, keepdims=True) + EPS)
    return (y * g.astype(jnp.float32)).astype(x.dtype)


def _layer_norm(x, g):
    xf = x.astype(jnp.float32)
    mu = jnp.mean(xf, axis=-1, keepdims=True)
    xc = xf - mu
    var = jnp.mean(xc * xc, axis=-1, keepdims=True)
    return (xc * lax.rsqrt(var + EPS) * g.astype(jnp.float32)).astype(x.dtype)


def _ada(cond, w, b):
    m = jax.nn.silu(cond) @ w + b
    return jnp.split(m, 3, axis=-1)


def _prologue(x, ctx, c, c_ctx, g, w_mod, b_mod):
    sh, sc, gt = _ada(c, w_mod, b_mod)
    shc, scc, gtc = _ada(c_ctx, w_mod, b_mod)
    hx = _rms_norm(x, g) * (1 + sc[:, None]) + sh[:, None]
    hc = _rms_norm(ctx, g) * (1 + scc) + shc
    h = jnp.concatenate([hc, hx], axis=1)
    return h, gt[:, None], gtc


def _chunk_gmlp(u, v, v_g, w_s, b_s):
    Bn, L, _ = v.shape
    v = _layer_norm(v, v_g)
    vc = v.reshape(Bn, L // CHUNK, CHUNK, A_GROUPS, A_GROUP_W)
    mixed = jnp.einsum('gpq,bnqgc->bnpgc', w_s, vc) + b_s.T[None, None, :, :, None]
    return u * mixed.reshape(Bn, L, A_WIDTH)


def _s5_discretize(lam_re, lam_im, log_dt, b_re, b_im):
    lam_re = lam_re.astype(jnp.float32)
    lam_im = lam_im.astype(jnp.float32)
    dt = jnp.exp(log_dt.astype(jnp.float32))[:, None]
    mag = jnp.exp(lam_re * dt)
    lb_re = mag * jnp.cos(lam_im * dt)
    lb_im = mag * jnp.sin(lam_im * dt)
    den = lam_re * lam_re + lam_im * lam_im
    n_re, n_im = lb_re - 1.0, lb_im
    f_re = (n_re * lam_re + n_im * lam_im) / den
    f_im = (n_im * lam_re - n_re * lam_im) / den
    b_re = b_re.astype(jnp.float32)
    b_im = b_im.astype(jnp.float32)
    bb_re = f_re[..., None] * b_re - f_im[..., None] * b_im
    bb_im = f_re[..., None] * b_im + f_im[..., None] * b_re
    return lb_re, lb_im, bb_re, bb_im


def _ssm_combine(e1, e2):
    a1r, a1i, b1r, b1i = e1
    a2r, a2i, b2r, b2i = e2
    return (a1r * a2r - a1i * a2i,
            a1r * a2i + a1i * a2r,
            a2r * b1r - a2i * b1i + b2r,
            a2r * b1i + a2i * b1r + b2i)


def _s5_direction(seq, lam_re, lam_im, log_dt, b_re, b_im, c_re, c_im):
    Bn, L, W = seq.shape
    lb_re, lb_im, bb_re, bb_im = _s5_discretize(lam_re, lam_im, log_dt, b_re, b_im)
    u = seq.reshape(Bn, L, B_GROUPS, B_CH)
    bu_re = jnp.einsum('blgh,gph->blgp', u, bb_re)
    bu_im = jnp.einsum('blgh,gph->blgp', u, bb_im)
    a_re = jnp.broadcast_to(lb_re, (1, L, B_GROUPS, B_STATE))
    a_im = jnp.broadcast_to(lb_im, (1, L, B_GROUPS, B_STATE))
    _, _, h_re, h_im = lax.associative_scan(_ssm_combine, (a_re, a_im, bu_re, bu_im), axis=1)
    y = (jnp.einsum('blgp,ghp->blgh', h_re, c_re.astype(jnp.float32))
         - jnp.einsum('blgp,ghp->blgh', h_im, c_im.astype(jnp.float32)))
    return y.reshape(Bn, L, W)


def _s5_bidir(xs, n_ctx, lam_re, lam_im, log_dt, b_re, b_im, c_re, c_im, d_skip, w_glu, b_glu):
    dtype = xs.dtype
    xf = xs.astype(jnp.float32)
    x_ctx, x_lat = xf[:, :n_ctx], xf[:, n_ctx:]
    fwd = _s5_direction(xf, lam_re[0], lam_im[0], log_dt[0], b_re[0], b_im[0], c_re[0], c_im[0])
    bwd_seq = jnp.concatenate([x_ctx[:, ::-1], x_lat[:, ::-1]], axis=1)
    bwd = _s5_direction(bwd_seq, lam_re[1], lam_im[1], log_dt[1], b_re[1], b_im[1], c_re[1], c_im[1])
    bwd = jnp.concatenate([bwd[:, :n_ctx][:, ::-1], bwd[:, n_ctx:][:, ::-1]], axis=1)
    y = fwd + bwd + d_skip.astype(jnp.float32) * xf
    y = jax.nn.gelu(y)
    y = y * jax.nn.sigmoid(y @ w_glu.astype(jnp.float32) + b_glu.astype(jnp.float32))
    return y.astype(dtype)


def _axial_rope(n_lat):
    rows = n_lat // GRID_W
    row = jnp.repeat(jnp.arange(rows), GRID_W)
    col = jnp.tile(jnp.arange(GRID_W), rows)
    freqs = ROPE_THETA ** (-jnp.arange(ROPE_PAIRS, dtype=jnp.float32) / ROPE_PAIRS)
    ang = jnp.stack([row[:, None] * freqs, col[:, None] * freqs], axis=1)
    return jnp.cos(ang), jnp.sin(ang)


def _apply_rope(x, cos, sin):
    lead = x.shape[:-1]
    xs = x.reshape(lead + (2, 2, ROPE_PAIRS)).astype(jnp.float32)
    bshape = (1, x.shape[1]) + (1,) * (x.ndim - 3) + (2, ROPE_PAIRS)
    cos = cos.reshape(bshape)
    sin = sin.reshape(bshape)
    x1, x2 = xs[..., 0, :], xs[..., 1, :]
    out = jnp.stack([x1 * cos - x2 * sin, x1 * sin + x2 * cos], axis=-2)
    return out.reshape(x.shape).astype(x.dtype)


def _attend(q, k, v):
    s = jnp.einsum('btkgd,bskd->bkgts', q, k).astype(jnp.float32) * (HEAD_DIM ** -0.5)
    p = jax.nn.softmax(s, axis=-1).astype(v.dtype)
    return jnp.einsum('bkgts,bskd->btkgd', p, v)


def _even_layer(x, ctx, c, c_ctx, norm_g, w_mod, b_mod, w_in, w_out, gm_v_g, gm_w_s, gm_b_s,
                lam_re, lam_im, log_dt, b_re, b_im, c_re, c_im, d_skip, w_glu, b_glu, update_ctx):
    n_ctx = ctx.shape[1]
    h, gt, gtc = _prologue(x, ctx, c, c_ctx, norm_g, w_mod, b_mod)
    u, v, g_a, xs, g_b = jnp.split(
        h @ w_in, [A_WIDTH, 2 * A_WIDTH, 3 * A_WIDTH, 3 * A_WIDTH + B_WIDTH], axis=-1)
    y_a = jnp.concatenate([
        _chunk_gmlp(u[:, :n_ctx], v[:, :n_ctx], gm_v_g, gm_w_s, gm_b_s),
        _chunk_gmlp(u[:, n_ctx:], v[:, n_ctx:], gm_v_g, gm_w_s, gm_b_s)], axis=1)
    y_a = y_a * jax.nn.silu(g_a)
    y_b = _s5_bidir(xs, n_ctx, lam_re, lam_im, log_dt, b_re, b_im, c_re, c_im, d_skip, w_glu, b_glu)
    y_b = y_b * jax.nn.silu(g_b)
    mix = jnp.concatenate([y_a, y_b], axis=-1)
    x = x + gt * (mix[:, n_ctx:] @ w_out)
    if update_ctx:
        ctx = ctx + gtc * (mix[:, :n_ctx] @ w_out)
    return x, ctx


def _odd_layer(x, ctx, c, c_ctx, norm_g, w_mod, b_mod, w_in, w_out, q_g, k_g, update_ctx):
    n_ctx = ctx.shape[1]
    Bn, L, _ = x.shape
    n_tot = n_ctx + L
    h, gt, gtc = _prologue(x, ctx, c, c_ctx, norm_g, w_mod, b_mod)
    q, k, v, gate = jnp.split(h @ w_in, [C_WIDTH, C_WIDTH + KV_WIDTH, C_WIDTH + 2 * KV_WIDTH], axis=-1)
    q = _rms_norm(q.reshape(Bn, n_tot, N_KV, Q_PER_KV, HEAD_DIM), q_g)
    k = _rms_norm(k.reshape(Bn, n_tot, N_KV, HEAD_DIM), k_g)
    v = v.reshape(Bn, n_tot, N_KV, HEAD_DIM)
    cos, sin = _axial_rope(L)
    q_lat = _apply_rope(q[:, n_ctx:], cos, sin)
    k_all = jnp.concatenate([k[:, :n_ctx], _apply_rope(k[:, n_ctx:], cos, sin)], axis=1)
    nb = L // Q_BLOCK
    qb = jnp.moveaxis(q_lat.reshape(Bn, nb, Q_BLOCK, N_KV, Q_PER_KV, HEAD_DIM), 1, 0)
    o = lax.map(lambda qi: _attend(qi, k_all, v), qb)
    o_lat = jnp.moveaxis(o, 0, 1).reshape(Bn, L, C_WIDTH)
    x = x + gt * ((o_lat * jax.nn.silu(gate[:, n_ctx:])) @ w_out)
    if update_ctx:
        o_ctx = _attend(q[:, :n_ctx], k[:, :n_ctx], v[:, :n_ctx]).reshape(Bn, n_ctx, C_WIDTH)
        ctx = ctx + gtc * ((o_ctx * jax.nn.silu(gate[:, :n_ctx])) @ w_out)
    return x, ctx


def setup_inputs(seed: int = 0) -> dict:
    key = jax.random.key(seed)
    ks = iter(jax.random.split(key, 40))
    f32 = jnp.float32
    nrm = lambda shape, s=1.0: jax.random.normal(next(ks), shape, f32) * s
    lam_re = -0.5 + nrm((N_EVEN, 2, B_GROUPS, B_STATE), 0.01)
    lam_im = (jnp.pi * jnp.arange(B_STATE, dtype=f32)) + nrm((N_EVEN, 2, B_GROUPS, B_STATE), 0.01)
    return {
        "x": nrm((BATCH, SEQ, D_MODEL)),
        "c": nrm((BATCH, D_MODEL)),
        "ctx": nrm((BATCH, CTX_LEN, D_MODEL)),
        "c_ctx": nrm((D_MODEL,)),
        "norm_g": 1.0 + nrm((DEPTH, D_MODEL), 0.02),
        "w_mod": nrm((DEPTH, D_MODEL, 3 * D_MODEL), 0.5 * D_MODEL ** -0.5),
        "b_mod": nrm((DEPTH, 3 * D_MODEL), 0.02),
        "we_in": nrm((N_EVEN, D_MODEL, EVEN_IN), D_MODEL ** -0.5),
        "we_out": nrm((N_EVEN, EVEN_MIX, D_MODEL), EVEN_MIX ** -0.5),
        "gm_v_g": 1.0 + nrm((N_EVEN, A_WIDTH), 0.02),
        "gm_w_s": nrm((N_EVEN, A_GROUPS, CHUNK, CHUNK), CHUNK ** -0.5),
        "gm_b_s": 1.0 + nrm((N_EVEN, A_GROUPS, CHUNK), 0.1),
        "s5_lam_re": lam_re,
        "s5_lam_im": lam_im,
        "s5_log_dt": jax.random.uniform(next(ks), (N_EVEN, 2, B_GROUPS), f32,
                                        minval=math.log(1e-3), maxval=math.log(1e-1)),
        "s5_b_re": nrm((N_EVEN, 2, B_GROUPS, B_STATE, B_CH), (2 * B_CH) ** -0.5),
        "s5_b_im": nrm((N_EVEN, 2, B_GROUPS, B_STATE, B_CH), (2 * B_CH) ** -0.5),
        "s5_c_re": nrm((N_EVEN, 2, B_GROUPS, B_CH, B_STATE), B_STATE ** -0.5),
        "s5_c_im": nrm((N_EVEN, 2, B_GROUPS, B_CH, B_STATE), B_STATE ** -0.5),
        "s5_d": nrm((N_EVEN, B_WIDTH)),
        "s5_w_glu": nrm((N_EVEN, B_WIDTH, B_WIDTH), B_WIDTH ** -0.5),
        "s5_b_glu": nrm((N_EVEN, B_WIDTH), 0.02),
        "wo_in": nrm((N_ODD, D_MODEL, ODD_IN), D_MODEL ** -0.5),
        "wo_out": nrm((N_ODD, C_WIDTH, D_MODEL), C_WIDTH ** -0.5),
        "q_norm_g": 1.0 + nrm((N_ODD, HEAD_DIM), 0.02),
        "k_norm_g": 1.0 + nrm((N_ODD, HEAD_DIM), 0.02),
        "final_g": 1.0 + nrm((D_MODEL,), 0.02),
    }


def reference(x, c, ctx, c_ctx, norm_g, w_mod, b_mod, we_in, we_out, gm_v_g, gm_w_s, gm_b_s,
              s5_lam_re, s5_lam_im, s5_log_dt, s5_b_re, s5_b_im, s5_c_re, s5_c_im, s5_d,
              s5_w_glu, s5_b_glu, wo_in, wo_out, q_norm_g, k_norm_g, final_g):
    for layer in range(DEPTH):
        update_ctx = layer < DEPTH - 1
        i = layer // 2
        if layer % 2 == 0:
            x, ctx = _even_layer(x, ctx, c, c_ctx, norm_g[layer], w_mod[layer], b_mod[layer],
                                 we_in[i], we_out[i], gm_v_g[i], gm_w_s[i], gm_b_s[i],
                                 s5_lam_re[i], s5_lam_im[i], s5_log_dt[i], s5_b_re[i], s5_b_im[i],
                                 s5_c_re[i], s5_c_im[i], s5_d[i], s5_w_glu[i], s5_b_glu[i], update_ctx)
        else:
            x, ctx = _odd_layer(x, ctx, c, c_ctx, norm_g[layer], w_mod[layer], b_mod[layer],
                                wo_in[i], wo_out[i], q_norm_g[i], k_norm_g[i], update_ctx)
    return _rms_norm(x, final_g)
```

```python
import functools
import math

import jax
import jax.numpy as jnp
from jax import lax
from jax.experimental import pallas as pl
from jax.experimental.pallas import tpu as pltpu

F32 = jnp.float32
BF16 = jnp.bfloat16

D_MODEL = 1024
BATCH = 4
SEQ = 4096
DEPTH = 4
GRID_W = 64
CTX_LEN = 256
EPS = 1e-6
NTOK = CTX_LEN + SEQ

LANES = 128
ROW_TILE = 256
N_TILES = NTOK // ROW_TILE
VMEM_LIMIT = 56 * 1024 * 1024

CHUNK = 128
A_WIDTH = D_MODEL // 2
A_GROUPS = 4
A_GROUP_W = A_WIDTH // A_GROUPS
B_WIDTH = D_MODEL // 2
B_CH = 16
B_GROUPS = B_WIDTH // B_CH
B_STATE = 64
EVEN_IN = 3 * A_WIDTH + 2 * B_WIDTH
HEAD_DIM = 128
N_Q = D_MODEL // HEAD_DIM
N_KV = 2
Q_PER_KV = N_Q // N_KV
C_WIDTH = N_Q * HEAD_DIM
KV_WIDTH = N_KV * HEAD_DIM
ODD_IN = 2 * C_WIDTH + 2 * KV_WIDTH
ROPE_THETA = 10000.0
ROPE_PAIRS = HEAD_DIM // 4

S5_Q = 16
S5_K = S5_Q * B_CH
N_CHUNKS = NTOK // S5_Q
CTX_CHUNKS = CTX_LEN // S5_Q
S5_ROWS = BATCH * N_CHUNKS
S5_BLOCK = 8
assert 2 * BATCH == S5_BLOCK
N_BLOCKS = S5_ROWS // S5_BLOCK
CTX_BLOCKS = CTX_CHUNKS * BATCH // S5_BLOCK
N_PAIRS = B_GROUPS // 2

MOD_ROWS = 8
CTX_ROW = BATCH


def _params(*sem):
    return pltpu.CompilerParams(dimension_semantics=sem, vmem_limit_bytes=VMEM_LIMIT)


def _mod_row(b, i):
    return jnp.where(i == 0, CTX_ROW, b)


def _mod_kernel(cond_ref, w_ref, b_ref, o_ref):
    c = cond_ref[...]
    s = c * jax.nn.sigmoid(c)
    o_ref[0] = jnp.dot(s, w_ref[0], preferred_element_type=F32,
                       precision=lax.Precision.HIGHEST) + b_ref[0]


def _modulation(cond, w_mod, b_mod):
    nblk = 3 * D_MODEL // D_MODEL
    return pl.pallas_call(
        _mod_kernel,
        grid=(DEPTH, nblk),
        in_specs=[
            pl.BlockSpec((MOD_ROWS, D_MODEL), lambda l, j: (0, 0)),
            pl.BlockSpec((1, D_MODEL, D_MODEL), lambda l, j: (l, 0, j)),
            pl.BlockSpec((1, 1, D_MODEL), lambda l, j: (l, 0, j)),
        ],
        out_specs=pl.BlockSpec((1, MOD_ROWS, D_MODEL), lambda l, j: (l, 0, j)),
        out_shape=jax.ShapeDtypeStruct((DEPTH, MOD_ROWS, 3 * D_MODEL), F32),
        compiler_params=_params("parallel", "parallel"),
        name="modulation",
    )(cond, w_mod, b_mod.reshape(DEPTH, 1, 3 * D_MODEL))


def _prologue(x, mod, g):
    sh = mod[:, :D_MODEL]
    sc = mod[:, D_MODEL:2 * D_MODEL]
    y = x * lax.rsqrt(jnp.mean(x * x, axis=-1, keepdims=True) + EPS)
    return (y * g) * (1 + sc) + sh


def _even_in_kernel(x_ref, mod_ref, g_ref, win_ref, vg_ref, ws_ref, bs_ref, ya_ref, xs_ref, gb_ref):
    h = _prologue(x_ref[0], mod_ref[0], g_ref[...])
    z = jnp.dot(h.astype(BF16), win_ref[...], preferred_element_type=F32)
    u = z[:, 0:A_WIDTH]
    v = z[:, A_WIDTH:2 * A_WIDTH]
    ga = z[:, 2 * A_WIDTH:3 * A_WIDTH]
    xs_ref[0] = z[:, 3 * A_WIDTH:3 * A_WIDTH + B_WIDTH]
    gb_ref[0] = z[:, 3 * A_WIDTH + B_WIDTH:]
    mu = jnp.mean(v, axis=-1, keepdims=True)
    vc = v - mu
    var = jnp.mean(vc * vc, axis=-1, keepdims=True)
    vn = ((vc * lax.rsqrt(var + EPS)) * vg_ref[...]).astype(BF16)
    rows = []
    for c in range(ROW_TILE // CHUNK):
        cols = []
        for g in range(A_GROUPS):
            blk = vn[c * CHUNK:(c + 1) * CHUNK, g * A_GROUP_W:(g + 1) * A_GROUP_W]
            cols.append(jnp.dot(ws_ref[g], blk, preferred_element_type=F32))
        rows.append(jnp.concatenate(cols, axis=1) + bs_ref[...])
    mixed = jnp.concatenate(rows, axis=0)
    ya_ref[0] = ((u * mixed) * (ga * jax.nn.sigmoid(ga))).astype(BF16)


def _even_in(xc, mod_l, norm_g, w_in, v_g, w_s, b_full):
    tile = lambda w: pl.BlockSpec((1, ROW_TILE, w), lambda b, i: (b, i, 0))
    full = lambda *s: pl.BlockSpec(s, lambda b, i: (0,) * len(s))
    return pl.pallas_call(
        _even_in_kernel,
        grid=(BATCH, N_TILES),
        in_specs=[
            tile(D_MODEL),
            pl.BlockSpec((1, 1, 3 * D_MODEL), lambda b, i: (_mod_row(b, i), 0, 0)),
            full(1, D_MODEL),
            full(D_MODEL, EVEN_IN),
            full(1, A_WIDTH),
            full(A_GROUPS, CHUNK, CHUNK),
            full(CHUNK, A_WIDTH),
        ],
        out_specs=[tile(A_WIDTH), tile(B_WIDTH), tile(B_WIDTH)],
        out_shape=[
            jax.ShapeDtypeStruct((BATCH, NTOK, A_WIDTH), BF16),
            jax.ShapeDtypeStruct((BATCH, NTOK, B_WIDTH), F32),
            jax.ShapeDtypeStruct((BATCH, NTOK, B_WIDTH), F32),
        ],
        compiler_params=_params("parallel", "parallel"),
        name="even_in",
    )(xc, mod_l, norm_g, w_in, v_g, w_s, b_full)


def _s5_weights(lam_re, lam_im, log_dt, b_re, b_im, c_re, c_im):
    hp = lax.Precision.HIGHEST
    lam_re = lam_re.astype(F32)
    lam_im = lam_im.astype(F32)
    dt = jnp.exp(log_dt.astype(F32))[..., None]
    mag = jnp.exp(lam_re * dt)
    lb_re = mag * jnp.cos(lam_im * dt)
    lb_im = mag * jnp.sin(lam_im * dt)
    den = lam_re * lam_re + lam_im * lam_im
    n_re, n_im = lb_re - 1.0, lb_im
    f_re = (n_re * lam_re + n_im * lam_im) / den
    f_im = (n_im * lam_re - n_re * lam_im) / den
    b_re = b_re.astype(F32)
    b_im = b_im.astype(F32)
    bb_re = f_re[..., None] * b_re - f_im[..., None] * b_im
    bb_im = f_re[..., None] * b_im + f_im[..., None] * b_re
    c_re = c_re.astype(F32)
    c_im = c_im.astype(F32)

    j = jnp.arange(S5_Q + 1, dtype=F32)[:, None, None, None]
    pm = jnp.exp(j * (lam_re * dt)[None])
    lp_re = pm * jnp.cos(j * (lam_im * dt)[None])
    lp_im = pm * jnp.sin(j * (lam_im * dt)[None])

    cl_re = c_re[None] * lp_re[:, :, :, None, :] - c_im[None] * lp_im[:, :, :, None, :]
    cl_im = c_re[None] * lp_im[:, :, :, None, :] + c_im[None] * lp_re[:, :, :, None, :]
    kern = (jnp.einsum('jdgop,dgpi->jdgoi', cl_re[:S5_Q], bb_re, precision=hp)
            - jnp.einsum('jdgop,dgpi->jdgoi', cl_im[:S5_Q], bb_im, precision=hp))
    kf, kb = kern[:, 0], kern[:, 1]
    kfull = jnp.concatenate([kb[:0:-1], (kf[0] + kb[0])[None], kf[1:]], axis=0)
    s_idx = jnp.arange(S5_Q)[:, None]
    t_idx = jnp.arange(S5_Q)[None, :]
    toep = kfull[t_idx - s_idx + (S5_Q - 1)]
    toep = toep.transpose(2, 0, 4, 1, 3).reshape(B_GROUPS, S5_K, S5_K)

    pw_f = jnp.arange(S5_Q - 1, -1, -1)
    pw_b = jnp.arange(S5_Q)

    def state_in(d, pw):
        pr, pi = lp_re[pw, d], lp_im[pw, d]
        wr = pr[..., None] * bb_re[d][None] - pi[..., None] * bb_im[d][None]
        wi = pr[..., None] * bb_im[d][None] + pi[..., None] * bb_re[d][None]
        to_k = lambda w: w.transpose(1, 0, 3, 2).reshape(B_GROUPS, S5_K, B_STATE)
        return to_k(wr), to_k(wi)

    blocks = [*state_in(0, pw_f), *state_in(1, pw_b)]
    zero = jnp.zeros((N_PAIRS, S5_K, B_STATE), F32)
    rows = []
    for e in range(2):
        cols = []
        for blk in blocks:
            blk = blk.reshape(N_PAIRS, 2, S5_K, B_STATE)[:, e]
            cols += [blk, zero] if e == 0 else [zero, blk]
        rows.append(jnp.concatenate(cols, axis=2))
    bst = jnp.concatenate(rows, axis=1)

    pw_cf = jnp.arange(1, S5_Q + 1)
    pw_cb = jnp.arange(S5_Q, 0, -1)

    def state_out(d, pw):
        to_n = lambda w: w.transpose(1, 3, 0, 2).reshape(B_GROUPS, B_STATE, S5_K)
        return to_n(cl_re[pw, d]), to_n(-cl_im[pw, d])

    oblocks = [*state_out(0, pw_cf), *state_out(1, pw_cb)]
    zrow = jnp.zeros((N_PAIRS, B_STATE, S5_K), F32)
    per_e = []
    for e in range(2):
        rws = []
        for blk in oblocks:
            blk = blk.reshape(N_PAIRS, 2, B_STATE, S5_K)[:, e]
            rws += [blk, zrow] if e == 0 else [zrow, blk]
        per_e.append(jnp.concatenate(rws, axis=1))
    cst = jnp.stack(per_e, axis=1).reshape(B_GROUPS, 4 * LANES, S5_K)

    a_rows = []
    for d in range(2):
        for part in (lp_re, lp_im):
            a_rows.append(part[S5_Q, d].reshape(N_PAIRS, 1, 2 * B_STATE))
    a16 = jnp.concatenate(a_rows + [jnp.zeros((N_PAIRS, 4, LANES), F32)], axis=1)
    return toep.astype(BF16), bst.astype(BF16), cst.astype(BF16), a16


def _s5_kernel(xg_ref, toep_ref, bst_ref, cst_ref, a16_ref, yg_ref, sl_ref):
    x0 = xg_ref[0]
    x1 = xg_ref[1]
    sl_ref[...] = jnp.dot(jnp.concatenate([x0, x1], axis=1), bst_ref[0], preferred_element_type=F32)
    a = a16_ref[0]
    bc = lambda r: jnp.broadcast_to(a[r:r + 1], (S5_BLOCK, LANES))
    a_f = (bc(0), bc(1))
    a_b = (bc(2), bc(3))
    first = lax.broadcasted_iota(jnp.int32, (S5_BLOCK, LANES), 0) < BATCH

    def cmul_add(a_ri, s_ri, l_ri):
        (a_re, a_im), (s_re, s_im), (l_re, l_im) = a_ri, s_ri, l_ri
        return (a_re * s_re - a_im * s_im + l_re, a_re * s_im + a_im * s_re + l_im)

    swap = lambda s_ri: tuple(pltpu.roll(s, BATCH, 0) for s in s_ri)

    def advance(m, col, a_ri, carry, fwd):
        r0 = pl.multiple_of(m * S5_BLOCK, S5_BLOCK)
        rows = pl.ds(r0, S5_BLOCK)
        loc = (sl_ref[rows, col:col + LANES], sl_ref[rows, col + LANES:col + 2 * LANES])
        mid = swap(cmul_add(a_ri, carry, loc))
        keep = first if fwd else jnp.logical_not(first)
        sl_ref[rows, col:col + LANES] = jnp.where(keep, carry[0], mid[0])
        sl_ref[rows, col + LANES:col + 2 * LANES] = jnp.where(keep, carry[1], mid[1])
        return swap(cmul_add(a_ri, mid, loc))

    def step(j, carry):
        sf, sb = carry
        mb = jnp.where(j < CTX_BLOCKS, CTX_BLOCKS - 1 - j, N_BLOCKS + CTX_BLOCKS - 1 - j)
        return (advance(j, 0, a_f, sf, True), advance(mb, 2 * LANES, a_b, sb, False))

    z = jnp.zeros((S5_BLOCK, LANES), F32)
    lax.fori_loop(0, N_BLOCKS, step, ((z, z), (z, z)))

    sp = sl_ref[...].astype(BF16)
    yg_ref[0] = (jnp.dot(x0, toep_ref[0], preferred_element_type=F32)
                 + jnp.dot(sp, cst_ref[0], preferred_element_type=F32))
    yg_ref[1] = (jnp.dot(x1, toep_ref[1], preferred_element_type=F32)
                 + jnp.dot(sp, cst_ref[1], preferred_element_type=F32))


def _s5(xg, toep, bst, cst, a16):
    return pl.pallas_call(
        _s5_kernel,
        grid=(N_PAIRS,),
        in_specs=[
            pl.BlockSpec((2, S5_ROWS, S5_K), lambda q: (q, 0, 0)),
            pl.BlockSpec((2, S5_K, S5_K), lambda q: (q, 0, 0)),
            pl.BlockSpec((1, 2 * S5_K, 4 * LANES), lambda q: (q, 0, 0)),
            pl.BlockSpec((2, 4 * LANES, S5_K), lambda q: (q, 0, 0)),
            pl.BlockSpec((1, 8, LANES), lambda q: (q, 0, 0)),
        ],
        out_specs=pl.BlockSpec((2, S5_ROWS, S5_K), lambda q: (q, 0, 0)),
        out_shape=jax.ShapeDtypeStruct((B_GROUPS, S5_ROWS, S5_K), F32),
        scratch_shapes=[pltpu.VMEM((S5_ROWS, 4 * LANES), F32)],
        compiler_params=_params("parallel"),
        name="s5_chunked",
    )(xg, toep, bst, cst, a16)


def _to_groups(xs):
    x = xs.reshape(BATCH, N_CHUNKS, S5_Q, B_GROUPS, B_CH).transpose(3, 1, 0, 2, 4)
    return x.reshape(B_GROUPS, S5_ROWS, S5_K).astype(BF16)


def _from_groups(yg):
    y = yg.reshape(B_GROUPS, N_CHUNKS, BATCH, S5_Q, B_CH).transpose(2, 1, 3, 0, 4)
    return y.reshape(BATCH, NTOK, B_WIDTH)


def _even_out_kernel(ya_ref, xs_ref, gb_ref, ys_ref, x_ref, mod_ref, d_ref, wglu_ref, bglu_ref,
                     wout_ref, o_ref):
    y = ys_ref[0] + d_ref[...] * xs_ref[0]
    y = jax.nn.gelu(y)
    t = jnp.dot(y.astype(BF16), wglu_ref[...], preferred_element_type=F32) + bglu_ref[...]
    y = y * jax.nn.sigmoid(t)
    gb = gb_ref[0]
    yb = (y * (gb * jax.nn.sigmoid(gb))).astype(BF16)
    mix = (jnp.dot(ya_ref[0], wout_ref[0:A_WIDTH, :], preferred_element_type=F32)
           + jnp.dot(yb, wout_ref[A_WIDTH:, :], preferred_element_type=F32))
    gt = mod_ref[0][:, 2 * D_MODEL:]
    o_ref[0] = x_ref[0] + gt * mix


def _even_out(ya, xs, gb, ys, xc, mod_l, d_skip, w_glu, b_glu, w_out):
    tile = lambda w: pl.BlockSpec((1, ROW_TILE, w), lambda b, i: (b, i, 0))
    full = lambda *s: pl.BlockSpec(s, lambda b, i: (0,) * len(s))
    return pl.pallas_call(
        _even_out_kernel,
        grid=(BATCH, N_TILES),
        in_specs=[
            tile(A_WIDTH), tile(B_WIDTH), tile(B_WIDTH), tile(B_WIDTH), tile(D_MODEL),
            pl.BlockSpec((1, 1, 3 * D_MODEL), lambda b, i: (_mod_row(b, i), 0, 0)),
            full(1, B_WIDTH), full(B_WIDTH, B_WIDTH), full(1, B_WIDTH), full(D_MODEL, D_MODEL),
        ],
        out_specs=tile(D_MODEL),
        out_shape=jax.ShapeDtypeStruct((BATCH, NTOK, D_MODEL), F32),
        compiler_params=_params("parallel", "parallel"),
        name="even_out",
    )(ya, xs, gb, ys, xc, mod_l, d_skip, w_glu, b_glu, w_out)


def _rope_tables():
    rows = SEQ // GRID_W
    row = jnp.repeat(jnp.arange(rows), GRID_W)
    col = jnp.tile(jnp.arange(GRID_W), rows)
    freqs = ROPE_THETA ** (-jnp.arange(ROPE_PAIRS, dtype=F32) / ROPE_PAIRS)
    ar = row[:, None] * freqs
    ac = col[:, None] * freqs
    zeros = jnp.zeros_like(ar)
    cos = jnp.concatenate([jnp.cos(ar), jnp.cos(ar), jnp.cos(ac), jnp.cos(ac)], axis=1)
    sa = jnp.concatenate([-jnp.sin(ar), zeros, -jnp.sin(ac), zeros], axis=1)
    sb = jnp.concatenate([zeros, jnp.sin(ar), zeros, jnp.sin(ac)], axis=1)
    pad = lambda t, v: jnp.concatenate([jnp.full((CTX_LEN, HEAD_DIM), v, F32), t], axis=0)
    return pad(cos, 1.0), pad(sa, 0.0), pad(sb, 0.0)


def _odd_in_kernel(x_ref, mod_ref, g_ref, win_ref, qg_ref, kg_ref, cos_ref, sa_ref, sb_ref,
                   q_ref, k_ref, v_ref, gate_ref):
    h = _prologue(x_ref[0], mod_ref[0], g_ref[...])
    z = jnp.dot(h.astype(BF16), win_ref[...], preferred_element_type=F32)
    cos = cos_ref[...]
    sa = sa_ref[...]
    sb = sb_ref[...]

    def norm_rope(xh, gain, scale):
        xn = (xh * lax.rsqrt(jnp.mean(xh * xh, axis=-1, keepdims=True) + EPS)) * gain
        xa = pltpu.roll(xn, HEAD_DIM - ROPE_PAIRS, 1)
        xb = pltpu.roll(xn, ROPE_PAIRS, 1)
        out = xn * cos + xa * sa + xb * sb
        return (out * scale).astype(BF16) if scale is not None else out.astype(BF16)

    for hq in range(N_Q):
        q_ref[0, :, hq * HEAD_DIM:(hq + 1) * HEAD_DIM] = norm_rope(
            z[:, hq * HEAD_DIM:(hq + 1) * HEAD_DIM], qg_ref[...], HEAD_DIM ** -0.5)
    for hk in range(N_KV):
        lo = C_WIDTH + hk * HEAD_DIM
        k_ref[0, :, hk * HEAD_DIM:(hk + 1) * HEAD_DIM] = norm_rope(z[:, lo:lo + HEAD_DIM], kg_ref[...], None)
    v_ref[0] = z[:, C_WIDTH + KV_WIDTH:C_WIDTH + 2 * KV_WIDTH].astype(BF16)
    gate_ref[0] = z[:, C_WIDTH + 2 * KV_WIDTH:]


def _odd_in(xc, mod_l, norm_g, w_in, q_g, k_g, cos, sa, sb):
    tile = lambda w: pl.BlockSpec((1, ROW_TILE, w), lambda b, i: (b, i, 0))
    full = lambda *s: pl.BlockSpec(s, lambda b, i: (0,) * len(s))
    tab = pl.BlockSpec((ROW_TILE, HEAD_DIM), lambda b, i: (i, 0))
    return pl.pallas_call(
        _odd_in_kernel,
        grid=(BATCH, N_TILES),
        in_specs=[
            tile(D_MODEL),
            pl.BlockSpec((1, 1, 3 * D_MODEL), lambda b, i: (_mod_row(b, i), 0, 0)),
            full(1, D_MODEL), full(D_MODEL, ODD_IN), full(1, HEAD_DIM), full(1, HEAD_DIM),
            tab, tab, tab,
        ],
        out_specs=[tile(C_WIDTH), tile(KV_WIDTH), tile(KV_WIDTH), tile(C_WIDTH)],
        out_shape=[
            jax.ShapeDtypeStruct((BATCH, NTOK, C_WIDTH), BF16),
            jax.ShapeDtypeStruct((BATCH, NTOK, KV_WIDTH), BF16),
            jax.ShapeDtypeStruct((BATCH, NTOK, KV_WIDTH), BF16),
            jax.ShapeDtypeStruct((BATCH, NTOK, C_WIDTH), F32),
        ],
        compiler_params=_params("parallel", "parallel"),
        name="odd_in",
    )(xc, mod_l, norm_g, w_in, q_g, k_g, cos, sa, sb)


def _attn_kernel(q_ref, k_ref, v_ref, o_ref, *, tile_offset):
    i = pl.program_id(2) + tile_offset

    def attend(n_keys):
        k = k_ref[0, 0:n_keys, :]
        v = v_ref[0, 0:n_keys, :]
        for h in range(Q_PER_KV):
            q = q_ref[0, :, h * HEAD_DIM:(h + 1) * HEAD_DIM]
            s = lax.dot_general(q, k, (((1,), (1,)), ((), ())), preferred_element_type=F32)
            p = jnp.exp(s - jnp.max(s, axis=-1, keepdims=True))
            l = jnp.sum(p, axis=-1, keepdims=True)
            o = jnp.dot(p.astype(BF16), v, preferred_element_type=F32) / l
            o_ref[0, :, h * HEAD_DIM:(h + 1) * HEAD_DIM] = o.astype(BF16)

    if tile_offset == 0:
        @pl.when(i == 0)
        def _():
            attend(CTX_LEN)

    @pl.when(i > 0)
    def _():
        attend(NTOK)


def _attention(q, k, v, tile_offset):
    n_tiles = N_TILES - tile_offset
    width = Q_PER_KV * HEAD_DIM
    qspec = pl.BlockSpec((1, ROW_TILE, width), lambda b, g, i: (b, i + tile_offset, g))
    kvspec = pl.BlockSpec((1, NTOK, HEAD_DIM), lambda b, g, i: (b, 0, g))
    return pl.pallas_call(
        functools.partial(_attn_kernel, tile_offset=tile_offset),
        grid=(BATCH, N_KV, n_tiles),
        in_specs=[qspec, kvspec, kvspec],
        out_specs=pl.BlockSpec((1, ROW_TILE, width), lambda b, g, i: (b, i, g)),
        out_shape=jax.ShapeDtypeStruct((BATCH, n_tiles * ROW_TILE, C_WIDTH), BF16),
        compiler_params=_params("parallel", "parallel", "arbitrary"),
        name="attention",
    )(q, k, v)


def _odd_out_kernel(o_ref, gate_ref, x_ref, mod_ref, wout_ref, fg_ref, out_ref, *, final):
    g = gate_ref[0]
    a = (o_ref[0].astype(F32) * (g * jax.nn.sigmoid(g))).astype(BF16)
    gt = mod_ref[0][:, 2 * D_MODEL:]
    x = x_ref[0] + gt * jnp.dot(a, wout_ref[...], preferred_element_type=F32)
    if final:
        x = (x * lax.rsqrt(jnp.mean(x * x, axis=-1, keepdims=True) + EPS)) * fg_ref[...]
    out_ref[0] = x


def _odd_out(o, gate, xc, mod_l, w_out, final_g, final):
    off = 1 if final else 0
    tile = lambda w: pl.BlockSpec((1, ROW_TILE, w), lambda b, i: (b, i + off, 0))
    full = lambda *s: pl.BlockSpec(s, lambda b, i: (0,) * len(s))
    return pl.pallas_call(
        functools.partial(_odd_out_kernel, final=final),
        grid=(BATCH, N_TILES - off),
        in_specs=[
            pl.BlockSpec((1, ROW_TILE, C_WIDTH), lambda b, i: (b, i, 0)),
            tile(C_WIDTH), tile(D_MODEL),
            pl.BlockSpec((1, 1, 3 * D_MODEL), lambda b, i: (_mod_row(b, i + off), 0, 0)),
            full(D_MODEL, D_MODEL), full(1, D_MODEL),
        ],
        out_specs=pl.BlockSpec((1, ROW_TILE, D_MODEL), lambda b, i: (b, i, 0)),
        out_shape=jax.ShapeDtypeStruct((BATCH, NTOK - off * ROW_TILE, D_MODEL), F32),
        compiler_params=_params("parallel", "parallel"),
        name="odd_out_final" if final else "odd_out",
    )(o, gate, xc, mod_l, w_out, final_g)


def kernel(x, c, ctx, c_ctx, norm_g, w_mod, b_mod, we_in, we_out, gm_v_g, gm_w_s, gm_b_s,
           s5_lam_re, s5_lam_im, s5_log_dt, s5_b_re, s5_b_im, s5_c_re, s5_c_im, s5_d,
           s5_w_glu, s5_b_glu, wo_in, wo_out, q_norm_g, k_norm_g, final_g):
    cond = jnp.concatenate([c, c_ctx[None], jnp.zeros((MOD_ROWS - BATCH - 1, D_MODEL), F32)], axis=0)
    mods = _modulation(cond, w_mod, b_mod).reshape(DEPTH, MOD_ROWS, 1, 3 * D_MODEL)
    xc = jnp.concatenate([ctx, x], axis=1)
    cos, sa, sb = _rope_tables()
    row = lambda a: a.reshape(1, -1)

    for layer in range(DEPTH):
        i = layer // 2
        if layer % 2 == 0:
            b_full = jnp.repeat(gm_b_s[i].T, A_GROUP_W, axis=1)
            ya, xs, gb = _even_in(xc, mods[layer], row(norm_g[layer]), we_in[i].astype(BF16),
                                  row(gm_v_g[i]), gm_w_s[i].astype(BF16), b_full)
            toep, bst, cst, a16 = _s5_weights(s5_lam_re[i], s5_lam_im[i], s5_log_dt[i], s5_b_re[i],
                                              s5_b_im[i], s5_c_re[i], s5_c_im[i])
            ys = _from_groups(_s5(_to_groups(xs), toep, bst, cst, a16))
            xc = _even_out(ya, xs, gb, ys, xc, mods[layer], row(s5_d[i]), s5_w_glu[i].astype(BF16),
                           row(s5_b_glu[i]), we_out[i].astype(BF16))
        else:
            final = layer == DEPTH - 1
            q, k, v, gate = _odd_in(xc, mods[layer], row(norm_g[layer]), wo_in[i].astype(BF16),
                                    row(q_norm_g[i]), row(k_norm_g[i]), cos, sa, sb)
            o = _attention(q, k, v, 1 if final else 0)
            xc = _odd_out(o, gate, xc, mods[layer], wo_out[i].astype(BF16), row(final_g), final)
    return xc
```

```python
import functools

import jax
import jax.numpy as jnp
from jax import lax
from jax.experimental import pallas as pl
from jax.experimental.pallas import tpu as pltpu

F32 = jnp.float32
BF16 = jnp.bfloat16
HIGHEST = lax.Precision.HIGHEST

D_MODEL = 1024
BATCH = 4
SEQ = 4096
DEPTH = 4
GRID_W = 64
CTX_LEN = 256
EPS = 1e-6
NTOK = CTX_LEN + SEQ

LANES = 128
SUBLANES = 8
ROW_TILE = 256
N_TILES = NTOK // ROW_TILE
VMEM_LIMIT = 56 * 1024 * 1024

CHUNK = 128
A_WIDTH = D_MODEL // 2
A_GROUPS = 4
A_GROUP_W = A_WIDTH // A_GROUPS
B_WIDTH = D_MODEL // 2
B_CH = 16
B_GROUPS = B_WIDTH // B_CH
B_STATE = 64
EVEN_IN = 3 * A_WIDTH + 2 * B_WIDTH
HEAD_DIM = 128
N_Q = D_MODEL // HEAD_DIM
N_KV = 2
Q_PER_KV = N_Q // N_KV
C_WIDTH = N_Q * HEAD_DIM
KV_WIDTH = N_KV * HEAD_DIM
ODD_IN = 2 * C_WIDTH + 2 * KV_WIDTH
ROPE_THETA = 10000.0
ROPE_PAIRS = HEAD_DIM // 4

S5_Q = 16
S5_K = S5_Q * B_CH
N_CHUNKS = NTOK // S5_Q
S5_ROWS = BATCH * N_CHUNKS
TILE_CHUNKS = ROW_TILE // S5_Q
BLOCKS_PER_SAMPLE = N_CHUNKS // SUBLANES
CTX_BLOCKS = CTX_LEN // S5_Q // SUBLANES
N_PAIRS = B_GROUPS // 2
GROUPS_PER_VREG = LANES // B_CH
N_LAGS = 2 * S5_Q - 1

MOD_ROWS = 8
CTX_ROW = BATCH


def _params(*sem):
    return pltpu.CompilerParams(dimension_semantics=sem, vmem_limit_bytes=VMEM_LIMIT)


def _mod_row(b, i):
    return jnp.where(i == 0, CTX_ROW, b)


def _lane_block(rows):
    return lax.broadcasted_iota(jnp.int32, (rows, LANES), 1) // B_CH


def _move_blocks(pieces, blk):
    acc = None
    for dst, (src, src_blk) in enumerate(pieces):
        sh = ((dst - src_blk) % GROUPS_PER_VREG) * B_CH
        rolled = pltpu.roll(src, sh, 1) if sh else src
        acc = rolled if acc is None else jnp.where(blk == dst, rolled, acc)
    return acc


def _mod_kernel(cond_ref, w_ref, b_ref, o_ref):
    c = cond_ref[...]
    s = c * jax.nn.sigmoid(c)
    o_ref[0] = jnp.dot(s, w_ref[0], preferred_element_type=F32, precision=HIGHEST) + b_ref[0]


def _modulation(cond, w_mod, b_mod):
    nblk = 3 * D_MODEL // D_MODEL
    return pl.pallas_call(
        _mod_kernel,
        grid=(DEPTH, nblk),
        in_specs=[
            pl.BlockSpec((MOD_ROWS, D_MODEL), lambda l, j: (0, 0)),
            pl.BlockSpec((1, D_MODEL, D_MODEL), lambda l, j: (l, 0, j)),
            pl.BlockSpec((1, 1, D_MODEL), lambda l, j: (l, 0, j)),
        ],
        out_specs=pl.BlockSpec((1, MOD_ROWS, D_MODEL), lambda l, j: (l, 0, j)),
        out_shape=jax.ShapeDtypeStruct((DEPTH, MOD_ROWS, 3 * D_MODEL), F32),
        compiler_params=_params("parallel", "parallel"),
        name="modulation",
    )(cond, w_mod, b_mod.reshape(DEPTH, 1, 3 * D_MODEL))


def _prologue(x, mod, g):
    sh = mod[:, :D_MODEL]
    sc = mod[:, D_MODEL:2 * D_MODEL]
    y = x * lax.rsqrt(jnp.mean(x * x, axis=-1, keepdims=True) + EPS)
    return (y * g) * (1 + sc) + sh


def _chunk_perm():
    r = jnp.arange(ROW_TILE)
    swapped = (r % TILE_CHUNKS) * S5_Q + r // TILE_CHUNKS
    return (r[:, None] == swapped[None, :]).astype(BF16)


def _even_in_kernel(x_ref, mod_ref, g_ref, win_ref, vg_ref, ws_ref, bs_ref, perm_ref,
                    ya_ref, xs_ref, gb_ref, xg_ref):
    h = _prologue(x_ref[0], mod_ref[0], g_ref[...])
    z = jnp.dot(h.astype(BF16), win_ref[...], preferred_element_type=F32)
    u = z[:, 0:A_WIDTH]
    v = z[:, A_WIDTH:2 * A_WIDTH]
    ga = z[:, 2 * A_WIDTH:3 * A_WIDTH]
    xs = z[:, 3 * A_WIDTH:3 * A_WIDTH + B_WIDTH]
    xs_ref[0] = xs
    gb_ref[0] = z[:, 3 * A_WIDTH + B_WIDTH:]
    mu = jnp.mean(v, axis=-1, keepdims=True)
    vc = v - mu
    var = jnp.mean(vc * vc, axis=-1, keepdims=True)
    vn = ((vc * lax.rsqrt(var + EPS)) * vg_ref[...]).astype(BF16)
    rows = []
    for c in range(ROW_TILE // CHUNK):
        cols = []
        for g in range(A_GROUPS):
            blk = vn[c * CHUNK:(c + 1) * CHUNK, g * A_GROUP_W:(g + 1) * A_GROUP_W]
            cols.append(jnp.dot(ws_ref[g], blk, preferred_element_type=F32))
        rows.append(jnp.concatenate(cols, axis=1) + bs_ref[...])
    mixed = jnp.concatenate(rows, axis=0)
    ya_ref[0] = ((u * mixed) * (ga * jax.nn.sigmoid(ga))).astype(BF16)

    r = jnp.dot(perm_ref[...], xs.astype(BF16), preferred_element_type=F32)
    blk = _lane_block(TILE_CHUNKS)
    for g in range(B_GROUPS):
        col, src_blk = divmod(g, GROUPS_PER_VREG)
        halves = []
        for j in range(S5_K // LANES):
            pieces = [(r[(j * GROUPS_PER_VREG + k) * TILE_CHUNKS:(j * GROUPS_PER_VREG + k + 1) * TILE_CHUNKS,
                         col * LANES:(col + 1) * LANES], src_blk) for k in range(GROUPS_PER_VREG)]
            halves.append(_move_blocks(pieces, blk))
        xg_ref[g] = jnp.concatenate(halves, axis=1).astype(BF16)


def _even_in(xc, mod_l, norm_g, w_in, v_g, w_s, b_full, perm):
    tile = lambda w: pl.BlockSpec((1, ROW_TILE, w), lambda b, i: (b, i, 0))
    full = lambda *s: pl.BlockSpec(s, lambda b, i: (0,) * len(s))
    return pl.pallas_call(
        _even_in_kernel,
        grid=(BATCH, N_TILES),
        in_specs=[
            tile(D_MODEL),
            pl.BlockSpec((1, 1, 3 * D_MODEL), lambda b, i: (_mod_row(b, i), 0, 0)),
            full(1, D_MODEL),
            full(D_MODEL, EVEN_IN),
            full(1, A_WIDTH),
            full(A_GROUPS, CHUNK, CHUNK),
            full(CHUNK, A_WIDTH),
            full(ROW_TILE, ROW_TILE),
        ],
        out_specs=[tile(A_WIDTH), tile(B_WIDTH), tile(B_WIDTH),
                   pl.BlockSpec((B_GROUPS, TILE_CHUNKS, S5_K), lambda b, i: (0, b * N_TILES + i, 0))],
        out_shape=[
            jax.ShapeDtypeStruct((BATCH, NTOK, A_WIDTH), BF16),
            jax.ShapeDtypeStruct((BATCH, NTOK, B_WIDTH), F32),
            jax.ShapeDtypeStruct((BATCH, NTOK, B_WIDTH), F32),
            jax.ShapeDtypeStruct((B_GROUPS, S5_ROWS, S5_K), BF16),
        ],
        compiler_params=_params("parallel", "parallel"),
        name="even_in",
    )(xc, mod_l, norm_g, w_in, v_g, w_s, b_full, perm)


def _s5_kernel(xg_ref, prow_ref, pmat_ref, yg_ref, sl_ref, bst_ref, toep_ref, cst_ref, lhs_ref):
    lane = lax.broadcasted_iota(jnp.int32, (S5_Q, LANES), 1)
    in_group = (lane < B_STATE, lane >= B_STATE)
    blk = lane // B_CH
    zero = jnp.zeros((S5_Q, LANES), F32)

    def cmul(ar, ai, br, bi):
        return ar * br - ai * bi, ar * bi + ai * br

    w_in, c_out, scan_consts, c_rows = [], [], [], []
    for d in range(2):
        lam_re = prow_ref[0, d, 0:1, :]
        lam_im = prow_ref[0, d, 1:2, :]
        dt = jnp.exp(prow_ref[0, d, 2:3, :])
        z_re, z_im = lam_re * dt, lam_im * dt
        mag = jnp.exp(z_re)
        lb_re, lb_im = mag * jnp.cos(z_im), mag * jnp.sin(z_im)
        den = lam_re * lam_re + lam_im * lam_im
        n_re, n_im = lb_re - 1.0, lb_im
        f_re = (n_re * lam_re + n_im * lam_im) / den
        f_im = (n_im * lam_re - n_re * lam_im) / den
        b_re, b_im = pmat_ref[0, d, 0], pmat_ref[0, d, 1]
        c_re, c_im = pmat_ref[0, d, 2], pmat_ref[0, d, 3]
        bb_re, bb_im = cmul(f_re, f_im, b_re, b_im)
        c_rows.append((c_re, c_im))

        def powers(steps):
            m = jnp.exp(steps * z_re)
            return m * jnp.cos(steps * z_im), m * jnp.sin(steps * z_im)

        j = lax.broadcasted_iota(jnp.int32, (3 * SUBLANES, LANES), 0).astype(F32)
        p_re, p_im = powers(j)
        row = lambda a, k: a[k:k + 1, :]
        w_in.append([cmul(bb_re, bb_im, row(p_re, k), row(p_im, k)) for k in range(S5_Q)])
        c_out.append([cmul(c_re, c_im, row(p_re, k), row(p_im, k)) for k in range(S5_Q + 1)])

        i8 = lax.broadcasted_iota(jnp.int32, (SUBLANES, LANES), 0)
        order = i8 if d == 0 else (SUBLANES - 1) - i8
        apow = powers((order * S5_Q).astype(F32))
        bc = lambda r: powers(jnp.full((SUBLANES, LANES), r * S5_Q, F32))
        scan_consts.append((apow, [bc(1), bc(2), bc(4)], bc(SUBLANES)))

    for e in range(2):
        sel = lambda a: jnp.where(in_group[e], a, 0.0)
        for s in range(S5_Q):
            fr, fi = w_in[0][S5_Q - 1 - s]
            br, bi = w_in[1][s]
            r0 = e * S5_K + s * B_CH
            bst_ref[r0:r0 + B_CH, :] = jnp.concatenate(
                [sel(fr), sel(fi), sel(br), sel(bi)], axis=1).astype(BF16)

    for e in range(2):
        sel = lambda a: jnp.where(in_group[e], a, 0.0)
        for t in range(S5_Q):
            fr, fi = c_out[0][t + 1]
            br, bi = c_out[1][S5_Q - t]
            cst_ref[e, t * B_CH:(t + 1) * B_CH, :] = jnp.concatenate(
                [sel(fr), sel(-fi), sel(br), sel(-bi)], axis=1).astype(BF16)

    for a in range(N_LAGS + 1):
        lag = a - (S5_Q - 1)
        f = w_in[0][lag] if 0 <= lag < S5_Q else (zero, zero)
        b = w_in[1][-lag] if -S5_Q < lag <= 0 else (zero, zero)
        lhs_ref[a * B_CH:(a + 1) * B_CH, :] = jnp.concatenate([f[0], f[1], b[0], b[1]], axis=1)
    cq_rows = []
    for e in range(2):
        sel = lambda a: jnp.where(in_group[e], a, 0.0)
        (fr, fi), (br, bi) = c_rows
        cq_rows.append(jnp.concatenate([sel(fr), sel(-fi), sel(br), sel(-bi)], axis=1))
    cq = jnp.concatenate(cq_rows + [jnp.zeros((LANES - 2 * B_CH, 4 * LANES), F32)], axis=0)
    kst = lax.dot_general(lhs_ref[...], cq, (((1,), (1,)), ((), ())),
                          preferred_element_type=F32, precision=HIGHEST)
    for e in range(2):
        g_cols = []
        for col in range((N_LAGS + 1) // GROUPS_PER_VREG):
            pieces = [(kst[(col * GROUPS_PER_VREG + k) * B_CH:(col * GROUPS_PER_VREG + k + 1) * B_CH, :], e)
                      for k in range(GROUPS_PER_VREG)]
            g_cols.append(_move_blocks(pieces, blk))
        for s in range(S5_Q):
            c0, sh = divmod((S5_Q - 1 - s) * B_CH, LANES)
            if sh == 0:
                cols = g_cols[c0:c0 + 2]
            else:
                rolled = [pltpu.roll(g_cols[c0 + k], LANES - sh, 1) for k in range(3)]
                cols = [jnp.where(lane < LANES - sh, rolled[k], rolled[k + 1]) for k in range(2)]
            toep_ref[e, s * B_CH:(s + 1) * B_CH, :] = jnp.concatenate(cols, axis=1).astype(BF16)

    x0 = xg_ref[0]
    x1 = xg_ref[1]
    sl_ref[...] = jnp.dot(jnp.concatenate([x0, x1], axis=1), bst_ref[...], preferred_element_type=F32)

    i8 = lax.broadcasted_iota(jnp.int32, (SUBLANES, LANES), 0)

    def scan_block(row0, col, carry, consts, fwd):
        (ap_re, ap_im), doubling, (a8_re, a8_im) = consts
        rows = pl.ds(pl.multiple_of(row0, SUBLANES), SUBLANES)
        p_re = sl_ref[rows, col:col + LANES]
        p_im = sl_ref[rows, col + LANES:col + 2 * LANES]

        def shifted(a, r):
            if fwd:
                return jnp.where(i8 >= r, pltpu.roll(a, r, 0), 0.0)
            return jnp.where(i8 < SUBLANES - r, pltpu.roll(a, SUBLANES - r, 0), 0.0)

        for r, (a_re, a_im) in zip((1, 2, 4), doubling):
            s_re, s_im = shifted(p_re, r), shifted(p_im, r)
            m_re, m_im = cmul(a_re, a_im, s_re, s_im)
            p_re, p_im = p_re + m_re, p_im + m_im
        c_re, c_im = carry
        e_re, e_im = cmul(ap_re, ap_im, c_re, c_im)
        sl_ref[rows, col:col + LANES] = shifted(p_re, 1) + e_re
        sl_ref[rows, col + LANES:col + 2 * LANES] = shifted(p_im, 1) + e_im
        last = SUBLANES - 1 if fwd else 0
        n_re, n_im = cmul(a8_re, a8_im, c_re, c_im)
        bcast = lambda a: jnp.broadcast_to(a[last:last + 1, :], (SUBLANES, LANES))
        return bcast(p_re) + n_re, bcast(p_im) + n_im

    def step(j, carry):
        jb = jnp.where(j < CTX_BLOCKS, CTX_BLOCKS - 1 - j, BLOCKS_PER_SAMPLE + CTX_BLOCKS - 1 - j)
        out = []
        for b in range(BATCH):
            sf, sb = carry[b]
            base = b * BLOCKS_PER_SAMPLE
            sf = scan_block((base + j) * SUBLANES, 0, sf, scan_consts[0], True)
            sb = scan_block((base + jb) * SUBLANES, 2 * LANES, sb, scan_consts[1], False)
            out.append((sf, sb))
        return tuple(out)

    z8 = jnp.zeros((SUBLANES, LANES), F32)
    lax.fori_loop(0, BLOCKS_PER_SAMPLE, step, tuple(((z8, z8), (z8, z8)) for _ in range(BATCH)))

    sp = sl_ref[...].astype(BF16)
    nt = (((1,), (1,)), ((), ()))
    yg_ref[0] = (jnp.dot(x0, toep_ref[0], preferred_element_type=F32)
                 + lax.dot_general(sp, cst_ref[0], nt, preferred_element_type=F32))
    yg_ref[1] = (jnp.dot(x1, toep_ref[1], preferred_element_type=F32)
                 + lax.dot_general(sp, cst_ref[1], nt, preferred_element_type=F32))


def _s5_params(lam_re, lam_im, log_dt, b_re, b_im, c_re, c_im):
    def rows(a):
        return a.astype(F32).reshape(2, N_PAIRS, 1, 2 * B_STATE).transpose(1, 0, 2, 3)

    dt = jnp.broadcast_to(log_dt[..., None], lam_re.shape)
    pad = jnp.zeros((N_PAIRS, 2, SUBLANES - 3, LANES), F32)
    prow = jnp.concatenate([rows(lam_re), rows(lam_im), rows(dt), pad], axis=2)

    def mats(a, channel_axis):
        a = a.astype(F32)
        if channel_axis == 3:
            a = a.transpose(0, 1, 3, 2)
        a = a.reshape(2, N_PAIRS, 2, B_CH, B_STATE).transpose(1, 0, 3, 2, 4)
        return a.reshape(N_PAIRS, 2, B_CH, 2 * B_STATE)

    pmat = jnp.stack([mats(b_re, 3), mats(b_im, 3), mats(c_re, 2), mats(c_im, 2)], axis=2)
    return prow, pmat


def _s5(xg, prow, pmat):
    return pl.pallas_call(
        _s5_kernel,
        grid=(N_PAIRS,),
        in_specs=[
            pl.BlockSpec((2, S5_ROWS, S5_K), lambda q: (q, 0, 0)),
            pl.BlockSpec((1, 2, SUBLANES, LANES), lambda q: (q, 0, 0, 0)),
            pl.BlockSpec((1, 2, 4, B_CH, LANES), lambda q: (q, 0, 0, 0, 0)),
        ],
        out_specs=pl.BlockSpec((2, S5_ROWS, S5_K), lambda q: (q, 0, 0)),
        out_shape=jax.ShapeDtypeStruct((B_GROUPS, S5_ROWS, S5_K), F32),
        scratch_shapes=[
            pltpu.VMEM((S5_ROWS, 4 * LANES), F32),
            pltpu.VMEM((2 * S5_K, 4 * LANES), BF16),
            pltpu.VMEM((2, S5_K, S5_K), BF16),
            pltpu.VMEM((2, S5_K, 4 * LANES), BF16),
            pltpu.VMEM(((N_LAGS + 1) * B_CH, 4 * LANES), F32),
        ],
        compiler_params=_params("parallel"),
        name="s5_chunked",
    )(xg, prow, pmat)


def _even_out_kernel(ya_ref, xs_ref, gb_ref, yg_ref, x_ref, mod_ref, d_ref, wglu_ref, bglu_ref,
                     wout_ref, perm_ref, o_ref):
    blk = _lane_block(TILE_CHUNKS)
    steps = []
    for t in range(S5_Q):
        j, src_blk = divmod(t, GROUPS_PER_VREG)
        cols = []
        for c in range(B_WIDTH // LANES):
            pieces = [(yg_ref[c * GROUPS_PER_VREG + k, :, j * LANES:(j + 1) * LANES], src_blk)
                      for k in range(GROUPS_PER_VREG)]
            cols.append(_move_blocks(pieces, blk))
        steps.append(jnp.concatenate(cols, axis=1))
    ys = jnp.dot(perm_ref[...].astype(F32), jnp.concatenate(steps, axis=0),
                 preferred_element_type=F32, precision=HIGHEST)

    y = ys + d_ref[...] * xs_ref[0]
    y = jax.nn.gelu(y)
    t = jnp.dot(y.astype(BF16), wglu_ref[...], preferred_element_type=F32) + bglu_ref[...]
    y = y * jax.nn.sigmoid(t)
    gb = gb_ref[0]
    yb = (y * (gb * jax.nn.sigmoid(gb))).astype(BF16)
    mix = (jnp.dot(ya_ref[0], wout_ref[0:A_WIDTH, :], preferred_element_type=F32)
           + jnp.dot(yb, wout_ref[A_WIDTH:, :], preferred_element_type=F32))
    gt = mod_ref[0][:, 2 * D_MODEL:]
    o_ref[0] = x_ref[0] + gt * mix


def _even_out(ya, xs, gb, yg, xc, mod_l, d_skip, w_glu, b_glu, w_out, perm):
    tile = lambda w: pl.BlockSpec((1, ROW_TILE, w), lambda b, i: (b, i, 0))
    full = lambda *s: pl.BlockSpec(s, lambda b, i: (0,) * len(s))
    return pl.pallas_call(
        _even_out_kernel,
        grid=(BATCH, N_TILES),
        in_specs=[
            tile(A_WIDTH), tile(B_WIDTH), tile(B_WIDTH),
            pl.BlockSpec((B_GROUPS, TILE_CHUNKS, S5_K), lambda b, i: (0, b * N_TILES + i, 0)),
            tile(D_MODEL),
            pl.BlockSpec((1, 1, 3 * D_MODEL), lambda b, i: (_mod_row(b, i), 0, 0)),
            full(1, B_WIDTH), full(B_WIDTH, B_WIDTH), full(1, B_WIDTH), full(D_MODEL, D_MODEL),
            full(ROW_TILE, ROW_TILE),
        ],
        out_specs=tile(D_MODEL),
        out_shape=jax.ShapeDtypeStruct((BATCH, NTOK, D_MODEL), F32),
        compiler_params=_params("parallel", "parallel"),
        name="even_out",
    )(ya, xs, gb, yg, xc, mod_l, d_skip, w_glu, b_glu, w_out, perm)


def _rope_tables():
    rows = SEQ // GRID_W
    row = jnp.repeat(jnp.arange(rows), GRID_W)
    col = jnp.tile(jnp.arange(GRID_W), rows)
    freqs = ROPE_THETA ** (-jnp.arange(ROPE_PAIRS, dtype=F32) / ROPE_PAIRS)
    ar = row[:, None] * freqs
    ac = col[:, None] * freqs
    zeros = jnp.zeros_like(ar)
    cos = jnp.concatenate([jnp.cos(ar), jnp.cos(ar), jnp.cos(ac), jnp.cos(ac)], axis=1)
    sa = jnp.concatenate([-jnp.sin(ar), zeros, -jnp.sin(ac), zeros], axis=1)
    sb = jnp.concatenate([zeros, jnp.sin(ar), zeros, jnp.sin(ac)], axis=1)
    pad = lambda t, v: jnp.concatenate([jnp.full((CTX_LEN, HEAD_DIM), v, F32), t], axis=0)
    return pad(cos, 1.0), pad(sa, 0.0), pad(sb, 0.0)


def _odd_in_kernel(x_ref, mod_ref, g_ref, win_ref, qg_ref, kg_ref, cos_ref, sa_ref, sb_ref,
                   q_ref, k_ref, v_ref, gate_ref):
    h = _prologue(x_ref[0], mod_ref[0], g_ref[...])
    z = jnp.dot(h.astype(BF16), win_ref[...], preferred_element_type=F32)
    cos = cos_ref[...]
    sa = sa_ref[...]
    sb = sb_ref[...]

    def norm_rope(xh, gain, scale):
        xn = (xh * lax.rsqrt(jnp.mean(xh * xh, axis=-1, keepdims=True) + EPS)) * gain
        xa = pltpu.roll(xn, HEAD_DIM - ROPE_PAIRS, 1)
        xb = pltpu.roll(xn, ROPE_PAIRS, 1)
        out = xn * cos + xa * sa + xb * sb
        return (out * scale).astype(BF16) if scale is not None else out.astype(BF16)

    for hq in range(N_Q):
        q_ref[0, :, hq * HEAD_DIM:(hq + 1) * HEAD_DIM] = norm_rope(
            z[:, hq * HEAD_DIM:(hq + 1) * HEAD_DIM], qg_ref[...], HEAD_DIM ** -0.5)
    for hk in range(N_KV):
        lo = C_WIDTH + hk * HEAD_DIM
        k_ref[0, :, hk * HEAD_DIM:(hk + 1) * HEAD_DIM] = norm_rope(z[:, lo:lo + HEAD_DIM], kg_ref[...], None)
    v_ref[0] = z[:, C_WIDTH + KV_WIDTH:C_WIDTH + 2 * KV_WIDTH].astype(BF16)
    gate_ref[0] = z[:, C_WIDTH + 2 * KV_WIDTH:]


def _odd_in(xc, mod_l, norm_g, w_in, q_g, k_g, cos, sa, sb):
    tile = lambda w: pl.BlockSpec((1, ROW_TILE, w), lambda b, i: (b, i, 0))
    full = lambda *s: pl.BlockSpec(s, lambda b, i: (0,) * len(s))
    tab = pl.BlockSpec((ROW_TILE, HEAD_DIM), lambda b, i: (i, 0))
    return pl.pallas_call(
        _odd_in_kernel,
        grid=(BATCH, N_TILES),
        in_specs=[
            tile(D_MODEL),
            pl.BlockSpec((1, 1, 3 * D_MODEL), lambda b, i: (_mod_row(b, i), 0, 0)),
            full(1, D_MODEL), full(D_MODEL, ODD_IN), full(1, HEAD_DIM), full(1, HEAD_DIM),
            tab, tab, tab,
        ],
        out_specs=[tile(C_WIDTH), tile(KV_WIDTH), tile(KV_WIDTH), tile(C_WIDTH)],
        out_shape=[
            jax.ShapeDtypeStruct((BATCH, NTOK, C_WIDTH), BF16),
            jax.ShapeDtypeStruct((BATCH, NTOK, KV_WIDTH), BF16),
            jax.ShapeDtypeStruct((BATCH, NTOK, KV_WIDTH), BF16),
            jax.ShapeDtypeStruct((BATCH, NTOK, C_WIDTH), F32),
        ],
        compiler_params=_params("parallel", "parallel"),
        name="odd_in",
    )(xc, mod_l, norm_g, w_in, q_g, k_g, cos, sa, sb)


def _attn_kernel(q_ref, k_ref, v_ref, o_ref, *, tile_offset):
    i = pl.program_id(2) + tile_offset

    def attend(n_keys):
        k = k_ref[0, 0:n_keys, :]
        v = v_ref[0, 0:n_keys, :]
        for h in range(Q_PER_KV):
            q = q_ref[0, :, h * HEAD_DIM:(h + 1) * HEAD_DIM]
            s = lax.dot_general(q, k, (((1,), (1,)), ((), ())), preferred_element_type=F32)
            p = jnp.exp(s - jnp.max(s, axis=-1, keepdims=True))
            l = jnp.sum(p, axis=-1, keepdims=True)
            o = jnp.dot(p.astype(BF16), v, preferred_element_type=F32) / l
            o_ref[0, :, h * HEAD_DIM:(h + 1) * HEAD_DIM] = o.astype(BF16)

    if tile_offset == 0:
        @pl.when(i == 0)
        def _():
            attend(CTX_LEN)

    @pl.when(i > 0)
    def _():
        attend(NTOK)


def _attention(q, k, v, tile_offset):
    n_tiles = N_TILES - tile_offset
    width = Q_PER_KV * HEAD_DIM
    qspec = pl.BlockSpec((1, ROW_TILE, width), lambda b, g, i: (b, i + tile_offset, g))
    kvspec = pl.BlockSpec((1, NTOK, HEAD_DIM), lambda b, g, i: (b, 0, g))
    return pl.pallas_call(
        functools.partial(_attn_kernel, tile_offset=tile_offset),
        grid=(BATCH, N_KV, n_tiles),
        in_specs=[qspec, kvspec, kvspec],
        out_specs=pl.BlockSpec((1, ROW_TILE, width), lambda b, g, i: (b, i, g)),
        out_shape=jax.ShapeDtypeStruct((BATCH, n_tiles * ROW_TILE, C_WIDTH), BF16),
        compiler_params=_params("parallel", "parallel", "arbitrary"),
        name="attention",
    )(q, k, v)


def _odd_out_kernel(o_ref, gate_ref, x_ref, mod_ref, wout_ref, fg_ref, out_ref, *, final):
    g = gate_ref[0]
    a = (o_ref[0].astype(F32) * (g * jax.nn.sigmoid(g))).astype(BF16)
    gt = mod_ref[0][:, 2 * D_MODEL:]
    x = x_ref[0] + gt * jnp.dot(a, wout_ref[...], preferred_element_type=F32)
    if final:
        x = (x * lax.rsqrt(jnp.mean(x * x, axis=-1, keepdims=True) + EPS)) * fg_ref[...]
    out_ref[0] = x


def _odd_out(o, gate, xc, mod_l, w_out, final_g, final):
    off = 1 if final else 0
    tile = lambda w: pl.BlockSpec((1, ROW_TILE, w), lambda b, i: (b, i + off, 0))
    full = lambda *s: pl.BlockSpec(s, lambda b, i: (0,) * len(s))
    return pl.pallas_call(
        functools.partial(_odd_out_kernel, final=final),
        grid=(BATCH, N_TILES - off),
        in_specs=[
            pl.BlockSpec((1, ROW_TILE, C_WIDTH), lambda b, i: (b, i, 0)),
            tile(C_WIDTH), tile(D_MODEL),
            pl.BlockSpec((1, 1, 3 * D_MODEL), lambda b, i: (_mod_row(b, i + off), 0, 0)),
            full(D_MODEL, D_MODEL), full(1, D_MODEL),
        ],
        out_specs=pl.BlockSpec((1, ROW_TILE, D_MODEL), lambda b, i: (b, i, 0)),
        out_shape=jax.ShapeDtypeStruct((BATCH, NTOK - off * ROW_TILE, D_MODEL), F32),
        compiler_params=_params("parallel", "parallel"),
        name="odd_out_final" if final else "odd_out",
    )(o, gate, xc, mod_l, w_out, final_g)


def kernel(x, c, ctx, c_ctx, norm_g, w_mod, b_mod, we_in, we_out, gm_v_g, gm_w_s, gm_b_s,
           s5_lam_re, s5_lam_im, s5_log_dt, s5_b_re, s5_b_im, s5_c_re, s5_c_im, s5_d,
           s5_w_glu, s5_b_glu, wo_in, wo_out, q_norm_g, k_norm_g, final_g):
    cond = jnp.concatenate([c, c_ctx[None], jnp.zeros((MOD_ROWS - BATCH - 1, D_MODEL), F32)], axis=0)
    mods = _modulation(cond, w_mod, b_mod).reshape(DEPTH, MOD_ROWS, 1, 3 * D_MODEL)
    xc = jnp.concatenate([ctx, x], axis=1)
    cos, sa, sb = _rope_tables()
    perm = _chunk_perm()
    row = lambda a: a.reshape(1, -1)

    for layer in range(DEPTH):
        i = layer // 2
        if layer % 2 == 0:
            b_full = jnp.repeat(gm_b_s[i].T, A_GROUP_W, axis=1)
            ya, xs, gb, xg = _even_in(xc, mods[layer], row(norm_g[layer]), we_in[i].astype(BF16),
                                      row(gm_v_g[i]), gm_w_s[i].astype(BF16), b_full, perm)
            prow, pmat = _s5_params(s5_lam_re[i], s5_lam_im[i], s5_log_dt[i], s5_b_re[i],
                                    s5_b_im[i], s5_c_re[i], s5_c_im[i])
            yg = _s5(xg, prow, pmat)
            xc = _even_out(ya, xs, gb, yg, xc, mods[layer], row(s5_d[i]), s5_w_glu[i].astype(BF16),
                           row(s5_b_glu[i]), we_out[i].astype(BF16), perm)
        else:
            final = layer == DEPTH - 1
            q, k, v, gate = _odd_in(xc, mods[layer], row(norm_g[layer]), wo_in[i].astype(BF16),
                                    row(q_norm_g[i]), row(k_norm_g[i]), cos, sa, sb)
            o = _attention(q, k, v, 1 if final else 0)
            xc = _odd_out(o, gate, xc, mods[layer], wo_out[i].astype(BF16), row(final_g), final)
    return xc
```

```python
import functools
import math

import jax
import jax.numpy as jnp
from jax import lax
from jax.experimental import pallas as pl
from jax.experimental.pallas import tpu as pltpu

F32 = jnp.float32
BF16 = jnp.bfloat16
HIGHEST = lax.Precision.HIGHEST

D_MODEL = 1024
BATCH = 4
SEQ = 4096
DEPTH = 4
GRID_W = 64
CTX_LEN = 256
EPS = 1e-6
NTOK = CTX_LEN + SEQ

LANES = 128
SUBLANES = 8
ROW_TILE = 256
N_TILES = NTOK // ROW_TILE
VMEM_LIMIT = 56 * 1024 * 1024

CHUNK = 128
A_WIDTH = D_MODEL // 2
A_GROUPS = 4
A_GROUP_W = A_WIDTH // A_GROUPS
B_WIDTH = D_MODEL // 2
B_CH = 16
B_GROUPS = B_WIDTH // B_CH
B_STATE = 64
EVEN_IN = 3 * A_WIDTH + 2 * B_WIDTH
HEAD_DIM = 128
N_Q = D_MODEL // HEAD_DIM
N_KV = 2
Q_PER_KV = N_Q // N_KV
C_WIDTH = N_Q * HEAD_DIM
KV_WIDTH = N_KV * HEAD_DIM
ODD_IN = 2 * C_WIDTH + 2 * KV_WIDTH
ROPE_THETA = 10000.0
ROPE_PAIRS = HEAD_DIM // 4
Q_SCALE = HEAD_DIM ** -0.5 * math.log2(math.e)
V_EXT = 2 * HEAD_DIM

S5_Q = 16
S5_K = S5_Q * B_CH
N_CHUNKS = NTOK // S5_Q
S5_ROWS = BATCH * N_CHUNKS
TILE_CHUNKS = ROW_TILE // S5_Q
BLOCKS_PER_SAMPLE = N_CHUNKS // SUBLANES
CTX_BLOCKS = CTX_LEN // S5_Q // SUBLANES
N_PAIRS = B_GROUPS // 2
GROUPS_PER_VREG = LANES // B_CH
N_LAGS = 2 * S5_Q - 1

MOD_ROWS = 8
CTX_ROW = BATCH


def _params(*sem):
    return pltpu.CompilerParams(dimension_semantics=sem, vmem_limit_bytes=VMEM_LIMIT)


def _mod_row(b, i):
    return jnp.where(i == 0, CTX_ROW, b)


def _lane_block(rows):
    return lax.broadcasted_iota(jnp.int32, (rows, LANES), 1) // B_CH


def _move_blocks(pieces, blk):
    acc = None
    for dst, (src, src_blk) in enumerate(pieces):
        sh = ((dst - src_blk) % GROUPS_PER_VREG) * B_CH
        rolled = pltpu.roll(src, sh, 1) if sh else src
        acc = rolled if acc is None else jnp.where(blk == dst, rolled, acc)
    return acc


def _mod_kernel(cond_ref, w_ref, b_ref, o_ref):
    c = cond_ref[...]
    s = c * jax.nn.sigmoid(c)
    o_ref[0] = jnp.dot(s, w_ref[0], preferred_element_type=F32, precision=HIGHEST) + b_ref[0]


def _modulation(cond, w_mod, b_mod):
    nblk = 3 * D_MODEL // D_MODEL
    return pl.pallas_call(
        _mod_kernel,
        grid=(DEPTH, nblk),
        in_specs=[
            pl.BlockSpec((MOD_ROWS, D_MODEL), lambda l, j: (0, 0)),
            pl.BlockSpec((1, D_MODEL, D_MODEL), lambda l, j: (l, 0, j)),
            pl.BlockSpec((1, 1, D_MODEL), lambda l, j: (l, 0, j)),
        ],
        out_specs=pl.BlockSpec((1, MOD_ROWS, D_MODEL), lambda l, j: (l, 0, j)),
        out_shape=jax.ShapeDtypeStruct((DEPTH, MOD_ROWS, 3 * D_MODEL), F32),
        compiler_params=_params("parallel", "parallel"),
        name="modulation",
    )(cond, w_mod, b_mod.reshape(DEPTH, 1, 3 * D_MODEL))


def _prologue(x, mod, g):
    sh = mod[:, :D_MODEL]
    sc = mod[:, D_MODEL:2 * D_MODEL]
    y = x * lax.rsqrt(jnp.mean(x * x, axis=-1, keepdims=True) + EPS)
    return (y * g) * (1 + sc) + sh


def _chunk_perm():
    r = jnp.arange(ROW_TILE)
    swapped = (r % TILE_CHUNKS) * S5_Q + r // TILE_CHUNKS
    return (r[:, None] == swapped[None, :]).astype(BF16)


def _even_in_kernel(x_ref, mod_ref, g_ref, win_ref, vg_ref, ws_ref, bs_ref, perm_ref,
                    ya_ref, xs_ref, gb_ref, xg_ref):
    h = _prologue(x_ref[0], mod_ref[0], g_ref[...])
    z = jnp.dot(h.astype(BF16), win_ref[...], preferred_element_type=F32)
    u = z[:, 0:A_WIDTH]
    v = z[:, A_WIDTH:2 * A_WIDTH]
    ga = z[:, 2 * A_WIDTH:3 * A_WIDTH]
    xs = z[:, 3 * A_WIDTH:3 * A_WIDTH + B_WIDTH]
    xs_ref[0] = xs
    gb_ref[0] = z[:, 3 * A_WIDTH + B_WIDTH:]
    mu = jnp.mean(v, axis=-1, keepdims=True)
    vc = v - mu
    var = jnp.mean(vc * vc, axis=-1, keepdims=True)
    vn = ((vc * lax.rsqrt(var + EPS)) * vg_ref[...]).astype(BF16)
    rows = []
    for c in range(ROW_TILE // CHUNK):
        cols = []
        for g in range(A_GROUPS):
            blk = vn[c * CHUNK:(c + 1) * CHUNK, g * A_GROUP_W:(g + 1) * A_GROUP_W]
            cols.append(jnp.dot(ws_ref[g], blk, preferred_element_type=F32))
        rows.append(jnp.concatenate(cols, axis=1) + bs_ref[...])
    mixed = jnp.concatenate(rows, axis=0)
    ya_ref[0] = ((u * mixed) * (ga * jax.nn.sigmoid(ga))).astype(BF16)

    r = jnp.dot(perm_ref[...], xs.astype(BF16), preferred_element_type=F32)
    blk = _lane_block(TILE_CHUNKS)
    for g in range(B_GROUPS):
        col, src_blk = divmod(g, GROUPS_PER_VREG)
        halves = []
        for j in range(S5_K // LANES):
            pieces = [(r[(j * GROUPS_PER_VREG + k) * TILE_CHUNKS:(j * GROUPS_PER_VREG + k + 1) * TILE_CHUNKS,
                         col * LANES:(col + 1) * LANES], src_blk) for k in range(GROUPS_PER_VREG)]
            halves.append(_move_blocks(pieces, blk))
        xg_ref[g] = jnp.concatenate(halves, axis=1).astype(BF16)


def _even_in(xc, mod_l, norm_g, w_in, v_g, w_s, b_full, perm):
    tile = lambda w: pl.BlockSpec((1, ROW_TILE, w), lambda b, i: (b, i, 0))
    full = lambda *s: pl.BlockSpec(s, lambda b, i: (0,) * len(s))
    return pl.pallas_call(
        _even_in_kernel,
        grid=(BATCH, N_TILES),
        in_specs=[
            tile(D_MODEL),
            pl.BlockSpec((1, 1, 3 * D_MODEL), lambda b, i: (_mod_row(b, i), 0, 0)),
            full(1, D_MODEL),
            full(D_MODEL, EVEN_IN),
            full(1, A_WIDTH),
            full(A_GROUPS, CHUNK, CHUNK),
            full(CHUNK, A_WIDTH),
            full(ROW_TILE, ROW_TILE),
        ],
        out_specs=[tile(A_WIDTH), tile(B_WIDTH), tile(B_WIDTH),
                   pl.BlockSpec((B_GROUPS, TILE_CHUNKS, S5_K), lambda b, i: (0, b * N_TILES + i, 0))],
        out_shape=[
            jax.ShapeDtypeStruct((BATCH, NTOK, A_WIDTH), BF16),
            jax.ShapeDtypeStruct((BATCH, NTOK, B_WIDTH), F32),
            jax.ShapeDtypeStruct((BATCH, NTOK, B_WIDTH), F32),
            jax.ShapeDtypeStruct((B_GROUPS, S5_ROWS, S5_K), BF16),
        ],
        compiler_params=_params("parallel", "parallel"),
        name="even_in",
    )(xc, mod_l, norm_g, w_in, v_g, w_s, b_full, perm)


def _s5_kernel(xg_ref, prow_ref, pmat_ref, yg_ref, sl_ref, bst_ref, toep_ref, cst_ref, lhs_ref):
    lane = lax.broadcasted_iota(jnp.int32, (S5_Q, LANES), 1)
    in_group = (lane < B_STATE, lane >= B_STATE)
    blk = lane // B_CH
    zero = jnp.zeros((S5_Q, LANES), F32)

    def cmul(ar, ai, br, bi):
        return ar * br - ai * bi, ar * bi + ai * br

    w_in, c_out, scan_consts, c_rows = [], [], [], []
    for d in range(2):
        lam_re = prow_ref[0, d, 0:1, :]
        lam_im = prow_ref[0, d, 1:2, :]
        dt = jnp.exp(prow_ref[0, d, 2:3, :])
        z_re, z_im = lam_re * dt, lam_im * dt
        mag = jnp.exp(z_re)
        lb_re, lb_im = mag * jnp.cos(z_im), mag * jnp.sin(z_im)
        den = lam_re * lam_re + lam_im * lam_im
        n_re, n_im = lb_re - 1.0, lb_im
        f_re = (n_re * lam_re + n_im * lam_im) / den
        f_im = (n_im * lam_re - n_re * lam_im) / den
        b_re, b_im = pmat_ref[0, d, 0], pmat_ref[0, d, 1]
        c_re, c_im = pmat_ref[0, d, 2], pmat_ref[0, d, 3]
        bb_re, bb_im = cmul(f_re, f_im, b_re, b_im)
        c_rows.append((c_re, c_im))

        def powers(steps):
            m = jnp.exp(steps * z_re)
            return m * jnp.cos(steps * z_im), m * jnp.sin(steps * z_im)

        j = lax.broadcasted_iota(jnp.int32, (3 * SUBLANES, LANES), 0).astype(F32)
        p_re, p_im = powers(j)
        row = lambda a, k: a[k:k + 1, :]
        w_in.append([cmul(bb_re, bb_im, row(p_re, k), row(p_im, k)) for k in range(S5_Q)])
        c_out.append([cmul(c_re, c_im, row(p_re, k), row(p_im, k)) for k in range(S5_Q + 1)])

        i8 = lax.broadcasted_iota(jnp.int32, (SUBLANES, LANES), 0)
        order = i8 if d == 0 else (SUBLANES - 1) - i8
        apow = powers((order * S5_Q).astype(F32))
        bc = lambda r: powers(jnp.full((SUBLANES, LANES), r * S5_Q, F32))
        scan_consts.append((apow, [bc(1), bc(2), bc(4)], bc(SUBLANES)))

    for e in range(2):
        sel = lambda a: jnp.where(in_group[e], a, 0.0)
        for s in range(S5_Q):
            fr, fi = w_in[0][S5_Q - 1 - s]
            br, bi = w_in[1][s]
            r0 = e * S5_K + s * B_CH
            bst_ref[r0:r0 + B_CH, :] = jnp.concatenate(
                [sel(fr), sel(fi), sel(br), sel(bi)], axis=1).astype(BF16)

    for e in range(2):
        sel = lambda a: jnp.where(in_group[e], a, 0.0)
        for t in range(S5_Q):
            fr, fi = c_out[0][t + 1]
            br, bi = c_out[1][S5_Q - t]
            cst_ref[e, t * B_CH:(t + 1) * B_CH, :] = jnp.concatenate(
                [sel(fr), sel(-fi), sel(br), sel(-bi)], axis=1).astype(BF16)

    for a in range(N_LAGS + 1):
        lag = a - (S5_Q - 1)
        f = w_in[0][lag] if 0 <= lag < S5_Q else (zero, zero)
        b = w_in[1][-lag] if -S5_Q < lag <= 0 else (zero, zero)
        lhs_ref[a * B_CH:(a + 1) * B_CH, :] = jnp.concatenate([f[0], f[1], b[0], b[1]], axis=1)
    cq_rows = []
    for e in range(2):
        sel = lambda a: jnp.where(in_group[e], a, 0.0)
        (fr, fi), (br, bi) = c_rows
        cq_rows.append(jnp.concatenate([sel(fr), sel(-fi), sel(br), sel(-bi)], axis=1))
    cq = jnp.concatenate(cq_rows + [jnp.zeros((LANES - 2 * B_CH, 4 * LANES), F32)], axis=0)
    kst = lax.dot_general(lhs_ref[...], cq, (((1,), (1,)), ((), ())),
                          preferred_element_type=F32, precision=HIGHEST)
    for e in range(2):
        g_cols = []
        for col in range((N_LAGS + 1) // GROUPS_PER_VREG):
            pieces = [(kst[(col * GROUPS_PER_VREG + k) * B_CH:(col * GROUPS_PER_VREG + k + 1) * B_CH, :], e)
                      for k in range(GROUPS_PER_VREG)]
            g_cols.append(_move_blocks(pieces, blk))
        for s in range(S5_Q):
            c0, sh = divmod((S5_Q - 1 - s) * B_CH, LANES)
            if sh == 0:
                cols = g_cols[c0:c0 + 2]
            else:
                rolled = [pltpu.roll(g_cols[c0 + k], LANES - sh, 1) for k in range(3)]
                cols = [jnp.where(lane < LANES - sh, rolled[k], rolled[k + 1]) for k in range(2)]
            toep_ref[e, s * B_CH:(s + 1) * B_CH, :] = jnp.concatenate(cols, axis=1).astype(BF16)

    x0 = xg_ref[0]
    x1 = xg_ref[1]
    sl_ref[...] = jnp.dot(jnp.concatenate([x0, x1], axis=1), bst_ref[...], preferred_element_type=F32)

    i8 = lax.broadcasted_iota(jnp.int32, (SUBLANES, LANES), 0)

    def scan_block(row0, col, carry, consts, fwd):
        (ap_re, ap_im), doubling, (a8_re, a8_im) = consts
        rows = pl.ds(pl.multiple_of(row0, SUBLANES), SUBLANES)
        p_re = sl_ref[rows, col:col + LANES]
        p_im = sl_ref[rows, col + LANES:col + 2 * LANES]

        def shifted(a, r):
            if fwd:
                return jnp.where(i8 >= r, pltpu.roll(a, r, 0), 0.0)
            return jnp.where(i8 < SUBLANES - r, pltpu.roll(a, SUBLANES - r, 0), 0.0)

        for r, (a_re, a_im) in zip((1, 2, 4), doubling):
            s_re, s_im = shifted(p_re, r), shifted(p_im, r)
            m_re, m_im = cmul(a_re, a_im, s_re, s_im)
            p_re, p_im = p_re + m_re, p_im + m_im
        c_re, c_im = carry
        e_re, e_im = cmul(ap_re, ap_im, c_re, c_im)
        sl_ref[rows, col:col + LANES] = shifted(p_re, 1) + e_re
        sl_ref[rows, col + LANES:col + 2 * LANES] = shifted(p_im, 1) + e_im
        last = SUBLANES - 1 if fwd else 0
        n_re, n_im = cmul(a8_re, a8_im, c_re, c_im)
        bcast = lambda a: jnp.broadcast_to(a[last:last + 1, :], (SUBLANES, LANES))
        return bcast(p_re) + n_re, bcast(p_im) + n_im

    def step(j, carry):
        jb = jnp.where(j < CTX_BLOCKS, CTX_BLOCKS - 1 - j, BLOCKS_PER_SAMPLE + CTX_BLOCKS - 1 - j)
        out = []
        for b in range(BATCH):
            sf, sb = carry[b]
            base = b * BLOCKS_PER_SAMPLE
            sf = scan_block((base + j) * SUBLANES, 0, sf, scan_consts[0], True)
            sb = scan_block((base + jb) * SUBLANES, 2 * LANES, sb, scan_consts[1], False)
            out.append((sf, sb))
        return tuple(out)

    z8 = jnp.zeros((SUBLANES, LANES), F32)
    lax.fori_loop(0, BLOCKS_PER_SAMPLE, step, tuple(((z8, z8), (z8, z8)) for _ in range(BATCH)))

    sp = sl_ref[...].astype(BF16)
    nt = (((1,), (1,)), ((), ()))
    yg_ref[0] = (jnp.dot(x0, toep_ref[0], preferred_element_type=F32)
                 + lax.dot_general(sp, cst_ref[0], nt, preferred_element_type=F32))
    yg_ref[1] = (jnp.dot(x1, toep_ref[1], preferred_element_type=F32)
                 + lax.dot_general(sp, cst_ref[1], nt, preferred_element_type=F32))


def _s5_params(lam_re, lam_im, log_dt, b_re, b_im, c_re, c_im):
    def rows(a):
        return a.astype(F32).reshape(2, N_PAIRS, 1, 2 * B_STATE).transpose(1, 0, 2, 3)

    dt = jnp.broadcast_to(log_dt[..., None], lam_re.shape)
    pad = jnp.zeros((N_PAIRS, 2, SUBLANES - 3, LANES), F32)
    prow = jnp.concatenate([rows(lam_re), rows(lam_im), rows(dt), pad], axis=2)

    def mats(a, channel_axis):
        a = a.astype(F32)
        if channel_axis == 3:
            a = a.transpose(0, 1, 3, 2)
        a = a.reshape(2, N_PAIRS, 2, B_CH, B_STATE).transpose(1, 0, 3, 2, 4)
        return a.reshape(N_PAIRS, 2, B_CH, 2 * B_STATE)

    pmat = jnp.stack([mats(b_re, 3), mats(b_im, 3), mats(c_re, 2), mats(c_im, 2)], axis=2)
    return prow, pmat


def _s5(xg, prow, pmat):
    return pl.pallas_call(
        _s5_kernel,
        grid=(N_PAIRS,),
        in_specs=[
            pl.BlockSpec((2, S5_ROWS, S5_K), lambda q: (q, 0, 0)),
            pl.BlockSpec((1, 2, SUBLANES, LANES), lambda q: (q, 0, 0, 0)),
            pl.BlockSpec((1, 2, 4, B_CH, LANES), lambda q: (q, 0, 0, 0, 0)),
        ],
        out_specs=pl.BlockSpec((2, S5_ROWS, S5_K), lambda q: (q, 0, 0)),
        out_shape=jax.ShapeDtypeStruct((B_GROUPS, S5_ROWS, S5_K), F32),
        scratch_shapes=[
            pltpu.VMEM((S5_ROWS, 4 * LANES), F32),
            pltpu.VMEM((2 * S5_K, 4 * LANES), BF16),
            pltpu.VMEM((2, S5_K, S5_K), BF16),
            pltpu.VMEM((2, S5_K, 4 * LANES), BF16),
            pltpu.VMEM(((N_LAGS + 1) * B_CH, 4 * LANES), F32),
        ],
        compiler_params=_params("parallel"),
        name="s5_chunked",
    )(xg, prow, pmat)


def _even_out_kernel(ya_ref, xs_ref, gb_ref, yg_ref, x_ref, mod_ref, d_ref, wglu_ref, bglu_ref,
                     wout_ref, perm_ref, o_ref):
    blk = _lane_block(TILE_CHUNKS)
    steps = []
    for t in range(S5_Q):
        j, src_blk = divmod(t, GROUPS_PER_VREG)
        cols = []
        for c in range(B_WIDTH // LANES):
            pieces = [(yg_ref[c * GROUPS_PER_VREG + k, :, j * LANES:(j + 1) * LANES], src_blk)
                      for k in range(GROUPS_PER_VREG)]
            cols.append(_move_blocks(pieces, blk))
        steps.append(jnp.concatenate(cols, axis=1))
    ys = jnp.dot(perm_ref[...].astype(F32), jnp.concatenate(steps, axis=0),
                 preferred_element_type=F32, precision=HIGHEST)

    y = ys + d_ref[...] * xs_ref[0]
    y = jax.nn.gelu(y)
    t = jnp.dot(y.astype(BF16), wglu_ref[...], preferred_element_type=F32) + bglu_ref[...]
    y = y * jax.nn.sigmoid(t)
    gb = gb_ref[0]
    yb = (y * (gb * jax.nn.sigmoid(gb))).astype(BF16)
    mix = (jnp.dot(ya_ref[0], wout_ref[0:A_WIDTH, :], preferred_element_type=F32)
           + jnp.dot(yb, wout_ref[A_WIDTH:, :], preferred_element_type=F32))
    gt = mod_ref[0][:, 2 * D_MODEL:]
    o_ref[0] = x_ref[0] + gt * mix


def _even_out(ya, xs, gb, yg, xc, mod_l, d_skip, w_glu, b_glu, w_out, perm):
    tile = lambda w: pl.BlockSpec((1, ROW_TILE, w), lambda b, i: (b, i, 0))
    full = lambda *s: pl.BlockSpec(s, lambda b, i: (0,) * len(s))
    return pl.pallas_call(
        _even_out_kernel,
        grid=(BATCH, N_TILES),
        in_specs=[
            tile(A_WIDTH), tile(B_WIDTH), tile(B_WIDTH),
            pl.BlockSpec((B_GROUPS, TILE_CHUNKS, S5_K), lambda b, i: (0, b * N_TILES + i, 0)),
            tile(D_MODEL),
            pl.BlockSpec((1, 1, 3 * D_MODEL), lambda b, i: (_mod_row(b, i), 0, 0)),
            full(1, B_WIDTH), full(B_WIDTH, B_WIDTH), full(1, B_WIDTH), full(D_MODEL, D_MODEL),
            full(ROW_TILE, ROW_TILE),
        ],
        out_specs=tile(D_MODEL),
        out_shape=jax.ShapeDtypeStruct((BATCH, NTOK, D_MODEL), F32),
        compiler_params=_params("parallel", "parallel"),
        name="even_out",
    )(ya, xs, gb, yg, xc, mod_l, d_skip, w_glu, b_glu, w_out, perm)


def _rope_tables():
    rows = SEQ // GRID_W
    row = jnp.repeat(jnp.arange(rows), GRID_W)
    col = jnp.tile(jnp.arange(GRID_W), rows)
    freqs = ROPE_THETA ** (-jnp.arange(ROPE_PAIRS, dtype=F32) / ROPE_PAIRS)
    ar = row[:, None] * freqs
    ac = col[:, None] * freqs
    zeros = jnp.zeros_like(ar)
    cos = jnp.concatenate([jnp.cos(ar), jnp.cos(ar), jnp.cos(ac), jnp.cos(ac)], axis=1)
    sa = jnp.concatenate([-jnp.sin(ar), zeros, -jnp.sin(ac), zeros], axis=1)
    sb = jnp.concatenate([zeros, jnp.sin(ar), zeros, jnp.sin(ac)], axis=1)
    pad = lambda t, v: jnp.concatenate([jnp.full((CTX_LEN, HEAD_DIM), v, F32), t], axis=0)
    return pad(cos, 1.0), pad(sa, 0.0), pad(sb, 0.0)


def _odd_in_kernel(x_ref, mod_ref, g_ref, win_ref, qg_ref, kg_ref, cos_ref, sa_ref, sb_ref,
                   q_ref, k_ref, v_ref, gate_ref):
    h = _prologue(x_ref[0], mod_ref[0], g_ref[...])
    z = jnp.dot(h.astype(BF16), win_ref[...], preferred_element_type=F32)
    cos = cos_ref[...]
    sa = sa_ref[...]
    sb = sb_ref[...]

    def norm_rope(xh, gain, scale):
        xn = (xh * lax.rsqrt(jnp.mean(xh * xh, axis=-1, keepdims=True) + EPS)) * gain
        xa = pltpu.roll(xn, HEAD_DIM - ROPE_PAIRS, 1)
        xb = pltpu.roll(xn, ROPE_PAIRS, 1)
        out = xn * cos + xa * sa + xb * sb
        return (out * scale).astype(BF16) if scale is not None else out.astype(BF16)

    for hq in range(N_Q):
        q_ref[0, :, hq * HEAD_DIM:(hq + 1) * HEAD_DIM] = norm_rope(
            z[:, hq * HEAD_DIM:(hq + 1) * HEAD_DIM], qg_ref[...], Q_SCALE)
    ones = jnp.ones((ROW_TILE, HEAD_DIM), BF16)
    for hk in range(N_KV):
        lo = C_WIDTH + hk * HEAD_DIM
        k_ref[0, :, hk * HEAD_DIM:(hk + 1) * HEAD_DIM] = norm_rope(z[:, lo:lo + HEAD_DIM], kg_ref[...], None)
        lo = C_WIDTH + KV_WIDTH + hk * HEAD_DIM
        v_ref[0, :, hk * V_EXT:hk * V_EXT + HEAD_DIM] = z[:, lo:lo + HEAD_DIM].astype(BF16)
        v_ref[0, :, hk * V_EXT + HEAD_DIM:(hk + 1) * V_EXT] = ones
    gate_ref[0] = z[:, C_WIDTH + 2 * KV_WIDTH:]


def _odd_in(xc, mod_l, norm_g, w_in, q_g, k_g, cos, sa, sb):
    tile = lambda w: pl.BlockSpec((1, ROW_TILE, w), lambda b, i: (b, i, 0))
    full = lambda *s: pl.BlockSpec(s, lambda b, i: (0,) * len(s))
    tab = pl.BlockSpec((ROW_TILE, HEAD_DIM), lambda b, i: (i, 0))
    return pl.pallas_call(
        _odd_in_kernel,
        grid=(BATCH, N_TILES),
        in_specs=[
            tile(D_MODEL),
            pl.BlockSpec((1, 1, 3 * D_MODEL), lambda b, i: (_mod_row(b, i), 0, 0)),
            full(1, D_MODEL), full(D_MODEL, ODD_IN), full(1, HEAD_DIM), full(1, HEAD_DIM),
            tab, tab, tab,
        ],
        out_specs=[tile(C_WIDTH), tile(KV_WIDTH), tile(N_KV * V_EXT), tile(C_WIDTH)],
        out_shape=[
            jax.ShapeDtypeStruct((BATCH, NTOK, C_WIDTH), BF16),
            jax.ShapeDtypeStruct((BATCH, NTOK, KV_WIDTH), BF16),
            jax.ShapeDtypeStruct((BATCH, NTOK, N_KV * V_EXT), BF16),
            jax.ShapeDtypeStruct((BATCH, NTOK, C_WIDTH), F32),
        ],
        compiler_params=_params("parallel", "parallel"),
        name="odd_in",
    )(xc, mod_l, norm_g, w_in, q_g, k_g, cos, sa, sb)


_NT = (((1,), (1,)), ((), ()))


def _softmax_weights(q, k):
    s = lax.dot_general(q, k, _NT, preferred_element_type=F32)
    return jnp.exp2(s - jnp.max(s, axis=-1, keepdims=True)).astype(BF16)


def _weighted_values(p, v):
    ov = jnp.dot(p, v, preferred_element_type=F32)
    return (ov[:, :HEAD_DIM] / ov[:, HEAD_DIM:HEAD_DIM + 1]).astype(BF16)


def _attn_lat_kernel(q_ref, k_ref, v_ref, o_ref, p_ref):
    @pl.when(pl.program_id(2) == 0)
    def _():
        p_ref[...] = jnp.ones(p_ref.shape, BF16)

    k = k_ref[0]
    v = v_ref[0]
    for h in range(Q_PER_KV):
        cols = slice(h * HEAD_DIM, (h + 1) * HEAD_DIM)
        p_new = _softmax_weights(q_ref[0, :, cols], k)
        o_ref[0, :, cols] = _weighted_values(p_ref[h], v)
        p_ref[h] = p_new


def _attention_latent(q, k, v):
    n_tiles = N_TILES - 1
    width = Q_PER_KV * HEAD_DIM
    return pl.pallas_call(
        _attn_lat_kernel,
        grid=(BATCH, N_KV, n_tiles + 1),
        in_specs=[
            pl.BlockSpec((1, ROW_TILE, width), lambda b, g, i: (b, jnp.minimum(i, n_tiles - 1) + 1, g)),
            pl.BlockSpec((1, NTOK, HEAD_DIM), lambda b, g, i: (b, 0, g)),
            pl.BlockSpec((1, NTOK, V_EXT), lambda b, g, i: (b, 0, g)),
        ],
        out_specs=pl.BlockSpec((1, ROW_TILE, width), lambda b, g, i: (b, jnp.maximum(i - 1, 0), g)),
        out_shape=jax.ShapeDtypeStruct((BATCH, SEQ, C_WIDTH), BF16),
        scratch_shapes=[pltpu.VMEM((Q_PER_KV, ROW_TILE, NTOK), BF16)],
        compiler_params=_params("parallel", "parallel", "arbitrary"),
        name="attention_latent",
    )(q, k, v)


def _attn_ctx_kernel(q_ref, k_ref, v_ref, o_ref):
    for h in range(N_Q):
        g = h // Q_PER_KV
        p = _softmax_weights(q_ref[0, :, h * HEAD_DIM:(h + 1) * HEAD_DIM],
                             k_ref[0, :, g * HEAD_DIM:(g + 1) * HEAD_DIM])
        o_ref[0, :, h * HEAD_DIM:(h + 1) * HEAD_DIM] = _weighted_values(
            p, v_ref[0, :, g * V_EXT:(g + 1) * V_EXT])


def _attention_context(q, k, v):
    spec = lambda w: pl.BlockSpec((1, CTX_LEN, w), lambda b: (b, 0, 0))
    return pl.pallas_call(
        _attn_ctx_kernel,
        grid=(BATCH,),
        in_specs=[spec(C_WIDTH), spec(KV_WIDTH), spec(N_KV * V_EXT)],
        out_specs=spec(C_WIDTH),
        out_shape=jax.ShapeDtypeStruct((BATCH, CTX_LEN, C_WIDTH), BF16),
        compiler_params=_params("parallel"),
        name="attention_context",
    )(q, k, v)


def _odd_out_kernel(olat_ref, *refs, final):
    if final:
        o = olat_ref[0]
    else:
        octx_ref, *refs = refs
        o = jnp.where(pl.program_id(1) == 0, octx_ref[0], olat_ref[0])
    gate_ref, x_ref, mod_ref, wout_ref, fg_ref, out_ref = refs
    g = gate_ref[0]
    a = (o.astype(F32) * (g * jax.nn.sigmoid(g))).astype(BF16)
    gt = mod_ref[0][:, 2 * D_MODEL:]
    x = x_ref[0] + gt * jnp.dot(a, wout_ref[...], preferred_element_type=F32)
    if final:
        x = (x * lax.rsqrt(jnp.mean(x * x, axis=-1, keepdims=True) + EPS)) * fg_ref[...]
    out_ref[0] = x


def _odd_out(o_lat, o_ctx, gate, xc, mod_l, w_out, final_g):
    final = o_ctx is None
    off = 1 if final else 0
    tile = lambda w: pl.BlockSpec((1, ROW_TILE, w), lambda b, i: (b, i + off, 0))
    full = lambda *s: pl.BlockSpec(s, lambda b, i: (0,) * len(s))
    lat = pl.BlockSpec((1, ROW_TILE, C_WIDTH), lambda b, i: (b, jnp.maximum(i + off - 1, 0), 0))
    ctx = [] if final else [pl.BlockSpec((1, CTX_LEN, C_WIDTH), lambda b, i: (b, 0, 0))]
    return pl.pallas_call(
        functools.partial(_odd_out_kernel, final=final),
        grid=(BATCH, N_TILES - off),
        in_specs=[
            lat, *ctx, tile(C_WIDTH), tile(D_MODEL),
            pl.BlockSpec((1, 1, 3 * D_MODEL), lambda b, i: (_mod_row(b, i + off), 0, 0)),
            full(D_MODEL, D_MODEL), full(1, D_MODEL),
        ],
        out_specs=pl.BlockSpec((1, ROW_TILE, D_MODEL), lambda b, i: (b, i, 0)),
        out_shape=jax.ShapeDtypeStruct((BATCH, NTOK - off * ROW_TILE, D_MODEL), F32),
        compiler_params=_params("parallel", "parallel"),
        name="odd_out_final" if final else "odd_out",
    )(o_lat, *([] if final else [o_ctx]), gate, xc, mod_l, w_out, final_g)


def kernel(x, c, ctx, c_ctx, norm_g, w_mod, b_mod, we_in, we_out, gm_v_g, gm_w_s, gm_b_s,
           s5_lam_re, s5_lam_im, s5_log_dt, s5_b_re, s5_b_im, s5_c_re, s5_c_im, s5_d,
           s5_w_glu, s5_b_glu, wo_in, wo_out, q_norm_g, k_norm_g, final_g):
    cond = jnp.concatenate([c, c_ctx[None], jnp.zeros((MOD_ROWS - BATCH - 1, D_MODEL), F32)], axis=0)
    mods = _modulation(cond, w_mod, b_mod).reshape(DEPTH, MOD_ROWS, 1, 3 * D_MODEL)
    xc = jnp.concatenate([ctx, x], axis=1)
    cos, sa, sb = _rope_tables()
    perm = _chunk_perm()
    row = lambda a: a.reshape(1, -1)

    for layer in range(DEPTH):
        i = layer // 2
        if layer % 2 == 0:
            b_full = jnp.repeat(gm_b_s[i].T, A_GROUP_W, axis=1)
            ya, xs, gb, xg = _even_in(xc, mods[layer], row(norm_g[layer]), we_in[i].astype(BF16),
                                      row(gm_v_g[i]), gm_w_s[i].astype(BF16), b_full, perm)
            prow, pmat = _s5_params(s5_lam_re[i], s5_lam_im[i], s5_log_dt[i], s5_b_re[i],
                                    s5_b_im[i], s5_c_re[i], s5_c_im[i])
            yg = _s5(xg, prow, pmat)
            xc = _even_out(ya, xs, gb, yg, xc, mods[layer], row(s5_d[i]), s5_w_glu[i].astype(BF16),
                           row(s5_b_glu[i]), we_out[i].astype(BF16), perm)
        else:
            final = layer == DEPTH - 1
            q, k, v, gate = _odd_in(xc, mods[layer], row(norm_g[layer]), wo_in[i].astype(BF16),
                                    row(q_norm_g[i]), row(k_norm_g[i]), cos, sa, sb)
            o_lat = _attention_latent(q, k, v)
            o_ctx = None if final else _attention_context(q, k, v)
            xc = _odd_out(o_lat, o_ctx, gate, xc, mods[layer], wo_out[i].astype(BF16), row(final_g))
    return xc
```

```python
import functools
import math

import jax
import jax.numpy as jnp
from jax import lax
from jax.experimental import pallas as pl
from jax.experimental.pallas import tpu as pltpu

F32 = jnp.float32
BF16 = jnp.bfloat16
HIGHEST = lax.Precision.HIGHEST

D_MODEL = 1024
BATCH = 4
SEQ = 4096
DEPTH = 4
GRID_W = 64
CTX_LEN = 256
EPS = 1e-6
NTOK = CTX_LEN + SEQ

LANES = 128
SUBLANES = 8
ROW_TILE = 256
N_TILES = NTOK // ROW_TILE
VMEM_LIMIT = 56 * 1024 * 1024

CHUNK = 128
A_WIDTH = D_MODEL // 2
A_GROUPS = 4
A_GROUP_W = A_WIDTH // A_GROUPS
B_WIDTH = D_MODEL // 2
B_CH = 16
B_GROUPS = B_WIDTH // B_CH
B_STATE = 64
EVEN_IN = 3 * A_WIDTH + 2 * B_WIDTH
HEAD_DIM = 128
N_Q = D_MODEL // HEAD_DIM
N_KV = 2
Q_PER_KV = N_Q // N_KV
C_WIDTH = N_Q * HEAD_DIM
KV_WIDTH = N_KV * HEAD_DIM
ODD_IN = 2 * C_WIDTH + 2 * KV_WIDTH
ROPE_THETA = 10000.0
ROPE_PAIRS = HEAD_DIM // 4
Q_SCALE = HEAD_DIM ** -0.5 * math.log2(math.e)
V_EXT = 2 * HEAD_DIM
ATTN_UNIT = 128

S5_Q = 16
S5_K = S5_Q * B_CH
N_CHUNKS = NTOK // S5_Q
S5_ROWS = BATCH * N_CHUNKS
TILE_CHUNKS = ROW_TILE // S5_Q
BLOCKS_PER_SAMPLE = N_CHUNKS // SUBLANES
CTX_BLOCKS = CTX_LEN // S5_Q // SUBLANES
N_PAIRS = B_GROUPS // 2
SCAN_UNROLL = 2
GROUPS_PER_VREG = LANES // B_CH
N_LAGS = 2 * S5_Q - 1

MOD_ROWS = 8
CTX_ROW = BATCH


def _params(*sem):
    return pltpu.CompilerParams(dimension_semantics=sem, vmem_limit_bytes=VMEM_LIMIT)


def _mod_row(b, i):
    return jnp.where(i == 0, CTX_ROW, b)


def _lane_block(rows):
    return lax.broadcasted_iota(jnp.int32, (rows, LANES), 1) // B_CH


def _move_blocks(pieces, blk):
    acc = None
    for dst, (src, src_blk) in enumerate(pieces):
        sh = ((dst - src_blk) % GROUPS_PER_VREG) * B_CH
        rolled = pltpu.roll(src, sh, 1) if sh else src
        acc = rolled if acc is None else jnp.where(blk == dst, rolled, acc)
    return acc


def _mod_kernel(cond_ref, w_ref, b_ref, o_ref):
    c = cond_ref[...]
    s = c * jax.nn.sigmoid(c)
    o_ref[0] = jnp.dot(s, w_ref[0], preferred_element_type=F32, precision=HIGHEST) + b_ref[0]


def _modulation(cond, w_mod, b_mod):
    nblk = 3 * D_MODEL // D_MODEL
    return pl.pallas_call(
        _mod_kernel,
        grid=(DEPTH, nblk),
        in_specs=[
            pl.BlockSpec((MOD_ROWS, D_MODEL), lambda l, j: (0, 0)),
            pl.BlockSpec((1, D_MODEL, D_MODEL), lambda l, j: (l, 0, j)),
            pl.BlockSpec((1, 1, D_MODEL), lambda l, j: (l, 0, j)),
        ],
        out_specs=pl.BlockSpec((1, MOD_ROWS, D_MODEL), lambda l, j: (l, 0, j)),
        out_shape=jax.ShapeDtypeStruct((DEPTH, MOD_ROWS, 3 * D_MODEL), F32),
        compiler_params=_params("parallel", "parallel"),
        name="modulation",
    )(cond, w_mod, b_mod.reshape(DEPTH, 1, 3 * D_MODEL))


def _prologue(x, mod, g):
    sh = mod[:, :D_MODEL]
    sc = mod[:, D_MODEL:2 * D_MODEL]
    y = x * lax.rsqrt(jnp.mean(x * x, axis=-1, keepdims=True) + EPS)
    return (y * g) * (1 + sc) + sh


def _chunk_perm():
    r = jnp.arange(ROW_TILE)
    swapped = (r % TILE_CHUNKS) * S5_Q + r // TILE_CHUNKS
    return (r[:, None] == swapped[None, :]).astype(BF16)


def _even_in_kernel(x_ref, mod_ref, g_ref, win_ref, vg_ref, ws_ref, bs_ref, perm_ref,
                    ya_ref, xs_ref, gb_ref, xg_ref):
    h = _prologue(x_ref[0], mod_ref[0], g_ref[...])
    z = jnp.dot(h.astype(BF16), win_ref[...], preferred_element_type=F32)
    u = z[:, 0:A_WIDTH]
    v = z[:, A_WIDTH:2 * A_WIDTH]
    ga = z[:, 2 * A_WIDTH:3 * A_WIDTH]
    xs = z[:, 3 * A_WIDTH:3 * A_WIDTH + B_WIDTH]
    xs_ref[0] = xs
    gb_ref[0] = z[:, 3 * A_WIDTH + B_WIDTH:]
    mu = jnp.mean(v, axis=-1, keepdims=True)
    vc = v - mu
    var = jnp.mean(vc * vc, axis=-1, keepdims=True)
    vn = ((vc * lax.rsqrt(var + EPS)) * vg_ref[...]).astype(BF16)
    rows = []
    for c in range(ROW_TILE // CHUNK):
        cols = []
        for g in range(A_GROUPS):
            blk = vn[c * CHUNK:(c + 1) * CHUNK, g * A_GROUP_W:(g + 1) * A_GROUP_W]
            cols.append(jnp.dot(ws_ref[g], blk, preferred_element_type=F32))
        rows.append(jnp.concatenate(cols, axis=1) + bs_ref[...])
    mixed = jnp.concatenate(rows, axis=0)
    ya_ref[0] = ((u * mixed) * (ga * jax.nn.sigmoid(ga))).astype(BF16)

    r = jnp.dot(perm_ref[...], xs.astype(BF16), preferred_element_type=F32)
    blk = _lane_block(TILE_CHUNKS)
    for g in range(B_GROUPS):
        col, src_blk = divmod(g, GROUPS_PER_VREG)
        halves = []
        for j in range(S5_K // LANES):
            pieces = [(r[(j * GROUPS_PER_VREG + k) * TILE_CHUNKS:(j * GROUPS_PER_VREG + k + 1) * TILE_CHUNKS,
                         col * LANES:(col + 1) * LANES], src_blk) for k in range(GROUPS_PER_VREG)]
            halves.append(_move_blocks(pieces, blk))
        xg_ref[g] = jnp.concatenate(halves, axis=1).astype(BF16)


def _even_in(xc, mod_l, norm_g, w_in, v_g, w_s, b_full, perm):
    tile = lambda w: pl.BlockSpec((1, ROW_TILE, w), lambda b, i: (b, i, 0))
    full = lambda *s: pl.BlockSpec(s, lambda b, i: (0,) * len(s))
    return pl.pallas_call(
        _even_in_kernel,
        grid=(BATCH, N_TILES),
        in_specs=[
            tile(D_MODEL),
            pl.BlockSpec((1, 1, 3 * D_MODEL), lambda b, i: (_mod_row(b, i), 0, 0)),
            full(1, D_MODEL),
            full(D_MODEL, EVEN_IN),
            full(1, A_WIDTH),
            full(A_GROUPS, CHUNK, CHUNK),
            full(CHUNK, A_WIDTH),
            full(ROW_TILE, ROW_TILE),
        ],
        out_specs=[tile(A_WIDTH), tile(B_WIDTH), tile(B_WIDTH),
                   pl.BlockSpec((B_GROUPS, TILE_CHUNKS, S5_K), lambda b, i: (0, b * N_TILES + i, 0))],
        out_shape=[
            jax.ShapeDtypeStruct((BATCH, NTOK, A_WIDTH), BF16),
            jax.ShapeDtypeStruct((BATCH, NTOK, B_WIDTH), F32),
            jax.ShapeDtypeStruct((BATCH, NTOK, B_WIDTH), F32),
            jax.ShapeDtypeStruct((B_GROUPS, S5_ROWS, S5_K), BF16),
        ],
        compiler_params=_params("parallel", "parallel"),
        name="even_in",
    )(xc, mod_l, norm_g, w_in, v_g, w_s, b_full, perm)


def _s5_kernel(xg_ref, prow_ref, pmat_ref, yg_ref, sl_ref, bst_ref, toep_ref, cst_ref, lhs_ref):
    lane = lax.broadcasted_iota(jnp.int32, (S5_Q, LANES), 1)
    in_group = (lane < B_STATE, lane >= B_STATE)
    blk = lane // B_CH
    zero = jnp.zeros((S5_Q, LANES), F32)

    def cmul(ar, ai, br, bi):
        return ar * br - ai * bi, ar * bi + ai * br

    w_in, c_out, scan_consts, c_rows = [], [], [], []
    for d in range(2):
        lam_re = prow_ref[0, d, 0:1, :]
        lam_im = prow_ref[0, d, 1:2, :]
        dt = jnp.exp(prow_ref[0, d, 2:3, :])
        z_re, z_im = lam_re * dt, lam_im * dt
        mag = jnp.exp(z_re)
        lb_re, lb_im = mag * jnp.cos(z_im), mag * jnp.sin(z_im)
        den = lam_re * lam_re + lam_im * lam_im
        n_re, n_im = lb_re - 1.0, lb_im
        f_re = (n_re * lam_re + n_im * lam_im) / den
        f_im = (n_im * lam_re - n_re * lam_im) / den
        b_re, b_im = pmat_ref[0, d, 0], pmat_ref[0, d, 1]
        c_re, c_im = pmat_ref[0, d, 2], pmat_ref[0, d, 3]
        bb_re, bb_im = cmul(f_re, f_im, b_re, b_im)
        c_rows.append((c_re, c_im))

        def powers(steps):
            m = jnp.exp(steps * z_re)
            return m * jnp.cos(steps * z_im), m * jnp.sin(steps * z_im)

        j = lax.broadcasted_iota(jnp.int32, (3 * SUBLANES, LANES), 0).astype(F32)
        p_re, p_im = powers(j)
        row = lambda a, k: a[k:k + 1, :]
        w_in.append([cmul(bb_re, bb_im, row(p_re, k), row(p_im, k)) for k in range(S5_Q)])
        c_out.append([cmul(c_re, c_im, row(p_re, k), row(p_im, k)) for k in range(S5_Q + 1)])

        i8 = lax.broadcasted_iota(jnp.int32, (SUBLANES, LANES), 0)
        order = i8 if d == 0 else (SUBLANES - 1) - i8
        apow = powers((order * S5_Q).astype(F32))
        bc = lambda r: powers(jnp.full((SUBLANES, LANES), r * S5_Q, F32))
        scan_consts.append((apow, [bc(1), bc(2), bc(4)], bc(SUBLANES)))

    for e in range(2):
        sel = lambda a: jnp.where(in_group[e], a, 0.0)
        for s in range(S5_Q):
            fr, fi = w_in[0][S5_Q - 1 - s]
            br, bi = w_in[1][s]
            r0 = e * S5_K + s * B_CH
            bst_ref[r0:r0 + B_CH, :] = jnp.concatenate(
                [sel(fr), sel(fi), sel(br), sel(bi)], axis=1).astype(BF16)

    for e in range(2):
        sel = lambda a: jnp.where(in_group[e], a, 0.0)
        for t in range(S5_Q):
            fr, fi = c_out[0][t + 1]
            br, bi = c_out[1][S5_Q - t]
            cst_ref[e, t * B_CH:(t + 1) * B_CH, :] = jnp.concatenate(
                [sel(fr), sel(-fi), sel(br), sel(-bi)], axis=1).astype(BF16)

    for a in range(N_LAGS + 1):
        lag = a - (S5_Q - 1)
        f = w_in[0][lag] if 0 <= lag < S5_Q else (zero, zero)
        b = w_in[1][-lag] if -S5_Q < lag <= 0 else (zero, zero)
        lhs_ref[a * B_CH:(a + 1) * B_CH, :] = jnp.concatenate([f[0], f[1], b[0], b[1]], axis=1)
    cq_rows = []
    for e in range(2):
        sel = lambda a: jnp.where(in_group[e], a, 0.0)
        (fr, fi), (br, bi) = c_rows
        cq_rows.append(jnp.concatenate([sel(fr), sel(-fi), sel(br), sel(-bi)], axis=1))
    cq = jnp.concatenate(cq_rows + [jnp.zeros((LANES - 2 * B_CH, 4 * LANES), F32)], axis=0)
    kst = lax.dot_general(lhs_ref[...], cq, (((1,), (1,)), ((), ())),
                          preferred_element_type=F32, precision=HIGHEST)
    for e in range(2):
        g_cols = []
        for col in range((N_LAGS + 1) // GROUPS_PER_VREG):
            pieces = [(kst[(col * GROUPS_PER_VREG + k) * B_CH:(col * GROUPS_PER_VREG + k + 1) * B_CH, :], e)
                      for k in range(GROUPS_PER_VREG)]
            g_cols.append(_move_blocks(pieces, blk))
        for s in range(S5_Q):
            c0, sh = divmod((S5_Q - 1 - s) * B_CH, LANES)
            if sh == 0:
                cols = g_cols[c0:c0 + 2]
            else:
                rolled = [pltpu.roll(g_cols[c0 + k], LANES - sh, 1) for k in range(3)]
                cols = [jnp.where(lane < LANES - sh, rolled[k], rolled[k + 1]) for k in range(2)]
            toep_ref[e, s * B_CH:(s + 1) * B_CH, :] = jnp.concatenate(cols, axis=1).astype(BF16)

    x0 = xg_ref[0]
    x1 = xg_ref[1]
    sl_ref[...] = jnp.dot(jnp.concatenate([x0, x1], axis=1), bst_ref[...], preferred_element_type=F32)

    i8 = lax.broadcasted_iota(jnp.int32, (SUBLANES, LANES), 0)

    def scan_block(p, carry, consts, fwd):
        (ap_re, ap_im), doubling, (a8_re, a8_im) = consts
        p_re, p_im = p

        def shifted(a, r):
            if fwd:
                return jnp.where(i8 >= r, pltpu.roll(a, r, 0), 0.0)
            return jnp.where(i8 < SUBLANES - r, pltpu.roll(a, SUBLANES - r, 0), 0.0)

        for r, (a_re, a_im) in zip((1, 2, 4), doubling):
            s_re, s_im = shifted(p_re, r), shifted(p_im, r)
            m_re, m_im = cmul(a_re, a_im, s_re, s_im)
            p_re, p_im = p_re + m_re, p_im + m_im
        c_re, c_im = carry
        e_re, e_im = cmul(ap_re, ap_im, c_re, c_im)
        entering = (shifted(p_re, 1) + e_re, shifted(p_im, 1) + e_im)
        last = SUBLANES - 1 if fwd else 0
        n_re, n_im = cmul(a8_re, a8_im, c_re, c_im)
        bcast = lambda a: jnp.broadcast_to(a[last:last + 1, :], (SUBLANES, LANES))
        return entering, (bcast(p_re) + n_re, bcast(p_im) + n_im)

    def step(j, carry):
        jb = jnp.where(j < CTX_BLOCKS, CTX_BLOCKS - 1 - j, BLOCKS_PER_SAMPLE + CTX_BLOCKS - 1 - j)
        work = []
        for b in range(BATCH):
            for d, jd in enumerate((j, jb)):
                row0 = pl.multiple_of((b * BLOCKS_PER_SAMPLE + jd) * SUBLANES, SUBLANES)
                rows, col = pl.ds(row0, SUBLANES), d * 2 * LANES
                p = (sl_ref[rows, col:col + LANES], sl_ref[rows, col + LANES:col + 2 * LANES])
                work.append((rows, col, p, carry[b][d], d))
        done = [(rows, col, scan_block(p, c, scan_consts[d], d == 0)) for rows, col, p, c, d in work]
        out = []
        for n, (rows, col, (entering, nxt)) in enumerate(done):
            sl_ref[rows, col:col + LANES] = entering[0]
            sl_ref[rows, col + LANES:col + 2 * LANES] = entering[1]
            out.append(nxt)
        return tuple((out[2 * b], out[2 * b + 1]) for b in range(BATCH))

    z8 = jnp.zeros((SUBLANES, LANES), F32)
    lax.fori_loop(0, BLOCKS_PER_SAMPLE, step, tuple(((z8, z8), (z8, z8)) for _ in range(BATCH)),
                  unroll=SCAN_UNROLL)

    sp = sl_ref[...].astype(BF16)
    nt = (((1,), (1,)), ((), ()))
    yg_ref[0] = (jnp.dot(x0, toep_ref[0], preferred_element_type=F32)
                 + lax.dot_general(sp, cst_ref[0], nt, preferred_element_type=F32))
    yg_ref[1] = (jnp.dot(x1, toep_ref[1], preferred_element_type=F32)
                 + lax.dot_general(sp, cst_ref[1], nt, preferred_element_type=F32))


def _s5_params(lam_re, lam_im, log_dt, b_re, b_im, c_re, c_im):
    def rows(a):
        return a.astype(F32).reshape(2, N_PAIRS, 1, 2 * B_STATE).transpose(1, 0, 2, 3)

    dt = jnp.broadcast_to(log_dt[..., None], lam_re.shape)
    pad = jnp.zeros((N_PAIRS, 2, SUBLANES - 3, LANES), F32)
    prow = jnp.concatenate([rows(lam_re), rows(lam_im), rows(dt), pad], axis=2)

    def mats(a, channel_axis):
        a = a.astype(F32)
        if channel_axis == 3:
            a = a.transpose(0, 1, 3, 2)
        a = a.reshape(2, N_PAIRS, 2, B_CH, B_STATE).transpose(1, 0, 3, 2, 4)
        return a.reshape(N_PAIRS, 2, B_CH, 2 * B_STATE)

    pmat = jnp.stack([mats(b_re, 3), mats(b_im, 3), mats(c_re, 2), mats(c_im, 2)], axis=2)
    return prow, pmat


def _s5(xg, prow, pmat):
    return pl.pallas_call(
        _s5_kernel,
        grid=(N_PAIRS,),
        in_specs=[
            pl.BlockSpec((2, S5_ROWS, S5_K), lambda q: (q, 0, 0)),
            pl.BlockSpec((1, 2, SUBLANES, LANES), lambda q: (q, 0, 0, 0)),
            pl.BlockSpec((1, 2, 4, B_CH, LANES), lambda q: (q, 0, 0, 0, 0)),
        ],
        out_specs=pl.BlockSpec((2, S5_ROWS, S5_K), lambda q: (q, 0, 0)),
        out_shape=jax.ShapeDtypeStruct((B_GROUPS, S5_ROWS, S5_K), F32),
        scratch_shapes=[
            pltpu.VMEM((S5_ROWS, 4 * LANES), F32),
            pltpu.VMEM((2 * S5_K, 4 * LANES), BF16),
            pltpu.VMEM((2, S5_K, S5_K), BF16),
            pltpu.VMEM((2, S5_K, 4 * LANES), BF16),
            pltpu.VMEM(((N_LAGS + 1) * B_CH, 4 * LANES), F32),
        ],
        compiler_params=_params("parallel"),
        name="s5_chunked",
    )(xg, prow, pmat)


def _even_out_kernel(ya_ref, xs_ref, gb_ref, yg_ref, x_ref, mod_ref, d_ref, wglu_ref, bglu_ref,
                     wout_ref, perm_ref, o_ref):
    blk = _lane_block(TILE_CHUNKS)
    steps = []
    for t in range(S5_Q):
        j, src_blk = divmod(t, GROUPS_PER_VREG)
        cols = []
        for c in range(B_WIDTH // LANES):
            pieces = [(yg_ref[c * GROUPS_PER_VREG + k, :, j * LANES:(j + 1) * LANES], src_blk)
                      for k in range(GROUPS_PER_VREG)]
            cols.append(_move_blocks(pieces, blk))
        steps.append(jnp.concatenate(cols, axis=1))
    ys = jnp.dot(perm_ref[...].astype(F32), jnp.concatenate(steps, axis=0),
                 preferred_element_type=F32, precision=HIGHEST)

    y = ys + d_ref[...] * xs_ref[0]
    y = jax.nn.gelu(y)
    t = jnp.dot(y.astype(BF16), wglu_ref[...], preferred_element_type=F32) + bglu_ref[...]
    y = y * jax.nn.sigmoid(t)
    gb = gb_ref[0]
    yb = (y * (gb * jax.nn.sigmoid(gb))).astype(BF16)
    mix = (jnp.dot(ya_ref[0], wout_ref[0:A_WIDTH, :], preferred_element_type=F32)
           + jnp.dot(yb, wout_ref[A_WIDTH:, :], preferred_element_type=F32))
    gt = mod_ref[0][:, 2 * D_MODEL:]
    o_ref[0] = x_ref[0] + gt * mix


def _even_out(ya, xs, gb, yg, xc, mod_l, d_skip, w_glu, b_glu, w_out, perm):
    tile = lambda w: pl.BlockSpec((1, ROW_TILE, w), lambda b, i: (b, i, 0))
    full = lambda *s: pl.BlockSpec(s, lambda b, i: (0,) * len(s))
    return pl.pallas_call(
        _even_out_kernel,
        grid=(BATCH, N_TILES),
        in_specs=[
            tile(A_WIDTH), tile(B_WIDTH), tile(B_WIDTH),
            pl.BlockSpec((B_GROUPS, TILE_CHUNKS, S5_K), lambda b, i: (0, b * N_TILES + i, 0)),
            tile(D_MODEL),
            pl.BlockSpec((1, 1, 3 * D_MODEL), lambda b, i: (_mod_row(b, i), 0, 0)),
            full(1, B_WIDTH), full(B_WIDTH, B_WIDTH), full(1, B_WIDTH), full(D_MODEL, D_MODEL),
            full(ROW_TILE, ROW_TILE),
        ],
        out_specs=tile(D_MODEL),
        out_shape=jax.ShapeDtypeStruct((BATCH, NTOK, D_MODEL), F32),
        compiler_params=_params("parallel", "parallel"),
        name="even_out",
    )(ya, xs, gb, yg, xc, mod_l, d_skip, w_glu, b_glu, w_out, perm)


def _rope_tables():
    rows = SEQ // GRID_W
    row = jnp.repeat(jnp.arange(rows), GRID_W)
    col = jnp.tile(jnp.arange(GRID_W), rows)
    freqs = ROPE_THETA ** (-jnp.arange(ROPE_PAIRS, dtype=F32) / ROPE_PAIRS)
    ar = row[:, None] * freqs
    ac = col[:, None] * freqs
    zeros = jnp.zeros_like(ar)
    cos = jnp.concatenate([jnp.cos(ar), jnp.cos(ar), jnp.cos(ac), jnp.cos(ac)], axis=1)
    sa = jnp.concatenate([-jnp.sin(ar), zeros, -jnp.sin(ac), zeros], axis=1)
    sb = jnp.concatenate([zeros, jnp.sin(ar), zeros, jnp.sin(ac)], axis=1)
    pad = lambda t, v: jnp.concatenate([jnp.full((CTX_LEN, HEAD_DIM), v, F32), t], axis=0)
    return pad(cos, 1.0), pad(sa, 0.0), pad(sb, 0.0)


def _odd_in_kernel(x_ref, mod_ref, g_ref, win_ref, qg_ref, kg_ref, cos_ref, sa_ref, sb_ref,
                   q_ref, k_ref, v_ref, gate_ref):
    h = _prologue(x_ref[0], mod_ref[0], g_ref[...])
    z = jnp.dot(h.astype(BF16), win_ref[...], preferred_element_type=F32)
    cos = cos_ref[...]
    sa = sa_ref[...]
    sb = sb_ref[...]

    def norm_rope(xh, gain, scale):
        xn = (xh * lax.rsqrt(jnp.mean(xh * xh, axis=-1, keepdims=True) + EPS)) * gain
        xa = pltpu.roll(xn, HEAD_DIM - ROPE_PAIRS, 1)
        xb = pltpu.roll(xn, ROPE_PAIRS, 1)
        out = xn * cos + xa * sa + xb * sb
        return (out * scale).astype(BF16) if scale is not None else out.astype(BF16)

    for hq in range(N_Q):
        q_ref[0, :, hq * HEAD_DIM:(hq + 1) * HEAD_DIM] = norm_rope(
            z[:, hq * HEAD_DIM:(hq + 1) * HEAD_DIM], qg_ref[...], Q_SCALE)
    ones = jnp.ones((ROW_TILE, HEAD_DIM), BF16)
    for hk in range(N_KV):
        lo = C_WIDTH + hk * HEAD_DIM
        k_ref[0, :, hk * HEAD_DIM:(hk + 1) * HEAD_DIM] = norm_rope(z[:, lo:lo + HEAD_DIM], kg_ref[...], None)
        lo = C_WIDTH + KV_WIDTH + hk * HEAD_DIM
        v_ref[0, :, hk * V_EXT:hk * V_EXT + HEAD_DIM] = z[:, lo:lo + HEAD_DIM].astype(BF16)
        v_ref[0, :, hk * V_EXT + HEAD_DIM:(hk + 1) * V_EXT] = ones
    gate_ref[0] = z[:, C_WIDTH + 2 * KV_WIDTH:]


def _odd_in(xc, mod_l, norm_g, w_in, q_g, k_g, cos, sa, sb):
    tile = lambda w: pl.BlockSpec((1, ROW_TILE, w), lambda b, i: (b, i, 0))
    full = lambda *s: pl.BlockSpec(s, lambda b, i: (0,) * len(s))
    tab = pl.BlockSpec((ROW_TILE, HEAD_DIM), lambda b, i: (i, 0))
    return pl.pallas_call(
        _odd_in_kernel,
        grid=(BATCH, N_TILES),
        in_specs=[
            tile(D_MODEL),
            pl.BlockSpec((1, 1, 3 * D_MODEL), lambda b, i: (_mod_row(b, i), 0, 0)),
            full(1, D_MODEL), full(D_MODEL, ODD_IN), full(1, HEAD_DIM), full(1, HEAD_DIM),
            tab, tab, tab,
        ],
        out_specs=[tile(C_WIDTH), tile(KV_WIDTH), tile(N_KV * V_EXT), tile(C_WIDTH)],
        out_shape=[
            jax.ShapeDtypeStruct((BATCH, NTOK, C_WIDTH), BF16),
            jax.ShapeDtypeStruct((BATCH, NTOK, KV_WIDTH), BF16),
            jax.ShapeDtypeStruct((BATCH, NTOK, N_KV * V_EXT), BF16),
            jax.ShapeDtypeStruct((BATCH, NTOK, C_WIDTH), F32),
        ],
        compiler_params=_params("parallel", "parallel"),
        name="odd_in",
    )(xc, mod_l, norm_g, w_in, q_g, k_g, cos, sa, sb)


_NT = (((1,), (1,)), ((), ()))


def _softmax_weights(q, k):
    s = lax.dot_general(q, k, _NT, preferred_element_type=F32)
    return jnp.exp2(s - jnp.max(s, axis=-1, keepdims=True)).astype(BF16)


def _weighted_values(p, v):
    ov = jnp.dot(p, v, preferred_element_type=F32)
    return (ov[:, :HEAD_DIM] / ov[:, HEAD_DIM:HEAD_DIM + 1]).astype(BF16)


def _attn_lat_kernel(q_ref, k_ref, v_ref, o_ref, p_ref):
    @pl.when(pl.program_id(0) == 0)
    def _():
        p_ref[...] = jnp.ones(p_ref.shape, BF16)

    k = k_ref[0]
    v = v_ref[0]
    for h in range(Q_PER_KV):
        cols = slice(h * HEAD_DIM, (h + 1) * HEAD_DIM)
        for r in range(ROW_TILE // ATTN_UNIT):
            rows = slice(r * ATTN_UNIT, (r + 1) * ATTN_UNIT)
            p_new = _softmax_weights(q_ref[0, rows, cols], k)
            o_ref[0, rows, cols] = _weighted_values(p_ref[h, rows, :], v)
            p_ref[h, rows, :] = p_new


def _attention_latent(q, k, v):
    n_tiles = N_TILES - 1
    n_units = BATCH * N_KV * n_tiles
    width = Q_PER_KV * HEAD_DIM

    def unit(u):
        return u // (N_KV * n_tiles), (u // n_tiles) % N_KV, u % n_tiles

    def q_map(t):
        b, g, i = unit(jnp.minimum(t, n_units - 1))
        return b, i + 1, g

    def k_map(t):
        b, g, _ = unit(jnp.minimum(t, n_units - 1))
        return b, 0, g

    def v_map(t):
        b, g, _ = unit(jnp.maximum(t - 1, 0))
        return b, 0, g

    def o_map(t):
        b, g, i = unit(jnp.maximum(t - 1, 0))
        return b, i, g

    return pl.pallas_call(
        _attn_lat_kernel,
        grid=(n_units + 1,),
        in_specs=[
            pl.BlockSpec((1, ROW_TILE, width), q_map),
            pl.BlockSpec((1, NTOK, HEAD_DIM), k_map),
            pl.BlockSpec((1, NTOK, V_EXT), v_map),
        ],
        out_specs=pl.BlockSpec((1, ROW_TILE, width), o_map),
        out_shape=jax.ShapeDtypeStruct((BATCH, SEQ, C_WIDTH), BF16),
        scratch_shapes=[pltpu.VMEM((Q_PER_KV, ROW_TILE, NTOK), BF16)],
        compiler_params=_params("arbitrary"),
        name="attention_latent",
    )(q, k, v)


def _attn_ctx_kernel(q_ref, k_ref, v_ref, o_ref):
    for h in range(N_Q):
        g = h // Q_PER_KV
        p = _softmax_weights(q_ref[0, :, h * HEAD_DIM:(h + 1) * HEAD_DIM],
                             k_ref[0, :, g * HEAD_DIM:(g + 1) * HEAD_DIM])
        o_ref[0, :, h * HEAD_DIM:(h + 1) * HEAD_DIM] = _weighted_values(
            p, v_ref[0, :, g * V_EXT:(g + 1) * V_EXT])


def _attention_context(q, k, v):
    spec = lambda w: pl.BlockSpec((1, CTX_LEN, w), lambda b: (b, 0, 0))
    return pl.pallas_call(
        _attn_ctx_kernel,
        grid=(BATCH,),
        in_specs=[spec(C_WIDTH), spec(KV_WIDTH), spec(N_KV * V_EXT)],
        out_specs=spec(C_WIDTH),
        out_shape=jax.ShapeDtypeStruct((BATCH, CTX_LEN, C_WIDTH), BF16),
        compiler_params=_params("parallel"),
        name="attention_context",
    )(q, k, v)


def _odd_out_kernel(olat_ref, *refs, final):
    if final:
        o = olat_ref[0]
    else:
        octx_ref, *refs = refs
        o = jnp.where(pl.program_id(1) == 0, octx_ref[0], olat_ref[0])
    gate_ref, x_ref, mod_ref, wout_ref, fg_ref, out_ref = refs
    g = gate_ref[0]
    a = (o.astype(F32) * (g * jax.nn.sigmoid(g))).astype(BF16)
    gt = mod_ref[0][:, 2 * D_MODEL:]
    x = x_ref[0] + gt * jnp.dot(a, wout_ref[...], preferred_element_type=F32)
    if final:
        x = (x * lax.rsqrt(jnp.mean(x * x, axis=-1, keepdims=True) + EPS)) * fg_ref[...]
    out_ref[0] = x


def _odd_out(o_lat, o_ctx, gate, xc, mod_l, w_out, final_g):
    final = o_ctx is None
    off = 1 if final else 0
    tile = lambda w: pl.BlockSpec((1, ROW_TILE, w), lambda b, i: (b, i + off, 0))
    full = lambda *s: pl.BlockSpec(s, lambda b, i: (0,) * len(s))
    lat = pl.BlockSpec((1, ROW_TILE, C_WIDTH), lambda b, i: (b, jnp.maximum(i + off - 1, 0), 0))
    ctx = [] if final else [pl.BlockSpec((1, CTX_LEN, C_WIDTH), lambda b, i: (b, 0, 0))]
    return pl.pallas_call(
        functools.partial(_odd_out_kernel, final=final),
        grid=(BATCH, N_TILES - off),
        in_specs=[
            lat, *ctx, tile(C_WIDTH), tile(D_MODEL),
            pl.BlockSpec((1, 1, 3 * D_MODEL), lambda b, i: (_mod_row(b, i + off), 0, 0)),
            full(D_MODEL, D_MODEL), full(1, D_MODEL),
        ],
        out_specs=pl.BlockSpec((1, ROW_TILE, D_MODEL), lambda b, i: (b, i, 0)),
        out_shape=jax.ShapeDtypeStruct((BATCH, NTOK - off * ROW_TILE, D_MODEL), F32),
        compiler_params=_params("parallel", "parallel"),
        name="odd_out_final" if final else "odd_out",
    )(o_lat, *([] if final else [o_ctx]), gate, xc, mod_l, w_out, final_g)


def kernel(x, c, ctx, c_ctx, norm_g, w_mod, b_mod, we_in, we_out, gm_v_g, gm_w_s, gm_b_s,
           s5_lam_re, s5_lam_im, s5_log_dt, s5_b_re, s5_b_im, s5_c_re, s5_c_im, s5_d,
           s5_w_glu, s5_b_glu, wo_in, wo_out, q_norm_g, k_norm_g, final_g):
    cond = jnp.concatenate([c, c_ctx[None], jnp.zeros((MOD_ROWS - BATCH - 1, D_MODEL), F32)], axis=0)
    mods = _modulation(cond, w_mod, b_mod).reshape(DEPTH, MOD_ROWS, 1, 3 * D_MODEL)
    xc = jnp.concatenate([ctx, x], axis=1)
    cos, sa, sb = _rope_tables()
    perm = _chunk_perm()
    row = lambda a: a.reshape(1, -1)

    for layer in range(DEPTH):
        i = layer // 2
        if layer % 2 == 0:
            b_full = jnp.repeat(gm_b_s[i].T, A_GROUP_W, axis=1)
            ya, xs, gb, xg = _even_in(xc, mods[layer], row(norm_g[layer]), we_in[i].astype(BF16),
                                      row(gm_v_g[i]), gm_w_s[i].astype(BF16), b_full, perm)
            prow, pmat = _s5_params(s5_lam_re[i], s5_lam_im[i], s5_log_dt[i], s5_b_re[i],
                                    s5_b_im[i], s5_c_re[i], s5_c_im[i])
            yg = _s5(xg, prow, pmat)
            xc = _even_out(ya, xs, gb, yg, xc, mods[layer], row(s5_d[i]), s5_w_glu[i].astype(BF16),
                           row(s5_b_glu[i]), we_out[i].astype(BF16), perm)
        else:
            final = layer == DEPTH - 1
            q, k, v, gate = _odd_in(xc, mods[layer], row(norm_g[layer]), wo_in[i].astype(BF16),
                                    row(q_norm_g[i]), row(k_norm_g[i]), cos, sa, sb)
            o_lat = _attention_latent(q, k, v)
            o_ctx = None if final else _attention_context(q, k, v)
            xc = _odd_out(o_lat, o_ctx, gate, xc, mods[layer], wo_out[i].astype(BF16), row(final_g))
    return xc
```

```python
import functools
import math

import jax
import jax.numpy as jnp
import numpy as np
from jax import lax
from jax.experimental import pallas as pl
from jax.experimental.pallas import tpu as pltpu

F32 = jnp.float32
BF16 = jnp.bfloat16
HIGHEST = lax.Precision.HIGHEST

D_MODEL = 1024
BATCH = 4
SEQ = 4096
DEPTH = 4
GRID_W = 64
CTX_LEN = 256
EPS = 1e-6
NTOK = CTX_LEN + SEQ

LANES = 128
SUBLANES = 8
ROW_TILE = 256
N_TILES = NTOK // ROW_TILE
VMEM_LIMIT = 56 * 1024 * 1024

CHUNK = 128
A_WIDTH = D_MODEL // 2
A_GROUPS = 4
A_GROUP_W = A_WIDTH // A_GROUPS
B_WIDTH = D_MODEL // 2
B_CH = 16
B_GROUPS = B_WIDTH // B_CH
B_STATE = 64
EVEN_IN = 3 * A_WIDTH + 2 * B_WIDTH
HEAD_DIM = 128
N_Q = D_MODEL // HEAD_DIM
N_KV = 2
Q_PER_KV = N_Q // N_KV
C_WIDTH = N_Q * HEAD_DIM
KV_WIDTH = N_KV * HEAD_DIM
ODD_IN = 2 * C_WIDTH + 2 * KV_WIDTH
ROPE_THETA = 10000.0
ROPE_PAIRS = HEAD_DIM // 4
Q_SCALE = HEAD_DIM ** -0.5 * math.log2(math.e)
V_EXT = 2 * HEAD_DIM
ATTN_UNIT = 128

S5_Q = 16
S5_K = S5_Q * B_CH
N_CHUNKS = NTOK // S5_Q
S5_ROWS = BATCH * N_CHUNKS
TILE_CHUNKS = ROW_TILE // S5_Q
SUB_TILE = ROW_TILE // 2
SUB_CHUNKS = SUB_TILE // S5_Q
assert SUB_TILE == CHUNK
ODD_SUB = ROW_TILE
BLOCKS_PER_SAMPLE = N_CHUNKS // SUBLANES
CTX_BLOCKS = CTX_LEN // S5_Q // SUBLANES
N_PAIRS = B_GROUPS // 2
SCAN_UNROLL = 2
GROUPS_PER_VREG = LANES // B_CH
N_LAGS = 2 * S5_Q - 1

MOD_ROWS = 8
CTX_ROW = BATCH


def _params(*sem):
    return pltpu.CompilerParams(dimension_semantics=sem, vmem_limit_bytes=VMEM_LIMIT)


def _mod_row(b, i):
    return jnp.where(i == 0, CTX_ROW, b)


def _lane_block(rows):
    return lax.broadcasted_iota(jnp.int32, (rows, LANES), 1) // B_CH


def _move_blocks(pieces, blk):
    acc = None
    for dst, (src, src_blk) in enumerate(pieces):
        sh = ((dst - src_blk) % GROUPS_PER_VREG) * B_CH
        rolled = pltpu.roll(src, sh, 1) if sh else src
        acc = rolled if acc is None else jnp.where(blk == dst, rolled, acc)
    return acc


def _merge_blocks(pick, blk):
    acc = pick(0)
    for k in range(1, GROUPS_PER_VREG):
        acc = jnp.where(blk == k, pick(k), acc)
    return acc


def _block_transpose(src, blk):
    n = GROUPS_PER_VREG
    rolled = []
    for r in range(n):
        m = _merge_blocks(lambda b: src[(b + r) % n], blk)
        rolled.append(pltpu.roll(m, r * B_CH, 1) if r else m)
    return [_merge_blocks(lambda k: rolled[(k - b) % n], blk) for b in range(n)]


def _mod_kernel(cond_ref, w_ref, b_ref, o_ref):
    c = cond_ref[...]
    s = c * jax.nn.sigmoid(c)
    o_ref[0] = jnp.dot(s, w_ref[0], preferred_element_type=F32, precision=HIGHEST) + b_ref[0]


def _modulation(cond, w_mod, b_mod):
    nblk = 3 * D_MODEL // D_MODEL
    return pl.pallas_call(
        _mod_kernel,
        grid=(DEPTH, nblk),
        in_specs=[
            pl.BlockSpec((MOD_ROWS, D_MODEL), lambda l, j: (0, 0)),
            pl.BlockSpec((1, D_MODEL, D_MODEL), lambda l, j: (l, 0, j)),
            pl.BlockSpec((1, 1, D_MODEL), lambda l, j: (l, 0, j)),
        ],
        out_specs=pl.BlockSpec((1, MOD_ROWS, D_MODEL), lambda l, j: (l, 0, j)),
        out_shape=jax.ShapeDtypeStruct((DEPTH, MOD_ROWS, 3 * D_MODEL), F32),
        compiler_params=_params("parallel", "parallel"),
        name="modulation",
    )(cond, w_mod, b_mod.reshape(DEPTH, 1, 3 * D_MODEL))


def _prologue(x, mod, g):
    sh = mod[:, :D_MODEL]
    sc = mod[:, D_MODEL:2 * D_MODEL]
    y = x * lax.rsqrt(jnp.mean(x * x, axis=-1, keepdims=True) + EPS)
    return (y * g) * (1 + sc) + sh


def _chunk_perm():
    r = jnp.arange(SUB_TILE)
    source = (r % SUB_CHUNKS) * S5_Q + r // SUB_CHUNKS
    return (source[:, None] == r[None, :]).astype(BF16)


def _even_in_kernel(x_ref, mod_ref, g_ref, win_ref, vg_ref, ws_ref, bs_ref, perm_ref,
                    ya_ref, xs_ref, gb_ref, xg_ref):
    blk = _lane_block(SUB_CHUNKS)
    xg_parts = []
    for sub in range(ROW_TILE // SUB_TILE):
        rows = slice(sub * SUB_TILE, (sub + 1) * SUB_TILE)
        h = _prologue(x_ref[0, rows, :], mod_ref[0], g_ref[...])
        z = jnp.dot(h.astype(BF16), win_ref[...], preferred_element_type=F32)
        u = z[:, 0:A_WIDTH]
        v = z[:, A_WIDTH:2 * A_WIDTH]
        ga = z[:, 2 * A_WIDTH:3 * A_WIDTH]
        xs = z[:, 3 * A_WIDTH:3 * A_WIDTH + B_WIDTH]
        xs_ref[0, rows, :] = xs
        gb_ref[0, rows, :] = z[:, 3 * A_WIDTH + B_WIDTH:]
        mu = jnp.mean(v, axis=-1, keepdims=True)
        vc = v - mu
        var = jnp.mean(vc * vc, axis=-1, keepdims=True)
        vn = ((vc * lax.rsqrt(var + EPS)) * vg_ref[...]).astype(BF16)
        mixed = jnp.concatenate(
            [jnp.dot(ws_ref[g], vn[:, g * A_GROUP_W:(g + 1) * A_GROUP_W], preferred_element_type=F32)
             for g in range(A_GROUPS)], axis=1) + bs_ref[...]
        ya_ref[0, rows, :] = ((u * mixed) * (ga * jax.nn.sigmoid(ga))).astype(BF16)

        r = jnp.dot(perm_ref[...], xs.astype(BF16), preferred_element_type=F32)
        parts = [[None] * (S5_K // LANES) for _ in range(B_GROUPS)]
        for col in range(B_WIDTH // LANES):
            for j in range(S5_K // LANES):
                steps = [r[(j * GROUPS_PER_VREG + k) * SUB_CHUNKS:(j * GROUPS_PER_VREG + k + 1) * SUB_CHUNKS,
                           col * LANES:(col + 1) * LANES] for k in range(GROUPS_PER_VREG)]
                for b, a in enumerate(_block_transpose(steps, blk)):
                    parts[col * GROUPS_PER_VREG + b][j] = a
        xg_parts.append([jnp.concatenate(p, axis=1) for p in parts])
    for g in range(B_GROUPS):
        xg_ref[g] = jnp.concatenate([parts[g] for parts in xg_parts], axis=0).astype(BF16)


def _even_in(xc, mod_l, norm_g, w_in, v_g, w_s, b_full, perm):
    tile = lambda w: pl.BlockSpec((1, ROW_TILE, w), lambda b, i: (b, i, 0))
    full = lambda *s: pl.BlockSpec(s, lambda b, i: (0,) * len(s))
    return pl.pallas_call(
        _even_in_kernel,
        grid=(BATCH, N_TILES),
        in_specs=[
            tile(D_MODEL),
            pl.BlockSpec((1, 1, 3 * D_MODEL), lambda b, i: (_mod_row(b, i), 0, 0)),
            full(1, D_MODEL),
            full(D_MODEL, EVEN_IN),
            full(1, A_WIDTH),
            full(A_GROUPS, CHUNK, CHUNK),
            full(CHUNK, A_WIDTH),
            full(SUB_TILE, SUB_TILE),
        ],
        out_specs=[tile(A_WIDTH), tile(B_WIDTH), tile(B_WIDTH),
                   pl.BlockSpec((B_GROUPS, TILE_CHUNKS, S5_K), lambda b, i: (0, b * N_TILES + i, 0))],
        out_shape=[
            jax.ShapeDtypeStruct((BATCH, NTOK, A_WIDTH), BF16),
            jax.ShapeDtypeStruct((BATCH, NTOK, B_WIDTH), F32),
            jax.ShapeDtypeStruct((BATCH, NTOK, B_WIDTH), F32),
            jax.ShapeDtypeStruct((B_GROUPS, S5_ROWS, S5_K), BF16),
        ],
        compiler_params=_params("parallel", "parallel"),
        name="even_in",
    )(xc, mod_l, norm_g, w_in, v_g, w_s, b_full, perm)


def _s5_kernel(xg_ref, prow_ref, pmat_ref, yg_ref, sl_ref, bst_ref, toep_ref, cst_ref, lhs_ref):
    lane = lax.broadcasted_iota(jnp.int32, (S5_Q, LANES), 1)
    in_group = (lane < B_STATE, lane >= B_STATE)
    blk = lane // B_CH
    zero = jnp.zeros((S5_Q, LANES), F32)

    def cmul(ar, ai, br, bi):
        return ar * br - ai * bi, ar * bi + ai * br

    w_in, c_out, scan_consts, c_rows = [], [], [], []
    for d in range(2):
        lam_re = prow_ref[0, d, 0:1, :]
        lam_im = prow_ref[0, d, 1:2, :]
        dt = jnp.exp(prow_ref[0, d, 2:3, :])
        z_re, z_im = lam_re * dt, lam_im * dt
        mag = jnp.exp(z_re)
        lb_re, lb_im = mag * jnp.cos(z_im), mag * jnp.sin(z_im)
        den = lam_re * lam_re + lam_im * lam_im
        n_re, n_im = lb_re - 1.0, lb_im
        f_re = (n_re * lam_re + n_im * lam_im) / den
        f_im = (n_im * lam_re - n_re * lam_im) / den
        b_re, b_im = pmat_ref[0, d, 0], pmat_ref[0, d, 1]
        c_re, c_im = pmat_ref[0, d, 2], pmat_ref[0, d, 3]
        bb_re, bb_im = cmul(f_re, f_im, b_re, b_im)
        c_rows.append((c_re, c_im))

        def powers(steps):
            m = jnp.exp(steps * z_re)
            return m * jnp.cos(steps * z_im), m * jnp.sin(steps * z_im)

        j = lax.broadcasted_iota(jnp.int32, (3 * SUBLANES, LANES), 0).astype(F32)
        p_re, p_im = powers(j)
        row = lambda a, k: a[k:k + 1, :]
        w_in.append([cmul(bb_re, bb_im, row(p_re, k), row(p_im, k)) for k in range(S5_Q)])
        c_out.append([cmul(c_re, c_im, row(p_re, k), row(p_im, k)) for k in range(S5_Q + 1)])

        i8 = lax.broadcasted_iota(jnp.int32, (SUBLANES, LANES), 0)
        order = i8 if d == 0 else (SUBLANES - 1) - i8
        apow = powers((order * S5_Q).astype(F32))
        bc = lambda r: powers(jnp.full((SUBLANES, LANES), r * S5_Q, F32))
        scan_consts.append((apow, [bc(1), bc(2), bc(4)], bc(SUBLANES)))

    for e in range(2):
        sel = lambda a: jnp.where(in_group[e], a, 0.0)
        for s in range(S5_Q):
            fr, fi = w_in[0][S5_Q - 1 - s]
            br, bi = w_in[1][s]
            r0 = e * S5_K + s * B_CH
            bst_ref[r0:r0 + B_CH, :] = jnp.concatenate(
                [sel(fr), sel(fi), sel(br), sel(bi)], axis=1).astype(BF16)

    for e in range(2):
        sel = lambda a: jnp.where(in_group[e], a, 0.0)
        for t in range(S5_Q):
            fr, fi = c_out[0][t + 1]
            br, bi = c_out[1][S5_Q - t]
            cst_ref[e, t * B_CH:(t + 1) * B_CH, :] = jnp.concatenate(
                [sel(fr), sel(-fi), sel(br), sel(-bi)], axis=1).astype(BF16)

    for a in range(N_LAGS + 1):
        lag = a - (S5_Q - 1)
        f = w_in[0][lag] if 0 <= lag < S5_Q else (zero, zero)
        b = w_in[1][-lag] if -S5_Q < lag <= 0 else (zero, zero)
        lhs_ref[a * B_CH:(a + 1) * B_CH, :] = jnp.concatenate([f[0], f[1], b[0], b[1]], axis=1)
    cq_rows = []
    for e in range(2):
        sel = lambda a: jnp.where(in_group[e], a, 0.0)
        (fr, fi), (br, bi) = c_rows
        cq_rows.append(jnp.concatenate([sel(fr), sel(-fi), sel(br), sel(-bi)], axis=1))
    cq = jnp.concatenate(cq_rows + [jnp.zeros((LANES - 2 * B_CH, 4 * LANES), F32)], axis=0)
    kst = lax.dot_general(lhs_ref[...], cq, (((1,), (1,)), ((), ())),
                          preferred_element_type=F32, precision=HIGHEST)
    for e in range(2):
        g_cols = []
        for col in range((N_LAGS + 1) // GROUPS_PER_VREG):
            pieces = [(kst[(col * GROUPS_PER_VREG + k) * B_CH:(col * GROUPS_PER_VREG + k + 1) * B_CH, :], e)
                      for k in range(GROUPS_PER_VREG)]
            g_cols.append(_move_blocks(pieces, blk))
        for s in range(S5_Q):
            c0, sh = divmod((S5_Q - 1 - s) * B_CH, LANES)
            if sh == 0:
                cols = g_cols[c0:c0 + 2]
            else:
                rolled = [pltpu.roll(g_cols[c0 + k], LANES - sh, 1) for k in range(3)]
                cols = [jnp.where(lane < LANES - sh, rolled[k], rolled[k + 1]) for k in range(2)]
            toep_ref[e, s * B_CH:(s + 1) * B_CH, :] = jnp.concatenate(cols, axis=1).astype(BF16)

    x0 = xg_ref[0]
    x1 = xg_ref[1]
    sl_ref[...] = jnp.dot(jnp.concatenate([x0, x1], axis=1), bst_ref[...], preferred_element_type=F32)

    i8 = lax.broadcasted_iota(jnp.int32, (SUBLANES, LANES), 0)

    def scan_block(p, carry, consts, fwd):
        (ap_re, ap_im), doubling, (a8_re, a8_im) = consts
        p_re, p_im = p

        def shifted(a, r):
            if fwd:
                return jnp.where(i8 >= r, pltpu.roll(a, r, 0), 0.0)
            return jnp.where(i8 < SUBLANES - r, pltpu.roll(a, SUBLANES - r, 0), 0.0)

        for r, (a_re, a_im) in zip((1, 2, 4), doubling):
            s_re, s_im = shifted(p_re, r), shifted(p_im, r)
            m_re, m_im = cmul(a_re, a_im, s_re, s_im)
            p_re, p_im = p_re + m_re, p_im + m_im
        c_re, c_im = carry
        e_re, e_im = cmul(ap_re, ap_im, c_re, c_im)
        entering = (shifted(p_re, 1) + e_re, shifted(p_im, 1) + e_im)
        last = SUBLANES - 1 if fwd else 0
        n_re, n_im = cmul(a8_re, a8_im, c_re, c_im)
        bcast = lambda a: jnp.broadcast_to(a[last:last + 1, :], (SUBLANES, LANES))
        return entering, (bcast(p_re) + n_re, bcast(p_im) + n_im)

    def step(j, carry):
        jb = jnp.where(j < CTX_BLOCKS, CTX_BLOCKS - 1 - j, BLOCKS_PER_SAMPLE + CTX_BLOCKS - 1 - j)
        work = []
        for b in range(BATCH):
            for d, jd in enumerate((j, jb)):
                row0 = pl.multiple_of((b * BLOCKS_PER_SAMPLE + jd) * SUBLANES, SUBLANES)
                rows, col = pl.ds(row0, SUBLANES), d * 2 * LANES
                p = (sl_ref[rows, col:col + LANES], sl_ref[rows, col + LANES:col + 2 * LANES])
                work.append((rows, col, p, carry[b][d], d))
        done = [(rows, col, scan_block(p, c, scan_consts[d], d == 0)) for rows, col, p, c, d in work]
        out = []
        for n, (rows, col, (entering, nxt)) in enumerate(done):
            sl_ref[rows, col:col + LANES] = entering[0]
            sl_ref[rows, col + LANES:col + 2 * LANES] = entering[1]
            out.append(nxt)
        return tuple((out[2 * b], out[2 * b + 1]) for b in range(BATCH))

    z8 = jnp.zeros((SUBLANES, LANES), F32)
    lax.fori_loop(0, BLOCKS_PER_SAMPLE, step, tuple(((z8, z8), (z8, z8)) for _ in range(BATCH)),
                  unroll=SCAN_UNROLL)

    sp = sl_ref[...].astype(BF16)
    nt = (((1,), (1,)), ((), ()))
    yg_ref[0] = (jnp.dot(x0, toep_ref[0], preferred_element_type=F32)
                 + lax.dot_general(sp, cst_ref[0], nt, preferred_element_type=F32))
    yg_ref[1] = (jnp.dot(x1, toep_ref[1], preferred_element_type=F32)
                 + lax.dot_general(sp, cst_ref[1], nt, preferred_element_type=F32))


def _s5_params(lam_re, lam_im, log_dt, b_re, b_im, c_re, c_im):
    def rows(a):
        return a.astype(F32).reshape(2, N_PAIRS, 1, 2 * B_STATE).transpose(1, 0, 2, 3)

    dt = jnp.broadcast_to(log_dt[..., None], lam_re.shape)
    pad = jnp.zeros((N_PAIRS, 2, SUBLANES - 3, LANES), F32)
    prow = jnp.concatenate([rows(lam_re), rows(lam_im), rows(dt), pad], axis=2)

    def mats(a, channel_axis):
        a = a.astype(F32)
        if channel_axis == 3:
            a = a.transpose(0, 1, 3, 2)
        a = a.reshape(2, N_PAIRS, 2, B_CH, B_STATE).transpose(1, 0, 3, 2, 4)
        return a.reshape(N_PAIRS, 2, B_CH, 2 * B_STATE)

    pmat = jnp.stack([mats(b_re, 3), mats(b_im, 3), mats(c_re, 2), mats(c_im, 2)], axis=2)
    return prow, pmat


def _s5(xg, prow, pmat):
    return pl.pallas_call(
        _s5_kernel,
        grid=(N_PAIRS,),
        in_specs=[
            pl.BlockSpec((2, S5_ROWS, S5_K), lambda q: (q, 0, 0)),
            pl.BlockSpec((1, 2, SUBLANES, LANES), lambda q: (q, 0, 0, 0)),
            pl.BlockSpec((1, 2, 4, B_CH, LANES), lambda q: (q, 0, 0, 0, 0)),
        ],
        out_specs=pl.BlockSpec((2, S5_ROWS, S5_K), lambda q: (q, 0, 0)),
        out_shape=jax.ShapeDtypeStruct((B_GROUPS, S5_ROWS, S5_K), F32),
        scratch_shapes=[
            pltpu.VMEM((S5_ROWS, 4 * LANES), F32),
            pltpu.VMEM((2 * S5_K, 4 * LANES), BF16),
            pltpu.VMEM((2, S5_K, S5_K), BF16),
            pltpu.VMEM((2, S5_K, 4 * LANES), BF16),
            pltpu.VMEM(((N_LAGS + 1) * B_CH, 4 * LANES), F32),
        ],
        compiler_params=_params("parallel"),
        name="s5_chunked",
    )(xg, prow, pmat)


def _even_out_kernel(ya_ref, xs_ref, gb_ref, yg_ref, x_ref, mod_ref, d_ref, wglu_ref, bglu_ref,
                     wout_ref, perm_ref, o_ref):
    blk = _lane_block(SUB_CHUNKS)
    gt = mod_ref[0][:, 2 * D_MODEL:]
    for sub in range(ROW_TILE // SUB_TILE):
        rows = slice(sub * SUB_TILE, (sub + 1) * SUB_TILE)
        chunks = slice(sub * SUB_CHUNKS, (sub + 1) * SUB_CHUNKS)
        steps = [[None] * (B_WIDTH // LANES) for _ in range(S5_Q)]
        for col in range(B_WIDTH // LANES):
            for j in range(S5_K // LANES):
                groups = [yg_ref[col * GROUPS_PER_VREG + k, chunks, j * LANES:(j + 1) * LANES]
                          for k in range(GROUPS_PER_VREG)]
                for b, a in enumerate(_block_transpose(groups, blk)):
                    steps[j * GROUPS_PER_VREG + b][col] = a
        ys = jnp.dot(perm_ref[...].astype(F32),
                     jnp.concatenate([jnp.concatenate(s, axis=1) for s in steps], axis=0),
                     preferred_element_type=F32, precision=HIGHEST)

        y = ys + d_ref[...] * xs_ref[0, rows, :]
        y = jax.nn.gelu(y)
        t = jnp.dot(y.astype(BF16), wglu_ref[...], preferred_element_type=F32) + bglu_ref[...]
        y = y * jax.nn.sigmoid(t)
        gb = gb_ref[0, rows, :]
        yb = (y * (gb * jax.nn.sigmoid(gb))).astype(BF16)
        mix = (jnp.dot(ya_ref[0, rows, :], wout_ref[0:A_WIDTH, :], preferred_element_type=F32)
               + jnp.dot(yb, wout_ref[A_WIDTH:, :], preferred_element_type=F32))
        o_ref[0, rows, :] = x_ref[0, rows, :] + gt * mix


def _even_out(ya, xs, gb, yg, xc, mod_l, d_skip, w_glu, b_glu, w_out, perm):
    tile = lambda w: pl.BlockSpec((1, ROW_TILE, w), lambda b, i: (b, i, 0))
    full = lambda *s: pl.BlockSpec(s, lambda b, i: (0,) * len(s))
    return pl.pallas_call(
        _even_out_kernel,
        grid=(BATCH, N_TILES),
        in_specs=[
            tile(A_WIDTH), tile(B_WIDTH), tile(B_WIDTH),
            pl.BlockSpec((B_GROUPS, TILE_CHUNKS, S5_K), lambda b, i: (0, b * N_TILES + i, 0)),
            tile(D_MODEL),
            pl.BlockSpec((1, 1, 3 * D_MODEL), lambda b, i: (_mod_row(b, i), 0, 0)),
            full(1, B_WIDTH), full(B_WIDTH, B_WIDTH), full(1, B_WIDTH), full(D_MODEL, D_MODEL),
            full(SUB_TILE, SUB_TILE),
        ],
        out_specs=tile(D_MODEL),
        out_shape=jax.ShapeDtypeStruct((BATCH, NTOK, D_MODEL), F32),
        compiler_params=_params("parallel", "parallel"),
        name="even_out",
    )(ya, xs, gb, yg, xc, mod_l, d_skip, w_glu, b_glu, w_out, perm)


_HEAD_ORDER = np.concatenate([np.arange(0, 32), np.arange(64, 96), np.arange(32, 64), np.arange(96, 128)])


def _rope_tables():
    t = np.arange(SEQ)
    freqs = ROPE_THETA ** (-np.arange(ROPE_PAIRS, dtype=np.float64) / ROPE_PAIRS)
    ang = np.concatenate([(t // GRID_W)[:, None] * freqs, (t % GRID_W)[:, None] * freqs], axis=1)
    cos = np.concatenate([np.cos(ang), np.cos(ang)], axis=1)
    sin = np.concatenate([-np.sin(ang), np.sin(ang)], axis=1)
    pad = lambda a, v: np.concatenate([np.full((CTX_LEN, HEAD_DIM), v), a], axis=0).astype(np.float32)
    return jnp.asarray(pad(cos, 1.0)), jnp.asarray(pad(sin, 0.0))


def _permute_heads(w_in, q_g, k_g):
    n_heads = N_Q + N_KV
    cols = (np.arange(n_heads)[:, None] * HEAD_DIM + _HEAD_ORDER[None, :]).reshape(-1)
    cols = np.concatenate([cols, np.arange(n_heads * HEAD_DIM, ODD_IN)])
    return w_in[:, cols], q_g[_HEAD_ORDER], k_g[_HEAD_ORDER]


def _odd_in_kernel(x_ref, mod_ref, g_ref, win_ref, qg_ref, kg_ref, cos_ref, sin_ref,
                   q_ref, k_ref, v_ref, gate_ref):
    ones = jnp.ones((ODD_SUB, HEAD_DIM), BF16)
    for sub in range(ROW_TILE // ODD_SUB):
        rows = slice(sub * ODD_SUB, (sub + 1) * ODD_SUB)
        h = _prologue(x_ref[0, rows, :], mod_ref[0], g_ref[...])
        z = jnp.dot(h.astype(BF16), win_ref[...], preferred_element_type=F32)
        cos = cos_ref[rows, :]
        sin = sin_ref[rows, :]

        def norm_rope(xh, gain, scale):
            xn = (xh * lax.rsqrt(jnp.mean(xh * xh, axis=-1, keepdims=True) + EPS)) * gain
            out = xn * cos + pltpu.roll(xn, HEAD_DIM // 2, 1) * sin
            return (out * scale).astype(BF16) if scale is not None else out.astype(BF16)

        for hq in range(N_Q):
            q_ref[0, rows, hq * HEAD_DIM:(hq + 1) * HEAD_DIM] = norm_rope(
                z[:, hq * HEAD_DIM:(hq + 1) * HEAD_DIM], qg_ref[...], Q_SCALE)
        for hk in range(N_KV):
            lo = C_WIDTH + hk * HEAD_DIM
            k_ref[0, rows, hk * HEAD_DIM:(hk + 1) * HEAD_DIM] = norm_rope(
                z[:, lo:lo + HEAD_DIM], kg_ref[...], None)
            lo = C_WIDTH + KV_WIDTH + hk * HEAD_DIM
            v_ref[0, rows, hk * V_EXT:hk * V_EXT + HEAD_DIM] = z[:, lo:lo + HEAD_DIM].astype(BF16)
            v_ref[0, rows, hk * V_EXT + HEAD_DIM:(hk + 1) * V_EXT] = ones
        gate_ref[0, rows, :] = z[:, C_WIDTH + 2 * KV_WIDTH:]


def _odd_in(xc, mod_l, norm_g, w_in, q_g, k_g, cos, sin):
    tile = lambda w: pl.BlockSpec((1, ROW_TILE, w), lambda b, i: (b, i, 0))
    full = lambda *s: pl.BlockSpec(s, lambda b, i: (0,) * len(s))
    tab = pl.BlockSpec((ROW_TILE, HEAD_DIM), lambda b, i: (i, 0))
    return pl.pallas_call(
        _odd_in_kernel,
        grid=(BATCH, N_TILES),
        in_specs=[
            tile(D_MODEL),
            pl.BlockSpec((1, 1, 3 * D_MODEL), lambda b, i: (_mod_row(b, i), 0, 0)),
            full(1, D_MODEL), full(D_MODEL, ODD_IN), full(1, HEAD_DIM), full(1, HEAD_DIM),
            tab, tab,
        ],
        out_specs=[tile(C_WIDTH), tile(KV_WIDTH), tile(N_KV * V_EXT), tile(C_WIDTH)],
        out_shape=[
            jax.ShapeDtypeStruct((BATCH, NTOK, C_WIDTH), BF16),
            jax.ShapeDtypeStruct((BATCH, NTOK, KV_WIDTH), BF16),
            jax.ShapeDtypeStruct((BATCH, NTOK, N_KV * V_EXT), BF16),
            jax.ShapeDtypeStruct((BATCH, NTOK, C_WIDTH), F32),
        ],
        compiler_params=_params("parallel", "parallel"),
        name="odd_in",
    )(xc, mod_l, norm_g, w_in, q_g, k_g, cos, sin)


_NT = (((1,), (1,)), ((), ()))


def _softmax_weights(q, k):
    s = lax.dot_general(q, k, _NT, preferred_element_type=F32)
    return jnp.exp2(s - jnp.max(s, axis=-1, keepdims=True)).astype(BF16)


def _weighted_values(p, v):
    ov = jnp.dot(p, v, preferred_element_type=F32)
    return (ov[:, :HEAD_DIM] / ov[:, HEAD_DIM:HEAD_DIM + 1]).astype(BF16)


def _attn_lat_kernel(q_ref, k_ref, v_ref, o_ref, p_ref):
    @pl.when(pl.program_id(0) == 0)
    def _():
        p_ref[...] = jnp.ones(p_ref.shape, BF16)

    k = k_ref[0]
    v = v_ref[0]
    for h in range(Q_PER_KV):
        cols = slice(h * HEAD_DIM, (h + 1) * HEAD_DIM)
        for r in range(ROW_TILE // ATTN_UNIT):
            rows = slice(r * ATTN_UNIT, (r + 1) * ATTN_UNIT)
            p_new = _softmax_weights(q_ref[0, rows, cols], k)
            o_ref[0, rows, cols] = _weighted_values(p_ref[h, rows, :], v)
            p_ref[h, rows, :] = p_new


def _attention_latent(q, k, v):
    n_tiles = N_TILES - 1
    n_units = BATCH * N_KV * n_tiles
    width = Q_PER_KV * HEAD_DIM

    def unit(u):
        return u // (N_KV * n_tiles), (u // n_tiles) % N_KV, u % n_tiles

    def q_map(t):
        b, g, i = unit(jnp.minimum(t, n_units - 1))
        return b, i + 1, g

    def k_map(t):
        b, g, _ = unit(jnp.minimum(t, n_units - 1))
        return b, 0, g

    def v_map(t):
        b, g, _ = unit(jnp.maximum(t - 1, 0))
        return b, 0, g

    def o_map(t):
        b, g, i = unit(jnp.maximum(t - 1, 0))
        return b, i, g

    return pl.pallas_call(
        _attn_lat_kernel,
        grid=(n_units + 1,),
        in_specs=[
            pl.BlockSpec((1, ROW_TILE, width), q_map),
            pl.BlockSpec((1, NTOK, HEAD_DIM), k_map),
            pl.BlockSpec((1, NTOK, V_EXT), v_map),
        ],
        out_specs=pl.BlockSpec((1, ROW_TILE, width), o_map),
        out_shape=jax.ShapeDtypeStruct((BATCH, SEQ, C_WIDTH), BF16),
        scratch_shapes=[pltpu.VMEM((Q_PER_KV, ROW_TILE, NTOK), BF16)],
        compiler_params=_params("arbitrary"),
        name="attention_latent",
    )(q, k, v)


def _attn_ctx_kernel(q_ref, k_ref, v_ref, o_ref):
    for h in range(N_Q):
        g = h // Q_PER_KV
        p = _softmax_weights(q_ref[0, :, h * HEAD_DIM:(h + 1) * HEAD_DIM],
                             k_ref[0, :, g * HEAD_DIM:(g + 1) * HEAD_DIM])
        o_ref[0, :, h * HEAD_DIM:(h + 1) * HEAD_DIM] = _weighted_values(
            p, v_ref[0, :, g * V_EXT:(g + 1) * V_EXT])


def _attention_context(q, k, v):
    spec = lambda w: pl.BlockSpec((1, CTX_LEN, w), lambda b: (b, 0, 0))
    return pl.pallas_call(
        _attn_ctx_kernel,
        grid=(BATCH,),
        in_specs=[spec(C_WIDTH), spec(KV_WIDTH), spec(N_KV * V_EXT)],
        out_specs=spec(C_WIDTH),
        out_shape=jax.ShapeDtypeStruct((BATCH, CTX_LEN, C_WIDTH), BF16),
        compiler_params=_params("parallel"),
        name="attention_context",
    )(q, k, v)


def _odd_out_kernel(olat_ref, *refs, final):
    if final:
        o = olat_ref[0]
    else:
        octx_ref, *refs = refs
        o = jnp.where(pl.program_id(1) == 0, octx_ref[0], olat_ref[0])
    gate_ref, x_ref, mod_ref, wout_ref, fg_ref, out_ref = refs
    g = gate_ref[0]
    a = (o.astype(F32) * (g * jax.nn.sigmoid(g))).astype(BF16)
    gt = mod_ref[0][:, 2 * D_MODEL:]
    x = x_ref[0] + gt * jnp.dot(a, wout_ref[...], preferred_element_type=F32)
    if final:
        x = (x * lax.rsqrt(jnp.mean(x * x, axis=-1, keepdims=True) + EPS)) * fg_ref[...]
    out_ref[0] = x


def _odd_out(o_lat, o_ctx, gate, xc, mod_l, w_out, final_g):
    final = o_ctx is None
    off = 1 if final else 0
    tile = lambda w: pl.BlockSpec((1, ROW_TILE, w), lambda b, i: (b, i + off, 0))
    full = lambda *s: pl.BlockSpec(s, lambda b, i: (0,) * len(s))
    lat = pl.BlockSpec((1, ROW_TILE, C_WIDTH), lambda b, i: (b, jnp.maximum(i + off - 1, 0), 0))
    ctx = [] if final else [pl.BlockSpec((1, CTX_LEN, C_WIDTH), lambda b, i: (b, 0, 0))]
    return pl.pallas_call(
        functools.partial(_odd_out_kernel, final=final),
        grid=(BATCH, N_TILES - off),
        in_specs=[
            lat, *ctx, tile(C_WIDTH), tile(D_MODEL),
            pl.BlockSpec((1, 1, 3 * D_MODEL), lambda b, i: (_mod_row(b, i + off), 0, 0)),
            full(D_MODEL, D_MODEL), full(1, D_MODEL),
        ],
        out_specs=pl.BlockSpec((1, ROW_TILE, D_MODEL), lambda b, i: (b, i, 0)),
        out_shape=jax.ShapeDtypeStruct((BATCH, NTOK - off * ROW_TILE, D_MODEL), F32),
        compiler_params=_params("parallel", "parallel"),
        name="odd_out_final" if final else "odd_out",
    )(o_lat, *([] if final else [o_ctx]), gate, xc, mod_l, w_out, final_g)


def kernel(x, c, ctx, c_ctx, norm_g, w_mod, b_mod, we_in, we_out, gm_v_g, gm_w_s, gm_b_s,
           s5_lam_re, s5_lam_im, s5_log_dt, s5_b_re, s5_b_im, s5_c_re, s5_c_im, s5_d,
           s5_w_glu, s5_b_glu, wo_in, wo_out, q_norm_g, k_norm_g, final_g):
    cond = jnp.concatenate([c, c_ctx[None], jnp.zeros((MOD_ROWS - BATCH - 1, D_MODEL), F32)], axis=0)
    mods = _modulation(cond, w_mod, b_mod).reshape(DEPTH, MOD_ROWS, 1, 3 * D_MODEL)
    xc = jnp.concatenate([ctx, x], axis=1)
    cos, sin = _rope_tables()
    perm = _chunk_perm()
    row = lambda a: a.reshape(1, -1)

    for layer in range(DEPTH):
        i = layer // 2
        if layer % 2 == 0:
            b_full = jnp.repeat(gm_b_s[i].T, A_GROUP_W, axis=1)
            ya, xs, gb, xg = _even_in(xc, mods[layer], row(norm_g[layer]), we_in[i].astype(BF16),
                                      row(gm_v_g[i]), gm_w_s[i].astype(BF16), b_full, perm)
            prow, pmat = _s5_params(s5_lam_re[i], s5_lam_im[i], s5_log_dt[i], s5_b_re[i],
                                    s5_b_im[i], s5_c_re[i], s5_c_im[i])
            yg = _s5(xg, prow, pmat)
            xc = _even_out(ya, xs, gb, yg, xc, mods[layer], row(s5_d[i]), s5_w_glu[i].astype(BF16),
                           row(s5_b_glu[i]), we_out[i].astype(BF16), perm.T)
        else:
            final = layer == DEPTH - 1
            w_in, q_g, k_g = _permute_heads(wo_in[i], q_norm_g[i], k_norm_g[i])
            q, k, v, gate = _odd_in(xc, mods[layer], row(norm_g[layer]), w_in.astype(BF16),
                                    row(q_g), row(k_g), cos, sin)
            o_lat = _attention_latent(q, k, v)
            o_ctx = None if final else _attention_context(q, k, v)
            xc = _odd_out(o_lat, o_ctx, gate, xc, mods[layer], wo_out[i].astype(BF16), row(final_g))
    return xc
```

```python
import functools
import math

import jax
import jax.numpy as jnp
import numpy as np
from jax import lax
from jax.experimental import pallas as pl
from jax.experimental.pallas import tpu as pltpu

F32 = jnp.float32
BF16 = jnp.bfloat16
HIGHEST = lax.Precision.HIGHEST

D_MODEL = 1024
BATCH = 4
SEQ = 4096
DEPTH = 4
GRID_W = 64
CTX_LEN = 256
EPS = 1e-6
NTOK = CTX_LEN + SEQ

LANES = 128
SUBLANES = 8
ROW_TILE = 256
N_TILES = NTOK // ROW_TILE
CTX_TILE = N_TILES - 1
LAT_TILES = N_TILES - 1
VMEM_LIMIT = 56 * 1024 * 1024

CHUNK = 128
A_WIDTH = D_MODEL // 2
A_GROUPS = 4
A_GROUP_W = A_WIDTH // A_GROUPS
B_WIDTH = D_MODEL // 2
B_CH = 16
B_GROUPS = B_WIDTH // B_CH
B_STATE = 64
EVEN_IN = 3 * A_WIDTH + 2 * B_WIDTH
HEAD_DIM = 128
N_Q = D_MODEL // HEAD_DIM
N_KV = 2
Q_PER_KV = N_Q // N_KV
C_WIDTH = N_Q * HEAD_DIM
KV_WIDTH = N_KV * HEAD_DIM
ODD_IN = 2 * C_WIDTH + 2 * KV_WIDTH
ROPE_THETA = 10000.0
ROPE_PAIRS = HEAD_DIM // 4
Q_SCALE = HEAD_DIM ** -0.5 * math.log2(math.e)
V_EXT = 2 * HEAD_DIM
ATTN_TILE = 512
ATTN_UNIT = 128

S5_Q = 16
S5_K = S5_Q * B_CH
N_CHUNKS = NTOK // S5_Q
S5_ROWS = BATCH * N_CHUNKS
TILE_CHUNKS = ROW_TILE // S5_Q
SUB_TILE = ROW_TILE // 2
SUB_CHUNKS = SUB_TILE // S5_Q
assert SUB_TILE == CHUNK
ODD_SUB = ROW_TILE
BLOCKS_PER_SAMPLE = N_CHUNKS // SUBLANES
CTX_BLOCKS = CTX_LEN // S5_Q // SUBLANES
N_PAIRS = B_GROUPS // 2
SCAN_UNROLL = 2
GROUPS_PER_VREG = LANES // B_CH
N_LAGS = 2 * S5_Q - 1

MOD_ROWS = 8
CTX_ROW = BATCH


def _params(*sem):
    return pltpu.CompilerParams(dimension_semantics=sem, vmem_limit_bytes=VMEM_LIMIT)


def _mod_row(b, i):
    return jnp.where(i == CTX_TILE, CTX_ROW, b)


def _lane_block(rows):
    return lax.broadcasted_iota(jnp.int32, (rows, LANES), 1) // B_CH


def _move_blocks(pieces, blk):
    acc = None
    for dst, (src, src_blk) in enumerate(pieces):
        sh = ((dst - src_blk) % GROUPS_PER_VREG) * B_CH
        rolled = pltpu.roll(src, sh, 1) if sh else src
        acc = rolled if acc is None else jnp.where(blk == dst, rolled, acc)
    return acc


def _merge_blocks(pick, blk):
    acc = pick(0)
    for k in range(1, GROUPS_PER_VREG):
        acc = jnp.where(blk == k, pick(k), acc)
    return acc


def _block_transpose(src, blk):
    n = GROUPS_PER_VREG
    rolled = []
    for r in range(n):
        m = _merge_blocks(lambda b: src[(b + r) % n], blk)
        rolled.append(pltpu.roll(m, r * B_CH, 1) if r else m)
    return [_merge_blocks(lambda k: rolled[(k - b) % n], blk) for b in range(n)]


def _mod_kernel(cond_ref, w_ref, b_ref, o_ref):
    c = cond_ref[...]
    s = c * jax.nn.sigmoid(c)
    o_ref[0] = jnp.dot(s, w_ref[0], preferred_element_type=F32, precision=HIGHEST) + b_ref[0]


def _modulation(cond, w_mod, b_mod):
    nblk = 3 * D_MODEL // D_MODEL
    return pl.pallas_call(
        _mod_kernel,
        grid=(DEPTH, nblk),
        in_specs=[
            pl.BlockSpec((MOD_ROWS, D_MODEL), lambda l, j: (0, 0)),
            pl.BlockSpec((1, D_MODEL, D_MODEL), lambda l, j: (l, 0, j)),
            pl.BlockSpec((1, 1, D_MODEL), lambda l, j: (l, 0, j)),
        ],
        out_specs=pl.BlockSpec((1, MOD_ROWS, D_MODEL), lambda l, j: (l, 0, j)),
        out_shape=jax.ShapeDtypeStruct((DEPTH, MOD_ROWS, 3 * D_MODEL), F32),
        compiler_params=_params("parallel", "parallel"),
        name="modulation",
    )(cond, w_mod, b_mod.reshape(DEPTH, 1, 3 * D_MODEL))


def _prologue(x, mod, g):
    sh = mod[:, :D_MODEL]
    sc = mod[:, D_MODEL:2 * D_MODEL]
    y = x * lax.rsqrt(jnp.mean(x * x, axis=-1, keepdims=True) + EPS)
    return (y * g) * (1 + sc) + sh


def _chunk_perm():
    r = jnp.arange(SUB_TILE)
    source = (r % SUB_CHUNKS) * S5_Q + r // SUB_CHUNKS
    return (source[:, None] == r[None, :]).astype(BF16)


def _even_in_kernel(x_ref, mod_ref, g_ref, win_ref, vg_ref, ws_ref, bs_ref, perm_ref,
                    ya_ref, xs_ref, gb_ref, xg_ref):
    blk = _lane_block(SUB_CHUNKS)
    xg_parts = []
    for sub in range(ROW_TILE // SUB_TILE):
        rows = slice(sub * SUB_TILE, (sub + 1) * SUB_TILE)
        h = _prologue(x_ref[0, rows, :], mod_ref[0], g_ref[...])
        z = jnp.dot(h.astype(BF16), win_ref[...], preferred_element_type=F32)
        u = z[:, 0:A_WIDTH]
        v = z[:, A_WIDTH:2 * A_WIDTH]
        ga = z[:, 2 * A_WIDTH:3 * A_WIDTH]
        xs = z[:, 3 * A_WIDTH:3 * A_WIDTH + B_WIDTH]
        xs_ref[0, rows, :] = xs
        gb_ref[0, rows, :] = z[:, 3 * A_WIDTH + B_WIDTH:]
        mu = jnp.mean(v, axis=-1, keepdims=True)
        vc = v - mu
        var = jnp.mean(vc * vc, axis=-1, keepdims=True)
        vn = ((vc * lax.rsqrt(var + EPS)) * vg_ref[...]).astype(BF16)
        mixed = jnp.concatenate(
            [jnp.dot(ws_ref[g], vn[:, g * A_GROUP_W:(g + 1) * A_GROUP_W], preferred_element_type=F32)
             for g in range(A_GROUPS)], axis=1) + bs_ref[...]
        ya_ref[0, rows, :] = ((u * mixed) * (ga * jax.nn.sigmoid(ga))).astype(BF16)

        r = jnp.dot(perm_ref[...], xs.astype(BF16), preferred_element_type=F32)
        parts = [[None] * (S5_K // LANES) for _ in range(B_GROUPS)]
        for col in range(B_WIDTH // LANES):
            for j in range(S5_K // LANES):
                steps = [r[(j * GROUPS_PER_VREG + k) * SUB_CHUNKS:(j * GROUPS_PER_VREG + k + 1) * SUB_CHUNKS,
                           col * LANES:(col + 1) * LANES] for k in range(GROUPS_PER_VREG)]
                for b, a in enumerate(_block_transpose(steps, blk)):
                    parts[col * GROUPS_PER_VREG + b][j] = a
        xg_parts.append([jnp.concatenate(p, axis=1) for p in parts])
    for g in range(B_GROUPS):
        xg_ref[g] = jnp.concatenate([parts[g] for parts in xg_parts], axis=0).astype(BF16)


def _even_in(xc, mod_l, norm_g, w_in, v_g, w_s, b_full, perm):
    tile = lambda w: pl.BlockSpec((1, ROW_TILE, w), lambda b, i: (b, i, 0))
    full = lambda *s: pl.BlockSpec(s, lambda b, i: (0,) * len(s))
    return pl.pallas_call(
        _even_in_kernel,
        grid=(BATCH, N_TILES),
        in_specs=[
            tile(D_MODEL),
            pl.BlockSpec((1, 1, 3 * D_MODEL), lambda b, i: (_mod_row(b, i), 0, 0)),
            full(1, D_MODEL),
            full(D_MODEL, EVEN_IN),
            full(1, A_WIDTH),
            full(A_GROUPS, CHUNK, CHUNK),
            full(CHUNK, A_WIDTH),
            full(SUB_TILE, SUB_TILE),
        ],
        out_specs=[tile(A_WIDTH), tile(B_WIDTH), tile(B_WIDTH),
                   pl.BlockSpec((B_GROUPS, TILE_CHUNKS, S5_K), lambda b, i: (0, b * N_TILES + i, 0))],
        out_shape=[
            jax.ShapeDtypeStruct((BATCH, NTOK, A_WIDTH), BF16),
            jax.ShapeDtypeStruct((BATCH, NTOK, B_WIDTH), F32),
            jax.ShapeDtypeStruct((BATCH, NTOK, B_WIDTH), F32),
            jax.ShapeDtypeStruct((B_GROUPS, S5_ROWS, S5_K), BF16),
        ],
        compiler_params=_params("parallel", "parallel"),
        name="even_in",
    )(xc, mod_l, norm_g, w_in, v_g, w_s, b_full, perm)


def _s5_kernel(xg_ref, prow_ref, pmat_ref, yg_ref, sl_ref, bst_ref, toep_ref, cst_ref, lhs_ref):
    lane = lax.broadcasted_iota(jnp.int32, (S5_Q, LANES), 1)
    in_group = (lane < B_STATE, lane >= B_STATE)
    blk = lane // B_CH
    zero = jnp.zeros((S5_Q, LANES), F32)

    def cmul(ar, ai, br, bi):
        return ar * br - ai * bi, ar * bi + ai * br

    w_in, c_out, scan_consts, c_rows = [], [], [], []
    for d in range(2):
        lam_re = prow_ref[0, d, 0:1, :]
        lam_im = prow_ref[0, d, 1:2, :]
        dt = jnp.exp(prow_ref[0, d, 2:3, :])
        z_re, z_im = lam_re * dt, lam_im * dt
        mag = jnp.exp(z_re)
        lb_re, lb_im = mag * jnp.cos(z_im), mag * jnp.sin(z_im)
        den = lam_re * lam_re + lam_im * lam_im
        n_re, n_im = lb_re - 1.0, lb_im
        f_re = (n_re * lam_re + n_im * lam_im) / den
        f_im = (n_im * lam_re - n_re * lam_im) / den
        b_re, b_im = pmat_ref[0, d, 0], pmat_ref[0, d, 1]
        c_re, c_im = pmat_ref[0, d, 2], pmat_ref[0, d, 3]
        bb_re, bb_im = cmul(f_re, f_im, b_re, b_im)
        c_rows.append((c_re, c_im))

        def powers(steps):
            m = jnp.exp(steps * z_re)
            return m * jnp.cos(steps * z_im), m * jnp.sin(steps * z_im)

        j = lax.broadcasted_iota(jnp.int32, (3 * SUBLANES, LANES), 0).astype(F32)
        p_re, p_im = powers(j)
        row = lambda a, k: a[k:k + 1, :]
        w_in.append([cmul(bb_re, bb_im, row(p_re, k), row(p_im, k)) for k in range(S5_Q)])
        c_out.append([cmul(c_re, c_im, row(p_re, k), row(p_im, k)) for k in range(S5_Q + 1)])

        i8 = lax.broadcasted_iota(jnp.int32, (SUBLANES, LANES), 0)
        order = i8 if d == 0 else (SUBLANES - 1) - i8
        apow = powers((order * S5_Q).astype(F32))
        bc = lambda r: powers(jnp.full((SUBLANES, LANES), r * S5_Q, F32))
        scan_consts.append((apow, [bc(1), bc(2), bc(4)], bc(SUBLANES)))

    for e in range(2):
        sel = lambda a: jnp.where(in_group[e], a, 0.0)
        for s in range(S5_Q):
            fr, fi = w_in[0][S5_Q - 1 - s]
            br, bi = w_in[1][s]
            r0 = e * S5_K + s * B_CH
            bst_ref[r0:r0 + B_CH, :] = jnp.concatenate(
                [sel(fr), sel(fi), sel(br), sel(bi)], axis=1).astype(BF16)

    for e in range(2):
        sel = lambda a: jnp.where(in_group[e], a, 0.0)
        for t in range(S5_Q):
            fr, fi = c_out[0][t + 1]
            br, bi = c_out[1][S5_Q - t]
            cst_ref[e, t * B_CH:(t + 1) * B_CH, :] = jnp.concatenate(
                [sel(fr), sel(-fi), sel(br), sel(-bi)], axis=1).astype(BF16)

    for a in range(N_LAGS + 1):
        lag = a - (S5_Q - 1)
        f = w_in[0][lag] if 0 <= lag < S5_Q else (zero, zero)
        b = w_in[1][-lag] if -S5_Q < lag <= 0 else (zero, zero)
        lhs_ref[a * B_CH:(a + 1) * B_CH, :] = jnp.concatenate([f[0], f[1], b[0], b[1]], axis=1)
    cq_rows = []
    for e in range(2):
        sel = lambda a: jnp.where(in_group[e], a, 0.0)
        (fr, fi), (br, bi) = c_rows
        cq_rows.append(jnp.concatenate([sel(fr), sel(-fi), sel(br), sel(-bi)], axis=1))
    cq = jnp.concatenate(cq_rows + [jnp.zeros((LANES - 2 * B_CH, 4 * LANES), F32)], axis=0)
    kst = lax.dot_general(lhs_ref[...], cq, (((1,), (1,)), ((), ())),
                          preferred_element_type=F32, precision=HIGHEST)
    for e in range(2):
        g_cols = []
        for col in range((N_LAGS + 1) // GROUPS_PER_VREG):
            pieces = [(kst[(col * GROUPS_PER_VREG + k) * B_CH:(col * GROUPS_PER_VREG + k + 1) * B_CH, :], e)
                      for k in range(GROUPS_PER_VREG)]
            g_cols.append(_move_blocks(pieces, blk))
        for s in range(S5_Q):
            c0, sh = divmod((S5_Q - 1 - s) * B_CH, LANES)
            if sh == 0:
                cols = g_cols[c0:c0 + 2]
            else:
                rolled = [pltpu.roll(g_cols[c0 + k], LANES - sh, 1) for k in range(3)]
                cols = [jnp.where(lane < LANES - sh, rolled[k], rolled[k + 1]) for k in range(2)]
            toep_ref[e, s * B_CH:(s + 1) * B_CH, :] = jnp.concatenate(cols, axis=1).astype(BF16)

    x0 = xg_ref[0]
    x1 = xg_ref[1]
    sl_ref[...] = jnp.dot(jnp.concatenate([x0, x1], axis=1), bst_ref[...], preferred_element_type=F32)

    i8 = lax.broadcasted_iota(jnp.int32, (SUBLANES, LANES), 0)

    def scan_block(p, carry, consts, fwd):
        (ap_re, ap_im), doubling, (a8_re, a8_im) = consts
        p_re, p_im = p

        def shifted(a, r):
            if fwd:
                return jnp.where(i8 >= r, pltpu.roll(a, r, 0), 0.0)
            return jnp.where(i8 < SUBLANES - r, pltpu.roll(a, SUBLANES - r, 0), 0.0)

        for r, (a_re, a_im) in zip((1, 2, 4), doubling):
            s_re, s_im = shifted(p_re, r), shifted(p_im, r)
            m_re, m_im = cmul(a_re, a_im, s_re, s_im)
            p_re, p_im = p_re + m_re, p_im + m_im
        c_re, c_im = carry
        e_re, e_im = cmul(ap_re, ap_im, c_re, c_im)
        entering = (shifted(p_re, 1) + e_re, shifted(p_im, 1) + e_im)
        last = SUBLANES - 1 if fwd else 0
        n_re, n_im = cmul(a8_re, a8_im, c_re, c_im)
        bcast = lambda a: jnp.broadcast_to(a[last:last + 1, :], (SUBLANES, LANES))
        return entering, (bcast(p_re) + n_re, bcast(p_im) + n_im)

    def step(j, carry):
        jf = jnp.where(j < CTX_BLOCKS, BLOCKS_PER_SAMPLE - CTX_BLOCKS + j, j - CTX_BLOCKS)
        jb = BLOCKS_PER_SAMPLE - 1 - j
        work = []
        for b in range(BATCH):
            for d, jd in enumerate((jf, jb)):
                row0 = pl.multiple_of((b * BLOCKS_PER_SAMPLE + jd) * SUBLANES, SUBLANES)
                rows, col = pl.ds(row0, SUBLANES), d * 2 * LANES
                p = (sl_ref[rows, col:col + LANES], sl_ref[rows, col + LANES:col + 2 * LANES])
                work.append((rows, col, p, carry[b][d], d))
        done = [(rows, col, scan_block(p, c, scan_consts[d], d == 0)) for rows, col, p, c, d in work]
        out = []
        for n, (rows, col, (entering, nxt)) in enumerate(done):
            sl_ref[rows, col:col + LANES] = entering[0]
            sl_ref[rows, col + LANES:col + 2 * LANES] = entering[1]
            out.append(nxt)
        return tuple((out[2 * b], out[2 * b + 1]) for b in range(BATCH))

    z8 = jnp.zeros((SUBLANES, LANES), F32)
    lax.fori_loop(0, BLOCKS_PER_SAMPLE, step, tuple(((z8, z8), (z8, z8)) for _ in range(BATCH)),
                  unroll=SCAN_UNROLL)

    sp = sl_ref[...].astype(BF16)
    nt = (((1,), (1,)), ((), ()))
    yg_ref[0] = (jnp.dot(x0, toep_ref[0], preferred_element_type=F32)
                 + lax.dot_general(sp, cst_ref[0], nt, preferred_element_type=F32))
    yg_ref[1] = (jnp.dot(x1, toep_ref[1], preferred_element_type=F32)
                 + lax.dot_general(sp, cst_ref[1], nt, preferred_element_type=F32))


def _s5_params(lam_re, lam_im, log_dt, b_re, b_im, c_re, c_im):
    def rows(a):
        return a.astype(F32).reshape(2, N_PAIRS, 1, 2 * B_STATE).transpose(1, 0, 2, 3)

    dt = jnp.broadcast_to(log_dt[..., None], lam_re.shape)
    pad = jnp.zeros((N_PAIRS, 2, SUBLANES - 3, LANES), F32)
    prow = jnp.concatenate([rows(lam_re), rows(lam_im), rows(dt), pad], axis=2)

    def mats(a, channel_axis):
        a = a.astype(F32)
        if channel_axis == 3:
            a = a.transpose(0, 1, 3, 2)
        a = a.reshape(2, N_PAIRS, 2, B_CH, B_STATE).transpose(1, 0, 3, 2, 4)
        return a.reshape(N_PAIRS, 2, B_CH, 2 * B_STATE)

    pmat = jnp.stack([mats(b_re, 3), mats(b_im, 3), mats(c_re, 2), mats(c_im, 2)], axis=2)
    return prow, pmat


def _s5(xg, prow, pmat):
    return pl.pallas_call(
        _s5_kernel,
        grid=(N_PAIRS,),
        in_specs=[
            pl.BlockSpec((2, S5_ROWS, S5_K), lambda q: (q, 0, 0)),
            pl.BlockSpec((1, 2, SUBLANES, LANES), lambda q: (q, 0, 0, 0)),
            pl.BlockSpec((1, 2, 4, B_CH, LANES), lambda q: (q, 0, 0, 0, 0)),
        ],
        out_specs=pl.BlockSpec((2, S5_ROWS, S5_K), lambda q: (q, 0, 0)),
        out_shape=jax.ShapeDtypeStruct((B_GROUPS, S5_ROWS, S5_K), F32),
        scratch_shapes=[
            pltpu.VMEM((S5_ROWS, 4 * LANES), F32),
            pltpu.VMEM((2 * S5_K, 4 * LANES), BF16),
            pltpu.VMEM((2, S5_K, S5_K), BF16),
            pltpu.VMEM((2, S5_K, 4 * LANES), BF16),
            pltpu.VMEM(((N_LAGS + 1) * B_CH, 4 * LANES), F32),
        ],
        compiler_params=_params("parallel"),
        name="s5_chunked",
    )(xg, prow, pmat)


def _even_out_kernel(ya_ref, xs_ref, gb_ref, yg_ref, x_ref, mod_ref, d_ref, wglu_ref, bglu_ref,
                     wout_ref, perm_ref, o_ref):
    blk = _lane_block(SUB_CHUNKS)
    gt = mod_ref[0][:, 2 * D_MODEL:]
    for sub in range(ROW_TILE // SUB_TILE):
        rows = slice(sub * SUB_TILE, (sub + 1) * SUB_TILE)
        chunks = slice(sub * SUB_CHUNKS, (sub + 1) * SUB_CHUNKS)
        steps = [[None] * (B_WIDTH // LANES) for _ in range(S5_Q)]
        for col in range(B_WIDTH // LANES):
            for j in range(S5_K // LANES):
                groups = [yg_ref[col * GROUPS_PER_VREG + k, chunks, j * LANES:(j + 1) * LANES]
                          for k in range(GROUPS_PER_VREG)]
                for b, a in enumerate(_block_transpose(groups, blk)):
                    steps[j * GROUPS_PER_VREG + b][col] = a
        ys = jnp.dot(perm_ref[...].astype(F32),
                     jnp.concatenate([jnp.concatenate(s, axis=1) for s in steps], axis=0),
                     preferred_element_type=F32, precision=HIGHEST)

        y = ys + d_ref[...] * xs_ref[0, rows, :]
        y = jax.nn.gelu(y)
        t = jnp.dot(y.astype(BF16), wglu_ref[...], preferred_element_type=F32) + bglu_ref[...]
        y = y * jax.nn.sigmoid(t)
        gb = gb_ref[0, rows, :]
        yb = (y * (gb * jax.nn.sigmoid(gb))).astype(BF16)
        mix = (jnp.dot(ya_ref[0, rows, :], wout_ref[0:A_WIDTH, :], preferred_element_type=F32)
               + jnp.dot(yb, wout_ref[A_WIDTH:, :], preferred_element_type=F32))
        o_ref[0, rows, :] = x_ref[0, rows, :] + gt * mix


def _even_out(ya, xs, gb, yg, xc, mod_l, d_skip, w_glu, b_glu, w_out, perm):
    tile = lambda w: pl.BlockSpec((1, ROW_TILE, w), lambda b, i: (b, i, 0))
    full = lambda *s: pl.BlockSpec(s, lambda b, i: (0,) * len(s))
    return pl.pallas_call(
        _even_out_kernel,
        grid=(BATCH, N_TILES),
        in_specs=[
            tile(A_WIDTH), tile(B_WIDTH), tile(B_WIDTH),
            pl.BlockSpec((B_GROUPS, TILE_CHUNKS, S5_K), lambda b, i: (0, b * N_TILES + i, 0)),
            tile(D_MODEL),
            pl.BlockSpec((1, 1, 3 * D_MODEL), lambda b, i: (_mod_row(b, i), 0, 0)),
            full(1, B_WIDTH), full(B_WIDTH, B_WIDTH), full(1, B_WIDTH), full(D_MODEL, D_MODEL),
            full(SUB_TILE, SUB_TILE),
        ],
        out_specs=tile(D_MODEL),
        out_shape=jax.ShapeDtypeStruct((BATCH, NTOK, D_MODEL), F32),
        compiler_params=_params("parallel", "parallel"),
        name="even_out",
    )(ya, xs, gb, yg, xc, mod_l, d_skip, w_glu, b_glu, w_out, perm)


_HEAD_ORDER = np.concatenate([np.arange(0, 32), np.arange(64, 96), np.arange(32, 64), np.arange(96, 128)])


def _rope_tables():
    t = np.arange(SEQ)
    freqs = ROPE_THETA ** (-np.arange(ROPE_PAIRS, dtype=np.float64) / ROPE_PAIRS)
    ang = np.concatenate([(t // GRID_W)[:, None] * freqs, (t % GRID_W)[:, None] * freqs], axis=1)
    cos = np.concatenate([np.cos(ang), np.cos(ang)], axis=1)
    sin = np.concatenate([-np.sin(ang), np.sin(ang)], axis=1)
    pad = lambda a, v: np.concatenate([a, np.full((CTX_LEN, HEAD_DIM), v)], axis=0).astype(np.float32)
    return jnp.asarray(pad(cos, 1.0)), jnp.asarray(pad(sin, 0.0))


def _permute_heads(w_in, q_g, k_g):
    n_heads = N_Q + N_KV
    cols = (np.arange(n_heads)[:, None] * HEAD_DIM + _HEAD_ORDER[None, :]).reshape(-1)
    cols = np.concatenate([cols, np.arange(n_heads * HEAD_DIM, ODD_IN)])
    return w_in[:, cols], q_g[_HEAD_ORDER], k_g[_HEAD_ORDER]


def _odd_in_kernel(x_ref, mod_ref, g_ref, win_ref, qg_ref, kg_ref, cos_ref, sin_ref,
                   q_ref, k_ref, v_ref, gate_ref):
    ones = jnp.ones((ODD_SUB, HEAD_DIM), BF16)
    for sub in range(ROW_TILE // ODD_SUB):
        rows = slice(sub * ODD_SUB, (sub + 1) * ODD_SUB)
        h = _prologue(x_ref[0, rows, :], mod_ref[0], g_ref[...])
        z = jnp.dot(h.astype(BF16), win_ref[...], preferred_element_type=F32)
        cos = cos_ref[rows, :]
        sin = sin_ref[rows, :]

        def norm_rope(xh, gain, scale):
            xn = (xh * lax.rsqrt(jnp.mean(xh * xh, axis=-1, keepdims=True) + EPS)) * gain
            out = xn * cos + pltpu.roll(xn, HEAD_DIM // 2, 1) * sin
            return (out * scale).astype(BF16) if scale is not None else out.astype(BF16)

        for hq in range(N_Q):
            q_ref[0, rows, hq * HEAD_DIM:(hq + 1) * HEAD_DIM] = norm_rope(
                z[:, hq * HEAD_DIM:(hq + 1) * HEAD_DIM], qg_ref[...], Q_SCALE)
        for hk in range(N_KV):
            lo = C_WIDTH + hk * HEAD_DIM
            k_ref[0, rows, hk * HEAD_DIM:(hk + 1) * HEAD_DIM] = norm_rope(
                z[:, lo:lo + HEAD_DIM], kg_ref[...], None)
            lo = C_WIDTH + KV_WIDTH + hk * HEAD_DIM
            v_ref[0, rows, hk * V_EXT:hk * V_EXT + HEAD_DIM] = z[:, lo:lo + HEAD_DIM].astype(BF16)
            v_ref[0, rows, hk * V_EXT + HEAD_DIM:(hk + 1) * V_EXT] = ones
        gate_ref[0, rows, :] = z[:, C_WIDTH + 2 * KV_WIDTH:]


def _odd_in(xc, mod_l, norm_g, w_in, q_g, k_g, cos, sin):
    tile = lambda w: pl.BlockSpec((1, ROW_TILE, w), lambda b, i: (b, i, 0))
    full = lambda *s: pl.BlockSpec(s, lambda b, i: (0,) * len(s))
    tab = pl.BlockSpec((ROW_TILE, HEAD_DIM), lambda b, i: (i, 0))
    return pl.pallas_call(
        _odd_in_kernel,
        grid=(BATCH, N_TILES),
        in_specs=[
            tile(D_MODEL),
            pl.BlockSpec((1, 1, 3 * D_MODEL), lambda b, i: (_mod_row(b, i), 0, 0)),
            full(1, D_MODEL), full(D_MODEL, ODD_IN), full(1, HEAD_DIM), full(1, HEAD_DIM),
            tab, tab,
        ],
        out_specs=[tile(C_WIDTH), tile(KV_WIDTH), tile(N_KV * V_EXT), tile(C_WIDTH)],
        out_shape=[
            jax.ShapeDtypeStruct((BATCH, NTOK, C_WIDTH), BF16),
            jax.ShapeDtypeStruct((BATCH, NTOK, KV_WIDTH), BF16),
            jax.ShapeDtypeStruct((BATCH, NTOK, N_KV * V_EXT), BF16),
            jax.ShapeDtypeStruct((BATCH, NTOK, C_WIDTH), F32),
        ],
        compiler_params=_params("parallel", "parallel"),
        name="odd_in",
    )(xc, mod_l, norm_g, w_in, q_g, k_g, cos, sin)


_NT = (((1,), (1,)), ((), ()))


def _softmax_weights(q, k):
    s = lax.dot_general(q, k, _NT, preferred_element_type=F32)
    return jnp.exp2(s - jnp.max(s, axis=-1, keepdims=True)).astype(BF16)


def _weighted_values(p, v):
    ov = jnp.dot(p, v, preferred_element_type=F32)
    return (ov[:, :HEAD_DIM] / ov[:, HEAD_DIM:HEAD_DIM + 1]).astype(BF16)


def _attn_lat_kernel(q_ref, k_ref, v_ref, o_ref, p_ref):
    @pl.when(pl.program_id(0) == 0)
    def _():
        p_ref[...] = jnp.ones(p_ref.shape, BF16)

    k = k_ref[0]
    v = v_ref[0]
    for h in range(Q_PER_KV):
        cols = slice(h * HEAD_DIM, (h + 1) * HEAD_DIM)
        for r in range(ATTN_TILE // ATTN_UNIT):
            rows = slice(r * ATTN_UNIT, (r + 1) * ATTN_UNIT)
            p_new = _softmax_weights(q_ref[0, rows, cols], k)
            o_ref[0, rows, cols] = _weighted_values(p_ref[h, rows, :], v)
            p_ref[h, rows, :] = p_new


def _attention_latent(q, k, v):
    n_tiles = SEQ // ATTN_TILE
    n_units = BATCH * N_KV * n_tiles
    width = Q_PER_KV * HEAD_DIM

    def unit(u):
        return u // (N_KV * n_tiles), (u // n_tiles) % N_KV, u % n_tiles

    def q_map(t):
        b, g, i = unit(jnp.minimum(t, n_units - 1))
        return b, i, g

    def k_map(t):
        b, g, _ = unit(jnp.minimum(t, n_units - 1))
        return b, 0, g

    def v_map(t):
        b, g, _ = unit(jnp.maximum(t - 1, 0))
        return b, 0, g

    def o_map(t):
        b, g, i = unit(jnp.maximum(t - 1, 0))
        return b, i, g

    return pl.pallas_call(
        _attn_lat_kernel,
        grid=(n_units + 1,),
        in_specs=[
            pl.BlockSpec((1, ATTN_TILE, width), q_map),
            pl.BlockSpec((1, NTOK, HEAD_DIM), k_map),
            pl.BlockSpec((1, NTOK, V_EXT), v_map),
        ],
        out_specs=pl.BlockSpec((1, ATTN_TILE, width), o_map),
        out_shape=jax.ShapeDtypeStruct((BATCH, SEQ, C_WIDTH), BF16),
        scratch_shapes=[pltpu.VMEM((Q_PER_KV, ATTN_TILE, NTOK), BF16)],
        compiler_params=_params("arbitrary"),
        name="attention_latent",
    )(q, k, v)


def _attn_ctx_kernel(q_ref, k_ref, v_ref, o_ref):
    for h in range(N_Q):
        g = h // Q_PER_KV
        p = _softmax_weights(q_ref[0, :, h * HEAD_DIM:(h + 1) * HEAD_DIM],
                             k_ref[0, :, g * HEAD_DIM:(g + 1) * HEAD_DIM])
        o_ref[0, :, h * HEAD_DIM:(h + 1) * HEAD_DIM] = _weighted_values(
            p, v_ref[0, :, g * V_EXT:(g + 1) * V_EXT])


def _attention_context(q, k, v):
    spec = lambda w: pl.BlockSpec((1, CTX_LEN, w), lambda b: (b, CTX_TILE, 0))
    return pl.pallas_call(
        _attn_ctx_kernel,
        grid=(BATCH,),
        in_specs=[spec(C_WIDTH), spec(KV_WIDTH), spec(N_KV * V_EXT)],
        out_specs=pl.BlockSpec((1, CTX_LEN, C_WIDTH), lambda b: (b, 0, 0)),
        out_shape=jax.ShapeDtypeStruct((BATCH, CTX_LEN, C_WIDTH), BF16),
        compiler_params=_params("parallel"),
        name="attention_context",
    )(q, k, v)


def _odd_out_kernel(olat_ref, *refs, final):
    if final:
        o = olat_ref[0]
    else:
        octx_ref, *refs = refs
        o = jnp.where(pl.program_id(1) == CTX_TILE, octx_ref[0], olat_ref[0])
    gate_ref, x_ref, mod_ref, wout_ref, fg_ref, out_ref = refs
    g = gate_ref[0]
    a = (o.astype(F32) * (g * jax.nn.sigmoid(g))).astype(BF16)
    gt = mod_ref[0][:, 2 * D_MODEL:]
    x = x_ref[0] + gt * jnp.dot(a, wout_ref[...], preferred_element_type=F32)
    if final:
        x = (x * lax.rsqrt(jnp.mean(x * x, axis=-1, keepdims=True) + EPS)) * fg_ref[...]
    out_ref[0] = x


def _odd_out(o_lat, o_ctx, gate, xc, mod_l, w_out, final_g):
    final = o_ctx is None
    n_tiles = LAT_TILES if final else N_TILES
    tile = lambda w: pl.BlockSpec((1, ROW_TILE, w), lambda b, i: (b, i, 0))
    full = lambda *s: pl.BlockSpec(s, lambda b, i: (0,) * len(s))
    lat = pl.BlockSpec((1, ROW_TILE, C_WIDTH), lambda b, i: (b, jnp.minimum(i, LAT_TILES - 1), 0))
    ctx = [] if final else [pl.BlockSpec((1, CTX_LEN, C_WIDTH), lambda b, i: (b, 0, 0))]
    return pl.pallas_call(
        functools.partial(_odd_out_kernel, final=final),
        grid=(BATCH, n_tiles),
        in_specs=[
            lat, *ctx, tile(C_WIDTH), tile(D_MODEL),
            pl.BlockSpec((1, 1, 3 * D_MODEL), lambda b, i: (_mod_row(b, i), 0, 0)),
            full(D_MODEL, D_MODEL), full(1, D_MODEL),
        ],
        out_specs=tile(D_MODEL),
        out_shape=jax.ShapeDtypeStruct((BATCH, n_tiles * ROW_TILE, D_MODEL), F32),
        compiler_params=_params("parallel", "parallel"),
        name="odd_out_final" if final else "odd_out",
    )(o_lat, *([] if final else [o_ctx]), gate, xc, mod_l, w_out, final_g)


def kernel(x, c, ctx, c_ctx, norm_g, w_mod, b_mod, we_in, we_out, gm_v_g, gm_w_s, gm_b_s,
           s5_lam_re, s5_lam_im, s5_log_dt, s5_b_re, s5_b_im, s5_c_re, s5_c_im, s5_d,
           s5_w_glu, s5_b_glu, wo_in, wo_out, q_norm_g, k_norm_g, final_g):
    cond = jnp.concatenate([c, c_ctx[None], jnp.zeros((MOD_ROWS - BATCH - 1, D_MODEL), F32)], axis=0)
    mods = _modulation(cond, w_mod, b_mod).reshape(DEPTH, MOD_ROWS, 1, 3 * D_MODEL)
    xc = jnp.concatenate([x, ctx], axis=1)
    cos, sin = _rope_tables()
    perm = _chunk_perm()
    row = lambda a: a.reshape(1, -1)

    for layer in range(DEPTH):
        i = layer // 2
        if layer % 2 == 0:
            b_full = jnp.repeat(gm_b_s[i].T, A_GROUP_W, axis=1)
            ya, xs, gb, xg = _even_in(xc, mods[layer], row(norm_g[layer]), we_in[i].astype(BF16),
                                      row(gm_v_g[i]), gm_w_s[i].astype(BF16), b_full, perm)
            prow, pmat = _s5_params(s5_lam_re[i], s5_lam_im[i], s5_log_dt[i], s5_b_re[i],
                                    s5_b_im[i], s5_c_re[i], s5_c_im[i])
            yg = _s5(xg, prow, pmat)
            xc = _even_out(ya, xs, gb, yg, xc, mods[layer], row(s5_d[i]), s5_w_glu[i].astype(BF16),
                           row(s5_b_glu[i]), we_out[i].astype(BF16), perm.T)
        else:
            final = layer == DEPTH - 1
            w_in, q_g, k_g = _permute_heads(wo_in[i], q_norm_g[i], k_norm_g[i])
            q, k, v, gate = _odd_in(xc, mods[layer], row(norm_g[layer]), w_in.astype(BF16),
                                    row(q_g), row(k_g), cos, sin)
            o_lat = _attention_latent(q, k, v)
            o_ctx = None if final else _attention_context(q, k, v)
            xc = _odd_out(o_lat, o_ctx, gate, xc, mods[layer], wo_out[i].astype(BF16), row(final_g))
    return xc
```

```python
import functools
import math

import jax
import jax.numpy as jnp
import numpy as np
from jax import lax
from jax.experimental import pallas as pl
from jax.experimental.pallas import tpu as pltpu

F32 = jnp.float32
BF16 = jnp.bfloat16
HIGHEST = lax.Precision.HIGHEST

D_MODEL = 1024
BATCH = 4
SEQ = 4096
DEPTH = 4
GRID_W = 64
CTX_LEN = 256
EPS = 1e-6
NTOK = CTX_LEN + SEQ

LANES = 128
SUBLANES = 8
ROW_TILE = 256
N_TILES = NTOK // ROW_TILE
CTX_TILE = N_TILES - 1
LAT_TILES = N_TILES - 1
BIG_TILE = 1024
PIECES = BIG_TILE // ROW_TILE
N_BIG = BATCH * NTOK // BIG_TILE
assert N_BIG * BIG_TILE == BATCH * NTOK and SEQ % BIG_TILE == 0
VMEM_LIMIT = 56 * 1024 * 1024

CHUNK = 128
A_WIDTH = D_MODEL // 2
A_GROUPS = 4
A_GROUP_W = A_WIDTH // A_GROUPS
B_WIDTH = D_MODEL // 2
B_CH = 16
B_GROUPS = B_WIDTH // B_CH
B_STATE = 64
EVEN_IN = 3 * A_WIDTH + 2 * B_WIDTH
HEAD_DIM = 128
N_Q = D_MODEL // HEAD_DIM
N_KV = 2
Q_PER_KV = N_Q // N_KV
C_WIDTH = N_Q * HEAD_DIM
KV_WIDTH = N_KV * HEAD_DIM
ODD_IN = 2 * C_WIDTH + 2 * KV_WIDTH
ROPE_THETA = 10000.0
ROPE_PAIRS = HEAD_DIM // 4
Q_SCALE = HEAD_DIM ** -0.5 * math.log2(math.e)
V_EXT = 2 * HEAD_DIM
ATTN_TILE = 512
ATTN_UNIT = 128

S5_Q = 16
S5_K = S5_Q * B_CH
N_CHUNKS = NTOK // S5_Q
S5_ROWS = BATCH * N_CHUNKS
TILE_CHUNKS = ROW_TILE // S5_Q
SUB_TILE = ROW_TILE // 2
SUB_CHUNKS = SUB_TILE // S5_Q
assert SUB_TILE == CHUNK
ODD_SUB = ROW_TILE
BLOCKS_PER_SAMPLE = N_CHUNKS // SUBLANES
CTX_BLOCKS = CTX_LEN // S5_Q // SUBLANES
N_PAIRS = B_GROUPS // 2
SCAN_UNROLL = 2
GROUPS_PER_VREG = LANES // B_CH
N_LAGS = 2 * S5_Q - 1

MOD_ROWS = 8
CTX_ROW = BATCH


def _params(*sem):
    return pltpu.CompilerParams(dimension_semantics=sem, vmem_limit_bytes=VMEM_LIMIT)


def _mod_row(b, i):
    return jnp.where(i == CTX_TILE, CTX_ROW, b)


def _lane_block(rows):
    return lax.broadcasted_iota(jnp.int32, (rows, LANES), 1) // B_CH


def _move_blocks(pieces, blk):
    acc = None
    for dst, (src, src_blk) in enumerate(pieces):
        sh = ((dst - src_blk) % GROUPS_PER_VREG) * B_CH
        rolled = pltpu.roll(src, sh, 1) if sh else src
        acc = rolled if acc is None else jnp.where(blk == dst, rolled, acc)
    return acc


def _merge_blocks(pick, blk):
    acc = pick(0)
    for k in range(1, GROUPS_PER_VREG):
        acc = jnp.where(blk == k, pick(k), acc)
    return acc


def _block_transpose(src, blk):
    n = GROUPS_PER_VREG
    rolled = []
    for r in range(n):
        m = _merge_blocks(lambda b: src[(b + r) % n], blk)
        rolled.append(pltpu.roll(m, r * B_CH, 1) if r else m)
    return [_merge_blocks(lambda k: rolled[(k - b) % n], blk) for b in range(n)]


def _mod_kernel(cond_ref, w_ref, b_ref, o_ref):
    c = cond_ref[...]
    s = c * jax.nn.sigmoid(c)
    o_ref[0] = jnp.dot(s, w_ref[0], preferred_element_type=F32, precision=HIGHEST) + b_ref[0]


def _modulation(cond, w_mod, b_mod):
    nblk = 3 * D_MODEL // D_MODEL
    return pl.pallas_call(
        _mod_kernel,
        grid=(DEPTH, nblk),
        in_specs=[
            pl.BlockSpec((MOD_ROWS, D_MODEL), lambda l, j: (0, 0)),
            pl.BlockSpec((1, D_MODEL, D_MODEL), lambda l, j: (l, 0, j)),
            pl.BlockSpec((1, 1, D_MODEL), lambda l, j: (l, 0, j)),
        ],
        out_specs=pl.BlockSpec((1, MOD_ROWS, D_MODEL), lambda l, j: (l, 0, j)),
        out_shape=jax.ShapeDtypeStruct((DEPTH, MOD_ROWS, 3 * D_MODEL), F32),
        compiler_params=_params("parallel", "parallel"),
        name="modulation",
    )(cond, w_mod, b_mod.reshape(DEPTH, 1, 3 * D_MODEL))


def _prologue(x, mod, g):
    sh = mod[:, :D_MODEL]
    sc = mod[:, D_MODEL:2 * D_MODEL]
    y = x * lax.rsqrt(jnp.mean(x * x, axis=-1, keepdims=True) + EPS)
    return (y * g) * (1 + sc) + sh


def _chunk_perm():
    r = jnp.arange(SUB_TILE)
    source = (r % SUB_CHUNKS) * S5_Q + r // SUB_CHUNKS
    return (source[:, None] == r[None, :]).astype(BF16)


def _even_in_kernel(x_ref, mod_ref, g_ref, win_ref, vg_ref, ws_ref, bs_ref, perm_ref,
                    ya_ref, xs_ref, gb_ref, xg_ref):
    blk = _lane_block(SUB_CHUNKS)
    xg_parts = []
    for sub in range(BIG_TILE // SUB_TILE):
        rows = slice(sub * SUB_TILE, (sub + 1) * SUB_TILE)
        h = _prologue(x_ref[rows, :], mod_ref[sub * SUB_TILE // ROW_TILE], g_ref[...])
        z = jnp.dot(h.astype(BF16), win_ref[...], preferred_element_type=F32)
        u = z[:, 0:A_WIDTH]
        v = z[:, A_WIDTH:2 * A_WIDTH]
        ga = z[:, 2 * A_WIDTH:3 * A_WIDTH]
        xs = z[:, 3 * A_WIDTH:3 * A_WIDTH + B_WIDTH]
        xs_ref[rows, :] = xs
        gb_ref[rows, :] = z[:, 3 * A_WIDTH + B_WIDTH:]
        mu = jnp.mean(v, axis=-1, keepdims=True)
        vc = v - mu
        var = jnp.mean(vc * vc, axis=-1, keepdims=True)
        vn = ((vc * lax.rsqrt(var + EPS)) * vg_ref[...]).astype(BF16)
        mixed = jnp.concatenate(
            [jnp.dot(ws_ref[g], vn[:, g * A_GROUP_W:(g + 1) * A_GROUP_W], preferred_element_type=F32)
             for g in range(A_GROUPS)], axis=1) + bs_ref[...]
        ya_ref[rows, :] = ((u * mixed) * (ga * jax.nn.sigmoid(ga))).astype(BF16)

        r = jnp.dot(perm_ref[...], xs.astype(BF16), preferred_element_type=F32)
        parts = [[None] * (S5_K // LANES) for _ in range(B_GROUPS)]
        for col in range(B_WIDTH // LANES):
            for j in range(S5_K // LANES):
                steps = [r[(j * GROUPS_PER_VREG + k) * SUB_CHUNKS:(j * GROUPS_PER_VREG + k + 1) * SUB_CHUNKS,
                           col * LANES:(col + 1) * LANES] for k in range(GROUPS_PER_VREG)]
                for b, a in enumerate(_block_transpose(steps, blk)):
                    parts[col * GROUPS_PER_VREG + b][j] = a
        xg_parts.append([jnp.concatenate(p, axis=1) for p in parts])
        if len(xg_parts) == 2:
            first = (sub - 1) * SUB_CHUNKS
            for g in range(B_GROUPS):
                xg_ref[g, first:first + 2 * SUB_CHUNKS, :] = jnp.concatenate(
                    [xg_parts[0][g], xg_parts[1][g]], axis=0).astype(BF16)
            xg_parts = []


def _tile(w):
    return pl.BlockSpec((BIG_TILE, w), lambda t: (t, 0))


def _full(*shape):
    return pl.BlockSpec(shape, lambda t: (0,) * len(shape))


_MOD_SPEC = pl.BlockSpec((PIECES, 1, 3 * D_MODEL), lambda t: (t, 0, 0))
_GROUP_SPEC = pl.BlockSpec((B_GROUPS, BIG_TILE // S5_Q, S5_K), lambda t: (0, t, 0))


def _even_in(xc, mod_p, norm_g, w_in, v_g, w_s, b_full, perm):
    return pl.pallas_call(
        _even_in_kernel,
        grid=(N_BIG,),
        in_specs=[
            _tile(D_MODEL), _MOD_SPEC, _full(1, D_MODEL), _full(D_MODEL, EVEN_IN), _full(1, A_WIDTH),
            _full(A_GROUPS, CHUNK, CHUNK), _full(CHUNK, A_WIDTH), _full(SUB_TILE, SUB_TILE),
        ],
        out_specs=[_tile(A_WIDTH), _tile(B_WIDTH), _tile(B_WIDTH), _GROUP_SPEC],
        out_shape=[
            jax.ShapeDtypeStruct((BATCH * NTOK, A_WIDTH), BF16),
            jax.ShapeDtypeStruct((BATCH * NTOK, B_WIDTH), F32),
            jax.ShapeDtypeStruct((BATCH * NTOK, B_WIDTH), F32),
            jax.ShapeDtypeStruct((B_GROUPS, S5_ROWS, S5_K), BF16),
        ],
        compiler_params=_params("parallel"),
        name="even_in",
    )(xc, mod_p, norm_g, w_in, v_g, w_s, b_full, perm)


def _s5_kernel(xg_ref, prow_ref, pmat_ref, yg_ref, sl_ref, bst_ref, toep_ref, cst_ref, lhs_ref):
    lane = lax.broadcasted_iota(jnp.int32, (S5_Q, LANES), 1)
    in_group = (lane < B_STATE, lane >= B_STATE)
    blk = lane // B_CH
    zero = jnp.zeros((S5_Q, LANES), F32)

    def cmul(ar, ai, br, bi):
        return ar * br - ai * bi, ar * bi + ai * br

    w_in, c_out, scan_consts, c_rows = [], [], [], []
    for d in range(2):
        lam_re = prow_ref[0, d, 0:1, :]
        lam_im = prow_ref[0, d, 1:2, :]
        dt = jnp.exp(prow_ref[0, d, 2:3, :])
        z_re, z_im = lam_re * dt, lam_im * dt
        mag = jnp.exp(z_re)
        lb_re, lb_im = mag * jnp.cos(z_im), mag * jnp.sin(z_im)
        den = lam_re * lam_re + lam_im * lam_im
        n_re, n_im = lb_re - 1.0, lb_im
        f_re = (n_re * lam_re + n_im * lam_im) / den
        f_im = (n_im * lam_re - n_re * lam_im) / den
        b_re, b_im = pmat_ref[0, d, 0], pmat_ref[0, d, 1]
        c_re, c_im = pmat_ref[0, d, 2], pmat_ref[0, d, 3]
        bb_re, bb_im = cmul(f_re, f_im, b_re, b_im)
        c_rows.append((c_re, c_im))

        def powers(steps):
            m = jnp.exp(steps * z_re)
            return m * jnp.cos(steps * z_im), m * jnp.sin(steps * z_im)

        j = lax.broadcasted_iota(jnp.int32, (3 * SUBLANES, LANES), 0).astype(F32)
        p_re, p_im = powers(j)
        row = lambda a, k: a[k:k + 1, :]
        w_in.append([cmul(bb_re, bb_im, row(p_re, k), row(p_im, k)) for k in range(S5_Q)])
        c_out.append([cmul(c_re, c_im, row(p_re, k), row(p_im, k)) for k in range(S5_Q + 1)])

        i8 = lax.broadcasted_iota(jnp.int32, (SUBLANES, LANES), 0)
        order = i8 if d == 0 else (SUBLANES - 1) - i8
        apow = powers((order * S5_Q).astype(F32))
        bc = lambda r: powers(jnp.full((SUBLANES, LANES), r * S5_Q, F32))
        scan_consts.append((apow, [bc(1), bc(2), bc(4)], bc(SUBLANES)))

    for e in range(2):
        sel = lambda a: jnp.where(in_group[e], a, 0.0)
        for s in range(S5_Q):
            fr, fi = w_in[0][S5_Q - 1 - s]
            br, bi = w_in[1][s]
            r0 = e * S5_K + s * B_CH
            bst_ref[r0:r0 + B_CH, :] = jnp.concatenate(
                [sel(fr), sel(fi), sel(br), sel(bi)], axis=1).astype(BF16)

    for e in range(2):
        sel = lambda a: jnp.where(in_group[e], a, 0.0)
        for t in range(S5_Q):
            fr, fi = c_out[0][t + 1]
            br, bi = c_out[1][S5_Q - t]
            cst_ref[e, t * B_CH:(t + 1) * B_CH, :] = jnp.concatenate(
                [sel(fr), sel(-fi), sel(br), sel(-bi)], axis=1).astype(BF16)

    for a in range(N_LAGS + 1):
        lag = a - (S5_Q - 1)
        f = w_in[0][lag] if 0 <= lag < S5_Q else (zero, zero)
        b = w_in[1][-lag] if -S5_Q < lag <= 0 else (zero, zero)
        lhs_ref[a * B_CH:(a + 1) * B_CH, :] = jnp.concatenate([f[0], f[1], b[0], b[1]], axis=1)
    cq_rows = []
    for e in range(2):
        sel = lambda a: jnp.where(in_group[e], a, 0.0)
        (fr, fi), (br, bi) = c_rows
        cq_rows.append(jnp.concatenate([sel(fr), sel(-fi), sel(br), sel(-bi)], axis=1))
    cq = jnp.concatenate(cq_rows + [jnp.zeros((LANES - 2 * B_CH, 4 * LANES), F32)], axis=0)
    kst = lax.dot_general(lhs_ref[...], cq, (((1,), (1,)), ((), ())),
                          preferred_element_type=F32, precision=HIGHEST)
    for e in range(2):
        g_cols = []
        for col in range((N_LAGS + 1) // GROUPS_PER_VREG):
            pieces = [(kst[(col * GROUPS_PER_VREG + k) * B_CH:(col * GROUPS_PER_VREG + k + 1) * B_CH, :], e)
                      for k in range(GROUPS_PER_VREG)]
            g_cols.append(_move_blocks(pieces, blk))
        for s in range(S5_Q):
            c0, sh = divmod((S5_Q - 1 - s) * B_CH, LANES)
            if sh == 0:
                cols = g_cols[c0:c0 + 2]
            else:
                rolled = [pltpu.roll(g_cols[c0 + k], LANES - sh, 1) for k in range(3)]
                cols = [jnp.where(lane < LANES - sh, rolled[k], rolled[k + 1]) for k in range(2)]
            toep_ref[e, s * B_CH:(s + 1) * B_CH, :] = jnp.concatenate(cols, axis=1).astype(BF16)

    x0 = xg_ref[0]
    x1 = xg_ref[1]
    sl_ref[...] = jnp.dot(jnp.concatenate([x0, x1], axis=1), bst_ref[...], preferred_element_type=F32)

    i8 = lax.broadcasted_iota(jnp.int32, (SUBLANES, LANES), 0)

    def scan_block(p, carry, consts, fwd):
        (ap_re, ap_im), doubling, (a8_re, a8_im) = consts
        p_re, p_im = p

        def shifted(a, r):
            if fwd:
                return jnp.where(i8 >= r, pltpu.roll(a, r, 0), 0.0)
            return jnp.where(i8 < SUBLANES - r, pltpu.roll(a, SUBLANES - r, 0), 0.0)

        for r, (a_re, a_im) in zip((1, 2, 4), doubling):
            s_re, s_im = shifted(p_re, r), shifted(p_im, r)
            m_re, m_im = cmul(a_re, a_im, s_re, s_im)
            p_re, p_im = p_re + m_re, p_im + m_im
        c_re, c_im = carry
        e_re, e_im = cmul(ap_re, ap_im, c_re, c_im)
        entering = (shifted(p_re, 1) + e_re, shifted(p_im, 1) + e_im)
        last = SUBLANES - 1 if fwd else 0
        n_re, n_im = cmul(a8_re, a8_im, c_re, c_im)
        bcast = lambda a: jnp.broadcast_to(a[last:last + 1, :], (SUBLANES, LANES))
        return entering, (bcast(p_re) + n_re, bcast(p_im) + n_im)

    def step(j, carry):
        jf = jnp.where(j < CTX_BLOCKS, BLOCKS_PER_SAMPLE - CTX_BLOCKS + j, j - CTX_BLOCKS)
        jb = BLOCKS_PER_SAMPLE - 1 - j
        work = []
        for b in range(BATCH):
            for d, jd in enumerate((jf, jb)):
                row0 = pl.multiple_of((b * BLOCKS_PER_SAMPLE + jd) * SUBLANES, SUBLANES)
                rows, col = pl.ds(row0, SUBLANES), d * 2 * LANES
                p = (sl_ref[rows, col:col + LANES], sl_ref[rows, col + LANES:col + 2 * LANES])
                work.append((rows, col, p, carry[b][d], d))
        done = [(rows, col, scan_block(p, c, scan_consts[d], d == 0)) for rows, col, p, c, d in work]
        out = []
        for n, (rows, col, (entering, nxt)) in enumerate(done):
            sl_ref[rows, col:col + LANES] = entering[0]
            sl_ref[rows, col + LANES:col + 2 * LANES] = entering[1]
            out.append(nxt)
        return tuple((out[2 * b], out[2 * b + 1]) for b in range(BATCH))

    z8 = jnp.zeros((SUBLANES, LANES), F32)
    lax.fori_loop(0, BLOCKS_PER_SAMPLE, step, tuple(((z8, z8), (z8, z8)) for _ in range(BATCH)),
                  unroll=SCAN_UNROLL)

    sp = sl_ref[...].astype(BF16)
    nt = (((1,), (1,)), ((), ()))
    yg_ref[0] = (jnp.dot(x0, toep_ref[0], preferred_element_type=F32)
                 + lax.dot_general(sp, cst_ref[0], nt, preferred_element_type=F32))
    yg_ref[1] = (jnp.dot(x1, toep_ref[1], preferred_element_type=F32)
                 + lax.dot_general(sp, cst_ref[1], nt, preferred_element_type=F32))


def _s5_params(lam_re, lam_im, log_dt, b_re, b_im, c_re, c_im):
    def rows(a):
        return a.astype(F32).reshape(2, N_PAIRS, 1, 2 * B_STATE).transpose(1, 0, 2, 3)

    dt = jnp.broadcast_to(log_dt[..., None], lam_re.shape)
    pad = jnp.zeros((N_PAIRS, 2, SUBLANES - 3, LANES), F32)
    prow = jnp.concatenate([rows(lam_re), rows(lam_im), rows(dt), pad], axis=2)

    def mats(a, channel_axis):
        a = a.astype(F32)
        if channel_axis == 3:
            a = a.transpose(0, 1, 3, 2)
        a = a.reshape(2, N_PAIRS, 2, B_CH, B_STATE).transpose(1, 0, 3, 2, 4)
        return a.reshape(N_PAIRS, 2, B_CH, 2 * B_STATE)

    pmat = jnp.stack([mats(b_re, 3), mats(b_im, 3), mats(c_re, 2), mats(c_im, 2)], axis=2)
    return prow, pmat


def _s5(xg, prow, pmat):
    return pl.pallas_call(
        _s5_kernel,
        grid=(N_PAIRS,),
        in_specs=[
            pl.BlockSpec((2, S5_ROWS, S5_K), lambda q: (q, 0, 0)),
            pl.BlockSpec((1, 2, SUBLANES, LANES), lambda q: (q, 0, 0, 0)),
            pl.BlockSpec((1, 2, 4, B_CH, LANES), lambda q: (q, 0, 0, 0, 0)),
        ],
        out_specs=pl.BlockSpec((2, S5_ROWS, S5_K), lambda q: (q, 0, 0)),
        out_shape=jax.ShapeDtypeStruct((B_GROUPS, S5_ROWS, S5_K), F32),
        scratch_shapes=[
            pltpu.VMEM((S5_ROWS, 4 * LANES), F32),
            pltpu.VMEM((2 * S5_K, 4 * LANES), BF16),
            pltpu.VMEM((2, S5_K, S5_K), BF16),
            pltpu.VMEM((2, S5_K, 4 * LANES), BF16),
            pltpu.VMEM(((N_LAGS + 1) * B_CH, 4 * LANES), F32),
        ],
        compiler_params=_params("parallel"),
        name="s5_chunked",
    )(xg, prow, pmat)


def _even_out_kernel(ya_ref, xs_ref, gb_ref, yg_ref, x_ref, mod_ref, d_ref, wglu_ref, bglu_ref,
                     wout_ref, perm_ref, o_ref):
    blk = _lane_block(SUB_CHUNKS)
    for sub in range(BIG_TILE // SUB_TILE):
        rows = slice(sub * SUB_TILE, (sub + 1) * SUB_TILE)
        chunks = slice(sub * SUB_CHUNKS, (sub + 1) * SUB_CHUNKS)
        gt = mod_ref[sub * SUB_TILE // ROW_TILE][:, 2 * D_MODEL:]
        steps = [[None] * (B_WIDTH // LANES) for _ in range(S5_Q)]
        for col in range(B_WIDTH // LANES):
            for j in range(S5_K // LANES):
                groups = [yg_ref[col * GROUPS_PER_VREG + k, chunks, j * LANES:(j + 1) * LANES]
                          for k in range(GROUPS_PER_VREG)]
                for b, a in enumerate(_block_transpose(groups, blk)):
                    steps[j * GROUPS_PER_VREG + b][col] = a
        ys = jnp.dot(perm_ref[...].astype(F32),
                     jnp.concatenate([jnp.concatenate(s, axis=1) for s in steps], axis=0),
                     preferred_element_type=F32, precision=HIGHEST)

        y = ys + d_ref[...] * xs_ref[rows, :]
        y = jax.nn.gelu(y)
        t = jnp.dot(y.astype(BF16), wglu_ref[...], preferred_element_type=F32) + bglu_ref[...]
        y = y * jax.nn.sigmoid(t)
        gb = gb_ref[rows, :]
        yb = (y * (gb * jax.nn.sigmoid(gb))).astype(BF16)
        mix = (jnp.dot(ya_ref[rows, :], wout_ref[0:A_WIDTH, :], preferred_element_type=F32)
               + jnp.dot(yb, wout_ref[A_WIDTH:, :], preferred_element_type=F32))
        o_ref[rows, :] = x_ref[rows, :] + gt * mix


def _even_out(ya, xs, gb, yg, xc, mod_p, d_skip, w_glu, b_glu, w_out, perm):
    return pl.pallas_call(
        _even_out_kernel,
        grid=(N_BIG,),
        in_specs=[
            _tile(A_WIDTH), _tile(B_WIDTH), _tile(B_WIDTH), _GROUP_SPEC, _tile(D_MODEL), _MOD_SPEC,
            _full(1, B_WIDTH), _full(B_WIDTH, B_WIDTH), _full(1, B_WIDTH), _full(D_MODEL, D_MODEL),
            _full(SUB_TILE, SUB_TILE),
        ],
        out_specs=_tile(D_MODEL),
        out_shape=jax.ShapeDtypeStruct((BATCH * NTOK, D_MODEL), F32),
        compiler_params=_params("parallel"),
        name="even_out",
    )(ya, xs, gb, yg, xc, mod_p, d_skip, w_glu, b_glu, w_out, perm)


_HEAD_ORDER = np.concatenate([np.arange(0, 32), np.arange(64, 96), np.arange(32, 64), np.arange(96, 128)])


def _rope_tables():
    t = np.arange(SEQ)
    freqs = ROPE_THETA ** (-np.arange(ROPE_PAIRS, dtype=np.float64) / ROPE_PAIRS)
    ang = np.concatenate([(t // GRID_W)[:, None] * freqs, (t % GRID_W)[:, None] * freqs], axis=1)
    cos = np.concatenate([np.cos(ang), np.cos(ang)], axis=1)
    sin = np.concatenate([-np.sin(ang), np.sin(ang)], axis=1)
    pad = lambda a, v: np.concatenate([a, np.full((CTX_LEN, HEAD_DIM), v)], axis=0).astype(np.float32)
    return jnp.asarray(pad(cos, 1.0)), jnp.asarray(pad(sin, 0.0))


def _permute_heads(w_in, q_g, k_g):
    n_heads = N_Q + N_KV
    cols = (np.arange(n_heads)[:, None] * HEAD_DIM + _HEAD_ORDER[None, :]).reshape(-1)
    cols = np.concatenate([cols, np.arange(n_heads * HEAD_DIM, ODD_IN)])
    return w_in[:, cols], q_g[_HEAD_ORDER], k_g[_HEAD_ORDER]


def _odd_in_kernel(x_ref, mod_ref, g_ref, win_ref, qg_ref, kg_ref, *refs):
    tabs, (q_ref, k_ref, v_ref, gate_ref) = refs[:2 * PIECES], refs[2 * PIECES:]
    ones = jnp.ones((ROW_TILE, HEAD_DIM), BF16)
    for piece in range(PIECES):
        rows = slice(piece * ROW_TILE, (piece + 1) * ROW_TILE)
        h = _prologue(x_ref[rows, :], mod_ref[piece], g_ref[...])
        z = jnp.dot(h.astype(BF16), win_ref[...], preferred_element_type=F32)
        cos = tabs[2 * piece][...]
        sin = tabs[2 * piece + 1][...]

        def norm_rope(xh, gain, scale):
            xn = (xh * lax.rsqrt(jnp.mean(xh * xh, axis=-1, keepdims=True) + EPS)) * gain
            out = xn * cos + pltpu.roll(xn, HEAD_DIM // 2, 1) * sin
            return (out * scale).astype(BF16) if scale is not None else out.astype(BF16)

        for hq in range(N_Q):
            q_ref[rows, hq * HEAD_DIM:(hq + 1) * HEAD_DIM] = norm_rope(
                z[:, hq * HEAD_DIM:(hq + 1) * HEAD_DIM], qg_ref[...], Q_SCALE)
        for hk in range(N_KV):
            lo = C_WIDTH + hk * HEAD_DIM
            k_ref[rows, hk * HEAD_DIM:(hk + 1) * HEAD_DIM] = norm_rope(
                z[:, lo:lo + HEAD_DIM], kg_ref[...], None)
            lo = C_WIDTH + KV_WIDTH + hk * HEAD_DIM
            v_ref[rows, hk * V_EXT:hk * V_EXT + HEAD_DIM] = z[:, lo:lo + HEAD_DIM].astype(BF16)
            v_ref[rows, hk * V_EXT + HEAD_DIM:(hk + 1) * V_EXT] = ones
        gate_ref[rows, :] = z[:, C_WIDTH + 2 * KV_WIDTH:]


def _odd_in(xc, mod_p, norm_g, w_in, q_g, k_g, cos, sin):
    tabs, tab_specs = [], []
    for p in range(PIECES):
        spec = pl.BlockSpec((ROW_TILE, HEAD_DIM), lambda t, p=p: ((t * PIECES + p) % N_TILES, 0))
        tabs += [cos, sin]
        tab_specs += [spec, spec]
    return pl.pallas_call(
        _odd_in_kernel,
        grid=(N_BIG,),
        in_specs=[
            _tile(D_MODEL), _MOD_SPEC, _full(1, D_MODEL), _full(D_MODEL, ODD_IN),
            _full(1, HEAD_DIM), _full(1, HEAD_DIM), *tab_specs,
        ],
        out_specs=[_tile(C_WIDTH), _tile(KV_WIDTH), _tile(N_KV * V_EXT), _tile(C_WIDTH)],
        out_shape=[
            jax.ShapeDtypeStruct((BATCH * NTOK, C_WIDTH), BF16),
            jax.ShapeDtypeStruct((BATCH * NTOK, KV_WIDTH), BF16),
            jax.ShapeDtypeStruct((BATCH * NTOK, N_KV * V_EXT), BF16),
            jax.ShapeDtypeStruct((BATCH * NTOK, C_WIDTH), F32),
        ],
        compiler_params=_params("parallel"),
        name="odd_in",
    )(xc, mod_p, norm_g, w_in, q_g, k_g, *tabs)


_NT = (((1,), (1,)), ((), ()))


def _softmax_weights(q, k):
    s = lax.dot_general(q, k, _NT, preferred_element_type=F32)
    return jnp.exp2(s - jnp.max(s, axis=-1, keepdims=True)).astype(BF16)


def _weighted_values(p, v):
    ov = jnp.dot(p, v, preferred_element_type=F32)
    return (ov[:, :HEAD_DIM] / ov[:, HEAD_DIM:HEAD_DIM + 1]).astype(BF16)


def _attn_lat_kernel(q_ref, k_ref, v_ref, o_ref, p_ref):
    @pl.when(pl.program_id(0) == 0)
    def _():
        p_ref[...] = jnp.ones(p_ref.shape, BF16)

    k = k_ref[0]
    v = v_ref[0]
    for h in range(Q_PER_KV):
        cols = slice(h * HEAD_DIM, (h + 1) * HEAD_DIM)
        for r in range(ATTN_TILE // ATTN_UNIT):
            rows = slice(r * ATTN_UNIT, (r + 1) * ATTN_UNIT)
            p_new = _softmax_weights(q_ref[0, rows, cols], k)
            o_ref[0, rows, cols] = _weighted_values(p_ref[h, rows, :], v)
            p_ref[h, rows, :] = p_new


def _attention_latent(q, k, v):
    n_tiles = SEQ // ATTN_TILE
    n_units = BATCH * N_KV * n_tiles
    width = Q_PER_KV * HEAD_DIM

    def unit(u):
        return u // (N_KV * n_tiles), (u // n_tiles) % N_KV, u % n_tiles

    def q_map(t):
        b, g, i = unit(jnp.minimum(t, n_units - 1))
        return b, i, g

    def k_map(t):
        b, g, _ = unit(jnp.minimum(t, n_units - 1))
        return b, 0, g

    def v_map(t):
        b, g, _ = unit(jnp.maximum(t - 1, 0))
        return b, 0, g

    def o_map(t):
        b, g, i = unit(jnp.maximum(t - 1, 0))
        return b, i, g

    return pl.pallas_call(
        _attn_lat_kernel,
        grid=(n_units + 1,),
        in_specs=[
            pl.BlockSpec((1, ATTN_TILE, width), q_map),
            pl.BlockSpec((1, NTOK, HEAD_DIM), k_map),
            pl.BlockSpec((1, NTOK, V_EXT), v_map),
        ],
        out_specs=pl.BlockSpec((1, ATTN_TILE, width), o_map),
        out_shape=jax.ShapeDtypeStruct((BATCH, NTOK, C_WIDTH), BF16),
        scratch_shapes=[pltpu.VMEM((Q_PER_KV, ATTN_TILE, NTOK), BF16)],
        compiler_params=_params("arbitrary"),
        name="attention_latent",
    )(q, k, v)


def _attn_ctx_kernel(q_ref, k_ref, v_ref, _, o_ref):
    for h in range(N_Q):
        g = h // Q_PER_KV
        p = _softmax_weights(q_ref[0, :, h * HEAD_DIM:(h + 1) * HEAD_DIM],
                             k_ref[0, :, g * HEAD_DIM:(g + 1) * HEAD_DIM])
        o_ref[0, :, h * HEAD_DIM:(h + 1) * HEAD_DIM] = _weighted_values(
            p, v_ref[0, :, g * V_EXT:(g + 1) * V_EXT])


def _attention_context(q, k, v, o):
    spec = lambda w: pl.BlockSpec((1, CTX_LEN, w), lambda b: (b, CTX_TILE, 0))
    return pl.pallas_call(
        _attn_ctx_kernel,
        grid=(BATCH,),
        in_specs=[spec(C_WIDTH), spec(KV_WIDTH), spec(N_KV * V_EXT), pl.BlockSpec(memory_space=pl.ANY)],
        out_specs=spec(C_WIDTH),
        out_shape=jax.ShapeDtypeStruct(o.shape, o.dtype),
        input_output_aliases={3: 0},
        compiler_params=_params("parallel"),
        name="attention_context",
    )(q, k, v, o)


def _odd_out_kernel(o_ref, gate_ref, x_ref, mod_ref, wout_ref, fg_ref, out_ref, *, final):
    for piece in range(PIECES):
        rows = slice(piece * ROW_TILE, (piece + 1) * ROW_TILE)
        g = gate_ref[rows, :]
        a = (o_ref[rows, :].astype(F32) * (g * jax.nn.sigmoid(g))).astype(BF16)
        gt = mod_ref[piece % mod_ref.shape[0]][:, 2 * D_MODEL:]
        x = x_ref[rows, :] + gt * jnp.dot(a, wout_ref[...], preferred_element_type=F32)
        if final:
            x = (x * lax.rsqrt(jnp.mean(x * x, axis=-1, keepdims=True) + EPS)) * fg_ref[...]
        out_ref[rows, :] = x


def _odd_out(o, gate, xc, mod_p, w_out, final_g):
    return pl.pallas_call(
        functools.partial(_odd_out_kernel, final=False),
        grid=(N_BIG,),
        in_specs=[_tile(C_WIDTH), _tile(C_WIDTH), _tile(D_MODEL), _MOD_SPEC,
                  _full(D_MODEL, D_MODEL), _full(1, D_MODEL)],
        out_specs=_tile(D_MODEL),
        out_shape=jax.ShapeDtypeStruct((BATCH * NTOK, D_MODEL), F32),
        compiler_params=_params("parallel"),
        name="odd_out",
    )(o, gate, xc, mod_p, w_out, final_g)


def _odd_out_final(o, gate, xc, mod_l, w_out, final_g):
    tile = lambda w: pl.BlockSpec((None, BIG_TILE, w), lambda b, j: (b, j, 0))
    full = lambda *s: pl.BlockSpec(s, lambda b, j: (0,) * len(s))
    per_sample = lambda a, w: a.reshape(BATCH, NTOK, w)
    return pl.pallas_call(
        functools.partial(_odd_out_kernel, final=True),
        grid=(BATCH, SEQ // BIG_TILE),
        in_specs=[tile(C_WIDTH), tile(C_WIDTH), tile(D_MODEL),
                  pl.BlockSpec((1, 1, 3 * D_MODEL), lambda b, j: (b, 0, 0)),
                  full(D_MODEL, D_MODEL), full(1, D_MODEL)],
        out_specs=tile(D_MODEL),
        out_shape=jax.ShapeDtypeStruct((BATCH, SEQ, D_MODEL), F32),
        compiler_params=_params("parallel", "parallel"),
        name="odd_out_final",
    )(per_sample(o, C_WIDTH), per_sample(gate, C_WIDTH), per_sample(xc, D_MODEL), mod_l, w_out, final_g)


def kernel(x, c, ctx, c_ctx, norm_g, w_mod, b_mod, we_in, we_out, gm_v_g, gm_w_s, gm_b_s,
           s5_lam_re, s5_lam_im, s5_log_dt, s5_b_re, s5_b_im, s5_c_re, s5_c_im, s5_d,
           s5_w_glu, s5_b_glu, wo_in, wo_out, q_norm_g, k_norm_g, final_g):
    cond = jnp.concatenate([c, c_ctx[None], jnp.zeros((MOD_ROWS - BATCH - 1, D_MODEL), F32)], axis=0)
    mods = _modulation(cond, w_mod, b_mod).reshape(DEPTH, MOD_ROWS, 1, 3 * D_MODEL)
    xc = jnp.concatenate([x, ctx], axis=1).reshape(BATCH * NTOK, D_MODEL)
    piece_rows = np.where(np.arange(N_TILES)[None, :] == CTX_TILE, CTX_ROW, np.arange(BATCH)[:, None]).reshape(-1)
    cos, sin = _rope_tables()
    perm = _chunk_perm()
    row = lambda a: a.reshape(1, -1)
    per_sample = lambda a: a.reshape(BATCH, NTOK, a.shape[-1])

    for layer in range(DEPTH):
        i = layer // 2
        mod_p = mods[layer][piece_rows]
        if layer % 2 == 0:
            b_full = jnp.repeat(gm_b_s[i].T, A_GROUP_W, axis=1)
            ya, xs, gb, xg = _even_in(xc, mod_p, row(norm_g[layer]), we_in[i].astype(BF16),
                                      row(gm_v_g[i]), gm_w_s[i].astype(BF16), b_full, perm)
            prow, pmat = _s5_params(s5_lam_re[i], s5_lam_im[i], s5_log_dt[i], s5_b_re[i],
                                    s5_b_im[i], s5_c_re[i], s5_c_im[i])
            yg = _s5(xg, prow, pmat)
            xc = _even_out(ya, xs, gb, yg, xc, mod_p, row(s5_d[i]), s5_w_glu[i].astype(BF16),
                           row(s5_b_glu[i]), we_out[i].astype(BF16), perm.T)
        else:
            w_in, q_g, k_g = _permute_heads(wo_in[i], q_norm_g[i], k_norm_g[i])
            q, k, v, gate = _odd_in(xc, mod_p, row(norm_g[layer]), w_in.astype(BF16),
                                    row(q_g), row(k_g), cos, sin)
            q, k, v = per_sample(q), per_sample(k), per_sample(v)
            o = _attention_latent(q, k, v)
            if layer == DEPTH - 1:
                return _odd_out_final(o, gate, xc, mods[layer], wo_out[i].astype(BF16), row(final_g))
            o = _attention_context(q, k, v, o).reshape(BATCH * NTOK, C_WIDTH)
            xc = _odd_out(o, gate, xc, mod_p, wo_out[i].astype(BF16), row(final_g))
```

```python
import functools
import math

import jax
import jax.numpy as jnp
import numpy as np
from jax import lax
from jax.experimental import pallas as pl
from jax.experimental.pallas import tpu as pltpu

F32 = jnp.float32
BF16 = jnp.bfloat16
HIGHEST = lax.Precision.HIGHEST

D_MODEL = 1024
BATCH = 4
SEQ = 4096
DEPTH = 4
GRID_W = 64
CTX_LEN = 256
EPS = 1e-6
NTOK = CTX_LEN + SEQ

LANES = 128
SUBLANES = 8
ROW_TILE = 256
N_TILES = NTOK // ROW_TILE
CTX_TILE = N_TILES - 1
LAT_TILES = N_TILES - 1
BIG_TILE = 1024
PIECES = BIG_TILE // ROW_TILE
N_BIG = BATCH * NTOK // BIG_TILE
assert N_BIG * BIG_TILE == BATCH * NTOK and SEQ % BIG_TILE == 0
VMEM_LIMIT = 56 * 1024 * 1024

CHUNK = 128
A_WIDTH = D_MODEL // 2
A_GROUPS = 4
A_GROUP_W = A_WIDTH // A_GROUPS
B_WIDTH = D_MODEL // 2
B_CH = 16
B_GROUPS = B_WIDTH // B_CH
B_STATE = 64
EVEN_IN = 3 * A_WIDTH + 2 * B_WIDTH
HEAD_DIM = 128
N_Q = D_MODEL // HEAD_DIM
N_KV = 2
Q_PER_KV = N_Q // N_KV
C_WIDTH = N_Q * HEAD_DIM
KV_WIDTH = N_KV * HEAD_DIM
ODD_IN = 2 * C_WIDTH + 2 * KV_WIDTH
ROPE_THETA = 10000.0
ROPE_PAIRS = HEAD_DIM // 4
Q_SCALE = HEAD_DIM ** -0.5 * math.log2(math.e)
V_EXT = 2 * HEAD_DIM
ATTN_TILE = 512
ATTN_UNIT = 128

S5_Q = 16
S5_K = S5_Q * B_CH
N_CHUNKS = NTOK // S5_Q
S5_ROWS = BATCH * N_CHUNKS
TILE_CHUNKS = ROW_TILE // S5_Q
SUB_TILE = ROW_TILE // 2
SUB_CHUNKS = SUB_TILE // S5_Q
assert SUB_TILE == CHUNK
ODD_SUB = ROW_TILE
BLOCKS_PER_SAMPLE = N_CHUNKS // SUBLANES
CTX_BLOCKS = CTX_LEN // S5_Q // SUBLANES
N_PAIRS = B_GROUPS // 2
SCAN_UNROLL = 2
GROUPS_PER_VREG = LANES // B_CH
N_LAGS = 2 * S5_Q - 1

MOD_ROWS = 8
CTX_ROW = BATCH


def _params(*sem):
    return pltpu.CompilerParams(dimension_semantics=sem, vmem_limit_bytes=VMEM_LIMIT)


def _mod_row(b, i):
    return jnp.where(i == CTX_TILE, CTX_ROW, b)


def _lane_block(rows):
    return lax.broadcasted_iota(jnp.int32, (rows, LANES), 1) // B_CH


def _move_blocks(pieces, blk):
    acc = None
    for dst, (src, src_blk) in enumerate(pieces):
        sh = ((dst - src_blk) % GROUPS_PER_VREG) * B_CH
        rolled = pltpu.roll(src, sh, 1) if sh else src
        acc = rolled if acc is None else jnp.where(blk == dst, rolled, acc)
    return acc


def _merge_blocks(pick, blk):
    acc = pick(0)
    for k in range(1, GROUPS_PER_VREG):
        acc = jnp.where(blk == k, pick(k), acc)
    return acc


def _block_transpose(src, blk):
    n = GROUPS_PER_VREG
    rolled = []
    for r in range(n):
        m = _merge_blocks(lambda b: src[(b + r) % n], blk)
        rolled.append(pltpu.roll(m, r * B_CH, 1) if r else m)
    return [_merge_blocks(lambda k: rolled[(k - b) % n], blk) for b in range(n)]


def _mod_kernel(cond_ref, w_ref, b_ref, o_ref):
    c = cond_ref[...]
    s = c * jax.nn.sigmoid(c)
    o_ref[0] = jnp.dot(s, w_ref[0], preferred_element_type=F32, precision=HIGHEST) + b_ref[0]


def _modulation(cond, w_mod, b_mod):
    nblk = 3 * D_MODEL // D_MODEL
    return pl.pallas_call(
        _mod_kernel,
        grid=(DEPTH, nblk),
        in_specs=[
            pl.BlockSpec((MOD_ROWS, D_MODEL), lambda l, j: (0, 0)),
            pl.BlockSpec((1, D_MODEL, D_MODEL), lambda l, j: (l, 0, j)),
            pl.BlockSpec((1, 1, D_MODEL), lambda l, j: (l, 0, j)),
        ],
        out_specs=pl.BlockSpec((1, MOD_ROWS, D_MODEL), lambda l, j: (l, 0, j)),
        out_shape=jax.ShapeDtypeStruct((DEPTH, MOD_ROWS, 3 * D_MODEL), F32),
        compiler_params=_params("parallel", "parallel"),
        name="modulation",
    )(cond, w_mod, b_mod.reshape(DEPTH, 1, 3 * D_MODEL))


def _prologue(x, mod, g):
    sh = mod[:, :D_MODEL]
    sc = mod[:, D_MODEL:2 * D_MODEL]
    y = x * lax.rsqrt(jnp.mean(x * x, axis=-1, keepdims=True) + EPS)
    return (y * g) * (1 + sc) + sh


def _chunk_perm():
    r = jnp.arange(SUB_TILE)
    source = (r % SUB_CHUNKS) * S5_Q + r // SUB_CHUNKS
    return (source[:, None] == r[None, :]).astype(BF16)


def _even_in_kernel(x_ref, mod_ref, g_ref, win_ref, vg_ref, ws_ref, bs_ref, perm_ref,
                    ya_ref, xs_ref, gb_ref, xg_ref):
    blk = _lane_block(SUB_CHUNKS)
    xg_parts = []
    for sub in range(BIG_TILE // SUB_TILE):
        rows = slice(sub * SUB_TILE, (sub + 1) * SUB_TILE)
        h = _prologue(x_ref[rows, :], mod_ref[sub * SUB_TILE // ROW_TILE], g_ref[...])
        z = jnp.dot(h.astype(BF16), win_ref[...], preferred_element_type=F32)
        u = z[:, 0:A_WIDTH]
        v = z[:, A_WIDTH:2 * A_WIDTH]
        ga = z[:, 2 * A_WIDTH:3 * A_WIDTH]
        xs = z[:, 3 * A_WIDTH:3 * A_WIDTH + B_WIDTH]
        xs_ref[rows, :] = xs
        gb_ref[rows, :] = z[:, 3 * A_WIDTH + B_WIDTH:].astype(BF16)
        mu = jnp.mean(v, axis=-1, keepdims=True)
        vc = v - mu
        var = jnp.mean(vc * vc, axis=-1, keepdims=True)
        vn = ((vc * lax.rsqrt(var + EPS)) * vg_ref[...]).astype(BF16)
        mixed = jnp.concatenate(
            [jnp.dot(ws_ref[g], vn[:, g * A_GROUP_W:(g + 1) * A_GROUP_W], preferred_element_type=F32)
             for g in range(A_GROUPS)], axis=1) + bs_ref[...]
        ya_ref[rows, :] = ((u * mixed) * (ga * jax.nn.sigmoid(ga))).astype(BF16)

        r = jnp.dot(perm_ref[...], xs.astype(BF16), preferred_element_type=F32)
        parts = [[None] * (S5_K // LANES) for _ in range(B_GROUPS)]
        for col in range(B_WIDTH // LANES):
            for j in range(S5_K // LANES):
                steps = [r[(j * GROUPS_PER_VREG + k) * SUB_CHUNKS:(j * GROUPS_PER_VREG + k + 1) * SUB_CHUNKS,
                           col * LANES:(col + 1) * LANES] for k in range(GROUPS_PER_VREG)]
                for b, a in enumerate(_block_transpose(steps, blk)):
                    parts[col * GROUPS_PER_VREG + b][j] = a
        xg_parts.append([jnp.concatenate(p, axis=1) for p in parts])
        if len(xg_parts) == 2:
            first = (sub - 1) * SUB_CHUNKS
            for g in range(B_GROUPS):
                xg_ref[g, first:first + 2 * SUB_CHUNKS, :] = jnp.concatenate(
                    [xg_parts[0][g], xg_parts[1][g]], axis=0).astype(BF16)
            xg_parts = []


def _tile(w):
    return pl.BlockSpec((BIG_TILE, w), lambda t: (t, 0))


def _full(*shape):
    return pl.BlockSpec(shape, lambda t: (0,) * len(shape))


_MOD_SPEC = pl.BlockSpec((PIECES, 1, 3 * D_MODEL), lambda t: (t, 0, 0))
_GROUP_SPEC = pl.BlockSpec((B_GROUPS, BIG_TILE // S5_Q, S5_K), lambda t: (0, t, 0))


def _even_in(xc, mod_p, norm_g, w_in, v_g, w_s, b_full, perm):
    return pl.pallas_call(
        _even_in_kernel,
        grid=(N_BIG,),
        in_specs=[
            _tile(D_MODEL), _MOD_SPEC, _full(1, D_MODEL), _full(D_MODEL, EVEN_IN), _full(1, A_WIDTH),
            _full(A_GROUPS, CHUNK, CHUNK), _full(CHUNK, A_WIDTH), _full(SUB_TILE, SUB_TILE),
        ],
        out_specs=[_tile(A_WIDTH), _tile(B_WIDTH), _tile(B_WIDTH), _GROUP_SPEC],
        out_shape=[
            jax.ShapeDtypeStruct((BATCH * NTOK, A_WIDTH), BF16),
            jax.ShapeDtypeStruct((BATCH * NTOK, B_WIDTH), F32),
            jax.ShapeDtypeStruct((BATCH * NTOK, B_WIDTH), BF16),
            jax.ShapeDtypeStruct((B_GROUPS, S5_ROWS, S5_K), BF16),
        ],
        compiler_params=_params("parallel"),
        name="even_in",
    )(xc, mod_p, norm_g, w_in, v_g, w_s, b_full, perm)


def _s5_kernel(xg_ref, prow_ref, pmat_ref, yg_ref, sl_ref, bst_ref, toep_ref, cst_ref, lhs_ref):
    lane = lax.broadcasted_iota(jnp.int32, (S5_Q, LANES), 1)
    in_group = (lane < B_STATE, lane >= B_STATE)
    blk = lane // B_CH
    zero = jnp.zeros((S5_Q, LANES), F32)

    def cmul(ar, ai, br, bi):
        return ar * br - ai * bi, ar * bi + ai * br

    w_in, c_out, scan_consts, c_rows = [], [], [], []
    for d in range(2):
        lam_re = prow_ref[0, d, 0:1, :]
        lam_im = prow_ref[0, d, 1:2, :]
        dt = jnp.exp(prow_ref[0, d, 2:3, :])
        z_re, z_im = lam_re * dt, lam_im * dt
        mag = jnp.exp(z_re)
        lb_re, lb_im = mag * jnp.cos(z_im), mag * jnp.sin(z_im)
        den = lam_re * lam_re + lam_im * lam_im
        n_re, n_im = lb_re - 1.0, lb_im
        f_re = (n_re * lam_re + n_im * lam_im) / den
        f_im = (n_im * lam_re - n_re * lam_im) / den
        b_re, b_im = pmat_ref[0, d, 0], pmat_ref[0, d, 1]
        c_re, c_im = pmat_ref[0, d, 2], pmat_ref[0, d, 3]
        bb_re, bb_im = cmul(f_re, f_im, b_re, b_im)
        c_rows.append((c_re, c_im))

        def powers(steps):
            m = jnp.exp(steps * z_re)
            return m * jnp.cos(steps * z_im), m * jnp.sin(steps * z_im)

        j = lax.broadcasted_iota(jnp.int32, (3 * SUBLANES, LANES), 0).astype(F32)
        p_re, p_im = powers(j)
        row = lambda a, k: a[k:k + 1, :]
        w_in.append([cmul(bb_re, bb_im, row(p_re, k), row(p_im, k)) for k in range(S5_Q)])
        c_out.append([cmul(c_re, c_im, row(p_re, k), row(p_im, k)) for k in range(S5_Q + 1)])

        i8 = lax.broadcasted_iota(jnp.int32, (SUBLANES, LANES), 0)
        order = i8 if d == 0 else (SUBLANES - 1) - i8
        apow = powers((order * S5_Q).astype(F32))
        bc = lambda r: powers(jnp.full((SUBLANES, LANES), r * S5_Q, F32))
        scan_consts.append((apow, [bc(1), bc(2), bc(4)], bc(SUBLANES)))

    for e in range(2):
        sel = lambda a: jnp.where(in_group[e], a, 0.0)
        for s in range(S5_Q):
            fr, fi = w_in[0][S5_Q - 1 - s]
            br, bi = w_in[1][s]
            r0 = e * S5_K + s * B_CH
            bst_ref[r0:r0 + B_CH, :] = jnp.concatenate(
                [sel(fr), sel(fi), sel(br), sel(bi)], axis=1).astype(BF16)

    for e in range(2):
        sel = lambda a: jnp.where(in_group[e], a, 0.0)
        for t in range(S5_Q):
            fr, fi = c_out[0][t + 1]
            br, bi = c_out[1][S5_Q - t]
            cst_ref[e, t * B_CH:(t + 1) * B_CH, :] = jnp.concatenate(
                [sel(fr), sel(-fi), sel(br), sel(-bi)], axis=1).astype(BF16)

    for a in range(N_LAGS + 1):
        lag = a - (S5_Q - 1)
        f = w_in[0][lag] if 0 <= lag < S5_Q else (zero, zero)
        b = w_in[1][-lag] if -S5_Q < lag <= 0 else (zero, zero)
        lhs_ref[a * B_CH:(a + 1) * B_CH, :] = jnp.concatenate([f[0], f[1], b[0], b[1]], axis=1)
    cq_rows = []
    for e in range(2):
        sel = lambda a: jnp.where(in_group[e], a, 0.0)
        (fr, fi), (br, bi) = c_rows
        cq_rows.append(jnp.concatenate([sel(fr), sel(-fi), sel(br), sel(-bi)], axis=1))
    cq = jnp.concatenate(cq_rows + [jnp.zeros((LANES - 2 * B_CH, 4 * LANES), F32)], axis=0)
    kst = lax.dot_general(lhs_ref[...], cq, (((1,), (1,)), ((), ())),
                          preferred_element_type=F32, precision=HIGHEST)
    for e in range(2):
        g_cols = []
        for col in range((N_LAGS + 1) // GROUPS_PER_VREG):
            pieces = [(kst[(col * GROUPS_PER_VREG + k) * B_CH:(col * GROUPS_PER_VREG + k + 1) * B_CH, :], e)
                      for k in range(GROUPS_PER_VREG)]
            g_cols.append(_move_blocks(pieces, blk))
        for s in range(S5_Q):
            c0, sh = divmod((S5_Q - 1 - s) * B_CH, LANES)
            if sh == 0:
                cols = g_cols[c0:c0 + 2]
            else:
                rolled = [pltpu.roll(g_cols[c0 + k], LANES - sh, 1) for k in range(3)]
                cols = [jnp.where(lane < LANES - sh, rolled[k], rolled[k + 1]) for k in range(2)]
            toep_ref[e, s * B_CH:(s + 1) * B_CH, :] = jnp.concatenate(cols, axis=1).astype(BF16)

    x0 = xg_ref[0]
    x1 = xg_ref[1]
    sl_ref[...] = jnp.dot(jnp.concatenate([x0, x1], axis=1), bst_ref[...], preferred_element_type=F32)

    i8 = lax.broadcasted_iota(jnp.int32, (SUBLANES, LANES), 0)

    def scan_block(p, carry, consts, fwd):
        (ap_re, ap_im), doubling, (a8_re, a8_im) = consts
        p_re, p_im = p

        def shifted(a, r):
            if fwd:
                return jnp.where(i8 >= r, pltpu.roll(a, r, 0), 0.0)
            return jnp.where(i8 < SUBLANES - r, pltpu.roll(a, SUBLANES - r, 0), 0.0)

        for r, (a_re, a_im) in zip((1, 2, 4), doubling):
            s_re, s_im = shifted(p_re, r), shifted(p_im, r)
            m_re, m_im = cmul(a_re, a_im, s_re, s_im)
            p_re, p_im = p_re + m_re, p_im + m_im
        c_re, c_im = carry
        e_re, e_im = cmul(ap_re, ap_im, c_re, c_im)
        entering = (shifted(p_re, 1) + e_re, shifted(p_im, 1) + e_im)
        last = SUBLANES - 1 if fwd else 0
        n_re, n_im = cmul(a8_re, a8_im, c_re, c_im)
        bcast = lambda a: jnp.broadcast_to(a[last:last + 1, :], (SUBLANES, LANES))
        return entering, (bcast(p_re) + n_re, bcast(p_im) + n_im)

    def step(j, carry):
        jf = jnp.where(j < CTX_BLOCKS, BLOCKS_PER_SAMPLE - CTX_BLOCKS + j, j - CTX_BLOCKS)
        jb = BLOCKS_PER_SAMPLE - 1 - j
        work = []
        for b in range(BATCH):
            for d, jd in enumerate((jf, jb)):
                row0 = pl.multiple_of((b * BLOCKS_PER_SAMPLE + jd) * SUBLANES, SUBLANES)
                rows, col = pl.ds(row0, SUBLANES), d * 2 * LANES
                p = (sl_ref[rows, col:col + LANES], sl_ref[rows, col + LANES:col + 2 * LANES])
                work.append((rows, col, p, carry[b][d], d))
        done = [(rows, col, scan_block(p, c, scan_consts[d], d == 0)) for rows, col, p, c, d in work]
        out = []
        for n, (rows, col, (entering, nxt)) in enumerate(done):
            sl_ref[rows, col:col + LANES] = entering[0]
            sl_ref[rows, col + LANES:col + 2 * LANES] = entering[1]
            out.append(nxt)
        return tuple((out[2 * b], out[2 * b + 1]) for b in range(BATCH))

    z8 = jnp.zeros((SUBLANES, LANES), F32)
    lax.fori_loop(0, BLOCKS_PER_SAMPLE, step, tuple(((z8, z8), (z8, z8)) for _ in range(BATCH)),
                  unroll=SCAN_UNROLL)

    sp = sl_ref[...].astype(BF16)
    nt = (((1,), (1,)), ((), ()))
    yg_ref[0] = (jnp.dot(x0, toep_ref[0], preferred_element_type=F32)
                 + lax.dot_general(sp, cst_ref[0], nt, preferred_element_type=F32)).astype(BF16)
    yg_ref[1] = (jnp.dot(x1, toep_ref[1], preferred_element_type=F32)
                 + lax.dot_general(sp, cst_ref[1], nt, preferred_element_type=F32)).astype(BF16)


def _s5_params(lam_re, lam_im, log_dt, b_re, b_im, c_re, c_im):
    def rows(a):
        return a.astype(F32).reshape(2, N_PAIRS, 1, 2 * B_STATE).transpose(1, 0, 2, 3)

    dt = jnp.broadcast_to(log_dt[..., None], lam_re.shape)
    pad = jnp.zeros((N_PAIRS, 2, SUBLANES - 3, LANES), F32)
    prow = jnp.concatenate([rows(lam_re), rows(lam_im), rows(dt), pad], axis=2)

    def mats(a, channel_axis):
        a = a.astype(F32)
        if channel_axis == 3:
            a = a.transpose(0, 1, 3, 2)
        a = a.reshape(2, N_PAIRS, 2, B_CH, B_STATE).transpose(1, 0, 3, 2, 4)
        return a.reshape(N_PAIRS, 2, B_CH, 2 * B_STATE)

    pmat = jnp.stack([mats(b_re, 3), mats(b_im, 3), mats(c_re, 2), mats(c_im, 2)], axis=2)
    return prow, pmat


def _s5(xg, prow, pmat):
    return pl.pallas_call(
        _s5_kernel,
        grid=(N_PAIRS,),
        in_specs=[
            pl.BlockSpec((2, S5_ROWS, S5_K), lambda q: (q, 0, 0)),
            pl.BlockSpec((1, 2, SUBLANES, LANES), lambda q: (q, 0, 0, 0)),
            pl.BlockSpec((1, 2, 4, B_CH, LANES), lambda q: (q, 0, 0, 0, 0)),
        ],
        out_specs=pl.BlockSpec((2, S5_ROWS, S5_K), lambda q: (q, 0, 0)),
        out_shape=jax.ShapeDtypeStruct((B_GROUPS, S5_ROWS, S5_K), BF16),
        scratch_shapes=[
            pltpu.VMEM((S5_ROWS, 4 * LANES), F32),
            pltpu.VMEM((2 * S5_K, 4 * LANES), BF16),
            pltpu.VMEM((2, S5_K, S5_K), BF16),
            pltpu.VMEM((2, S5_K, 4 * LANES), BF16),
            pltpu.VMEM(((N_LAGS + 1) * B_CH, 4 * LANES), F32),
        ],
        compiler_params=_params("parallel"),
        name="s5_chunked",
    )(xg, prow, pmat)


def _even_out_kernel(ya_ref, xs_ref, gb_ref, yg_ref, x_ref, mod_ref, d_ref, wglu_ref, bglu_ref,
                     wout_ref, perm_ref, o_ref):
    blk = _lane_block(SUB_CHUNKS)
    for sub in range(BIG_TILE // SUB_TILE):
        rows = slice(sub * SUB_TILE, (sub + 1) * SUB_TILE)
        chunks = slice(sub * SUB_CHUNKS, (sub + 1) * SUB_CHUNKS)
        gt = mod_ref[sub * SUB_TILE // ROW_TILE][:, 2 * D_MODEL:]
        steps = [[None] * (B_WIDTH // LANES) for _ in range(S5_Q)]
        for col in range(B_WIDTH // LANES):
            for j in range(S5_K // LANES):
                groups = [yg_ref[col * GROUPS_PER_VREG + k, chunks, j * LANES:(j + 1) * LANES].astype(F32)
                          for k in range(GROUPS_PER_VREG)]
                for b, a in enumerate(_block_transpose(groups, blk)):
                    steps[j * GROUPS_PER_VREG + b][col] = a
        ys = jnp.dot(perm_ref[...],
                     jnp.concatenate([jnp.concatenate(s, axis=1) for s in steps], axis=0).astype(BF16),
                     preferred_element_type=F32)

        y = ys + d_ref[...] * xs_ref[rows, :]
        y = jax.nn.gelu(y)
        t = jnp.dot(y.astype(BF16), wglu_ref[...], preferred_element_type=F32) + bglu_ref[...]
        y = y * jax.nn.sigmoid(t)
        gb = gb_ref[rows, :].astype(F32)
        yb = (y * (gb * jax.nn.sigmoid(gb))).astype(BF16)
        mix = (jnp.dot(ya_ref[rows, :], wout_ref[0:A_WIDTH, :], preferred_element_type=F32)
               + jnp.dot(yb, wout_ref[A_WIDTH:, :], preferred_element_type=F32))
        o_ref[rows, :] = x_ref[rows, :] + gt * mix


def _even_out(ya, xs, gb, yg, xc, mod_p, d_skip, w_glu, b_glu, w_out, perm):
    return pl.pallas_call(
        _even_out_kernel,
        grid=(N_BIG,),
        in_specs=[
            _tile(A_WIDTH), _tile(B_WIDTH), _tile(B_WIDTH), _GROUP_SPEC, _tile(D_MODEL), _MOD_SPEC,
            _full(1, B_WIDTH), _full(B_WIDTH, B_WIDTH), _full(1, B_WIDTH), _full(D_MODEL, D_MODEL),
            _full(SUB_TILE, SUB_TILE),
        ],
        out_specs=_tile(D_MODEL),
        out_shape=jax.ShapeDtypeStruct((BATCH * NTOK, D_MODEL), F32),
        compiler_params=_params("parallel"),
        name="even_out",
    )(ya, xs, gb, yg, xc, mod_p, d_skip, w_glu, b_glu, w_out, perm)


_HEAD_ORDER = np.concatenate([np.arange(0, 32), np.arange(64, 96), np.arange(32, 64), np.arange(96, 128)])


def _rope_tables():
    t = np.arange(SEQ)
    freqs = ROPE_THETA ** (-np.arange(ROPE_PAIRS, dtype=np.float64) / ROPE_PAIRS)
    ang = np.concatenate([(t // GRID_W)[:, None] * freqs, (t % GRID_W)[:, None] * freqs], axis=1)
    cos = np.concatenate([np.cos(ang), np.cos(ang)], axis=1)
    sin = np.concatenate([-np.sin(ang), np.sin(ang)], axis=1)
    pad = lambda a, v: np.concatenate([a, np.full((CTX_LEN, HEAD_DIM), v)], axis=0).astype(np.float32)
    return jnp.asarray(pad(cos, 1.0)), jnp.asarray(pad(sin, 0.0))


def _permute_heads(w_in, q_g, k_g):
    n_heads = N_Q + N_KV
    cols = (np.arange(n_heads)[:, None] * HEAD_DIM + _HEAD_ORDER[None, :]).reshape(-1)
    cols = np.concatenate([cols, np.arange(n_heads * HEAD_DIM, ODD_IN)])
    return w_in[:, cols], q_g[_HEAD_ORDER], k_g[_HEAD_ORDER]


def _odd_in_kernel(x_ref, mod_ref, g_ref, win_ref, qg_ref, kg_ref, *refs):
    tabs, (q_ref, k_ref, v_ref, gate_ref) = refs[:2 * PIECES], refs[2 * PIECES:]
    ones = jnp.ones((ROW_TILE, HEAD_DIM), BF16)
    for piece in range(PIECES):
        rows = slice(piece * ROW_TILE, (piece + 1) * ROW_TILE)
        h = _prologue(x_ref[rows, :], mod_ref[piece], g_ref[...])
        z = jnp.dot(h.astype(BF16), win_ref[...], preferred_element_type=F32)
        cos = tabs[2 * piece][...]
        sin = tabs[2 * piece + 1][...]

        def norm_rope(xh, gain, scale):
            xn = (xh * lax.rsqrt(jnp.mean(xh * xh, axis=-1, keepdims=True) + EPS)) * gain
            out = xn * cos + pltpu.roll(xn, HEAD_DIM // 2, 1) * sin
            return (out * scale).astype(BF16) if scale is not None else out.astype(BF16)

        for hq in range(N_Q):
            q_ref[rows, hq * HEAD_DIM:(hq + 1) * HEAD_DIM] = norm_rope(
                z[:, hq * HEAD_DIM:(hq + 1) * HEAD_DIM], qg_ref[...], Q_SCALE)
        for hk in range(N_KV):
            lo = C_WIDTH + hk * HEAD_DIM
            k_ref[rows, hk * HEAD_DIM:(hk + 1) * HEAD_DIM] = norm_rope(
                z[:, lo:lo + HEAD_DIM], kg_ref[...], None)
            lo = C_WIDTH + KV_WIDTH + hk * HEAD_DIM
            v_ref[rows, hk * V_EXT:hk * V_EXT + HEAD_DIM] = z[:, lo:lo + HEAD_DIM].astype(BF16)
            v_ref[rows, hk * V_EXT + HEAD_DIM:(hk + 1) * V_EXT] = ones
        gate_ref[rows, :] = z[:, C_WIDTH + 2 * KV_WIDTH:].astype(BF16)


def _odd_in(xc, mod_p, norm_g, w_in, q_g, k_g, cos, sin):
    tabs, tab_specs = [], []
    for p in range(PIECES):
        spec = pl.BlockSpec((ROW_TILE, HEAD_DIM), lambda t, p=p: ((t * PIECES + p) % N_TILES, 0))
        tabs += [cos, sin]
        tab_specs += [spec, spec]
    return pl.pallas_call(
        _odd_in_kernel,
        grid=(N_BIG,),
        in_specs=[
            _tile(D_MODEL), _MOD_SPEC, _full(1, D_MODEL), _full(D_MODEL, ODD_IN),
            _full(1, HEAD_DIM), _full(1, HEAD_DIM), *tab_specs,
        ],
        out_specs=[_tile(C_WIDTH), _tile(KV_WIDTH), _tile(N_KV * V_EXT), _tile(C_WIDTH)],
        out_shape=[
            jax.ShapeDtypeStruct((BATCH * NTOK, C_WIDTH), BF16),
            jax.ShapeDtypeStruct((BATCH * NTOK, KV_WIDTH), BF16),
            jax.ShapeDtypeStruct((BATCH * NTOK, N_KV * V_EXT), BF16),
            jax.ShapeDtypeStruct((BATCH * NTOK, C_WIDTH), BF16),
        ],
        compiler_params=_params("parallel"),
        name="odd_in",
    )(xc, mod_p, norm_g, w_in, q_g, k_g, *tabs)


_NT = (((1,), (1,)), ((), ()))


def _softmax_weights(q, k):
    s = lax.dot_general(q, k, _NT, preferred_element_type=F32)
    return jnp.exp2(s - jnp.max(s, axis=-1, keepdims=True)).astype(BF16)


def _weighted_values(p, v):
    ov = jnp.dot(p, v, preferred_element_type=F32)
    return (ov[:, :HEAD_DIM] / ov[:, HEAD_DIM:HEAD_DIM + 1]).astype(BF16)


def _attn_lat_kernel(q_ref, k_ref, v_ref, o_ref, p_ref):
    @pl.when(pl.program_id(0) == 0)
    def _():
        p_ref[...] = jnp.ones(p_ref.shape, BF16)

    k = k_ref[0]
    v = v_ref[0]
    for h in range(Q_PER_KV):
        cols = slice(h * HEAD_DIM, (h + 1) * HEAD_DIM)
        for r in range(ATTN_TILE // ATTN_UNIT):
            rows = slice(r * ATTN_UNIT, (r + 1) * ATTN_UNIT)
            p_new = _softmax_weights(q_ref[0, rows, cols], k)
            o_ref[0, rows, cols] = _weighted_values(p_ref[h, rows, :], v)
            p_ref[h, rows, :] = p_new


def _attention_latent(q, k, v):
    n_tiles = SEQ // ATTN_TILE
    n_units = BATCH * N_KV * n_tiles
    width = Q_PER_KV * HEAD_DIM

    def unit(u):
        return u // (N_KV * n_tiles), (u // n_tiles) % N_KV, u % n_tiles

    def q_map(t):
        b, g, i = unit(jnp.minimum(t, n_units - 1))
        return b, i, g

    def k_map(t):
        b, g, _ = unit(jnp.minimum(t, n_units - 1))
        return b, 0, g

    def v_map(t):
        b, g, _ = unit(jnp.maximum(t - 1, 0))
        return b, 0, g

    def o_map(t):
        b, g, i = unit(jnp.maximum(t - 1, 0))
        return b, i, g

    return pl.pallas_call(
        _attn_lat_kernel,
        grid=(n_units + 1,),
        in_specs=[
            pl.BlockSpec((1, ATTN_TILE, width), q_map),
            pl.BlockSpec((1, NTOK, HEAD_DIM), k_map),
            pl.BlockSpec((1, NTOK, V_EXT), v_map),
        ],
        out_specs=pl.BlockSpec((1, ATTN_TILE, width), o_map),
        out_shape=jax.ShapeDtypeStruct((BATCH, NTOK, C_WIDTH), BF16),
        scratch_shapes=[pltpu.VMEM((Q_PER_KV, ATTN_TILE, NTOK), BF16)],
        compiler_params=_params("arbitrary"),
        name="attention_latent",
    )(q, k, v)


def _attn_ctx_kernel(q_ref, k_ref, v_ref, _, o_ref):
    for h in range(N_Q):
        g = h // Q_PER_KV
        p = _softmax_weights(q_ref[0, :, h * HEAD_DIM:(h + 1) * HEAD_DIM],
                             k_ref[0, :, g * HEAD_DIM:(g + 1) * HEAD_DIM])
        o_ref[0, :, h * HEAD_DIM:(h + 1) * HEAD_DIM] = _weighted_values(
            p, v_ref[0, :, g * V_EXT:(g + 1) * V_EXT])


def _attention_context(q, k, v, o):
    spec = lambda w: pl.BlockSpec((1, CTX_LEN, w), lambda b: (b, CTX_TILE, 0))
    return pl.pallas_call(
        _attn_ctx_kernel,
        grid=(BATCH,),
        in_specs=[spec(C_WIDTH), spec(KV_WIDTH), spec(N_KV * V_EXT), pl.BlockSpec(memory_space=pl.ANY)],
        out_specs=spec(C_WIDTH),
        out_shape=jax.ShapeDtypeStruct(o.shape, o.dtype),
        input_output_aliases={3: 0},
        compiler_params=_params("parallel"),
        name="attention_context",
    )(q, k, v, o)


def _odd_out_kernel(o_ref, gate_ref, x_ref, mod_ref, wout_ref, fg_ref, out_ref, *, final):
    for piece in range(PIECES):
        rows = slice(piece * ROW_TILE, (piece + 1) * ROW_TILE)
        g = gate_ref[rows, :].astype(F32)
        a = (o_ref[rows, :].astype(F32) * (g * jax.nn.sigmoid(g))).astype(BF16)
        gt = mod_ref[piece % mod_ref.shape[0]][:, 2 * D_MODEL:]
        x = x_ref[rows, :] + gt * jnp.dot(a, wout_ref[...], preferred_element_type=F32)
        if final:
            x = (x * lax.rsqrt(jnp.mean(x * x, axis=-1, keepdims=True) + EPS)) * fg_ref[...]
        out_ref[rows, :] = x


def _odd_out(o, gate, xc, mod_p, w_out, final_g):
    return pl.pallas_call(
        functools.partial(_odd_out_kernel, final=False),
        grid=(N_BIG,),
        in_specs=[_tile(C_WIDTH), _tile(C_WIDTH), _tile(D_MODEL), _MOD_SPEC,
                  _full(D_MODEL, D_MODEL), _full(1, D_MODEL)],
        out_specs=_tile(D_MODEL),
        out_shape=jax.ShapeDtypeStruct((BATCH * NTOK, D_MODEL), F32),
        compiler_params=_params("parallel"),
        name="odd_out",
    )(o, gate, xc, mod_p, w_out, final_g)


def _odd_out_final(o, gate, xc, mod_l, w_out, final_g):
    tile = lambda w: pl.BlockSpec((None, BIG_TILE, w), lambda b, j: (b, j, 0))
    full = lambda *s: pl.BlockSpec(s, lambda b, j: (0,) * len(s))
    per_sample = lambda a, w: a.reshape(BATCH, NTOK, w)
    return pl.pallas_call(
        functools.partial(_odd_out_kernel, final=True),
        grid=(BATCH, SEQ // BIG_TILE),
        in_specs=[tile(C_WIDTH), tile(C_WIDTH), tile(D_MODEL),
                  pl.BlockSpec((1, 1, 3 * D_MODEL), lambda b, j: (b, 0, 0)),
                  full(D_MODEL, D_MODEL), full(1, D_MODEL)],
        out_specs=tile(D_MODEL),
        out_shape=jax.ShapeDtypeStruct((BATCH, SEQ, D_MODEL), F32),
        compiler_params=_params("parallel", "parallel"),
        name="odd_out_final",
    )(per_sample(o, C_WIDTH), per_sample(gate, C_WIDTH), per_sample(xc, D_MODEL), mod_l, w_out, final_g)


def kernel(x, c, ctx, c_ctx, norm_g, w_mod, b_mod, we_in, we_out, gm_v_g, gm_w_s, gm_b_s,
           s5_lam_re, s5_lam_im, s5_log_dt, s5_b_re, s5_b_im, s5_c_re, s5_c_im, s5_d,
           s5_w_glu, s5_b_glu, wo_in, wo_out, q_norm_g, k_norm_g, final_g):
    cond = jnp.concatenate([c, c_ctx[None], jnp.zeros((MOD_ROWS - BATCH - 1, D_MODEL), F32)], axis=0)
    mods = _modulation(cond, w_mod, b_mod).reshape(DEPTH, MOD_ROWS, 1, 3 * D_MODEL)
    xc = jnp.concatenate([x, ctx], axis=1).reshape(BATCH * NTOK, D_MODEL)
    piece_rows = np.where(np.arange(N_TILES)[None, :] == CTX_TILE, CTX_ROW, np.arange(BATCH)[:, None]).reshape(-1)
    cos, sin = _rope_tables()
    perm = _chunk_perm()
    row = lambda a: a.reshape(1, -1)
    per_sample = lambda a: a.reshape(BATCH, NTOK, a.shape[-1])

    for layer in range(DEPTH):
        i = layer // 2
        mod_p = mods[layer][piece_rows]
        if layer % 2 == 0:
            b_full = jnp.repeat(gm_b_s[i].T, A_GROUP_W, axis=1)
            ya, xs, gb, xg = _even_in(xc, mod_p, row(norm_g[layer]), we_in[i].astype(BF16),
                                      row(gm_v_g[i]), gm_w_s[i].astype(BF16), b_full, perm)
            prow, pmat = _s5_params(s5_lam_re[i], s5_lam_im[i], s5_log_dt[i], s5_b_re[i],
                                    s5_b_im[i], s5_c_re[i], s5_c_im[i])
            yg = _s5(xg, prow, pmat)
            xc = _even_out(ya, xs, gb, yg, xc, mod_p, row(s5_d[i]), s5_w_glu[i].astype(BF16),
                           row(s5_b_glu[i]), we_out[i].astype(BF16), perm.T)
        else:
            w_in, q_g, k_g = _permute_heads(wo_in[i], q_norm_g[i], k_norm_g[i])
            q, k, v, gate = _odd_in(xc, mod_p, row(norm_g[layer]), w_in.astype(BF16),
                                    row(q_g), row(k_g), cos, sin)
            q, k, v = per_sample(q), per_sample(k), per_sample(v)
            o = _attention_latent(q, k, v)
            if layer == DEPTH - 1:
                return _odd_out_final(o, gate, xc, mods[layer], wo_out[i].astype(BF16), row(final_g))
            o = _attention_context(q, k, v, o).reshape(BATCH * NTOK, C_WIDTH)
            xc = _odd_out(o, gate, xc, mod_p, wo_out[i].astype(BF16), row(final_g))
```

```python
import functools
import math

import jax
import jax.numpy as jnp
import numpy as np
from jax import lax
from jax.experimental import pallas as pl
from jax.experimental.pallas import tpu as pltpu

F32 = jnp.float32
BF16 = jnp.bfloat16
HIGHEST = lax.Precision.HIGHEST

D_MODEL = 1024
BATCH = 4
SEQ = 4096
DEPTH = 4
GRID_W = 64
CTX_LEN = 256
EPS = 1e-6
NTOK = CTX_LEN + SEQ

LANES = 128
SUBLANES = 8
ROW_TILE = 256
N_TILES = NTOK // ROW_TILE
CTX_TILE = N_TILES - 1
LAT_TILES = N_TILES - 1
BIG_TILE = 1024
PIECES = BIG_TILE // ROW_TILE
N_BIG = BATCH * NTOK // BIG_TILE
assert N_BIG * BIG_TILE == BATCH * NTOK and SEQ % BIG_TILE == 0
VMEM_LIMIT = 56 * 1024 * 1024

CHUNK = 128
A_WIDTH = D_MODEL // 2
A_GROUPS = 4
A_GROUP_W = A_WIDTH // A_GROUPS
B_WIDTH = D_MODEL // 2
B_CH = 16
B_GROUPS = B_WIDTH // B_CH
B_STATE = 64
EVEN_IN = 3 * A_WIDTH + 2 * B_WIDTH
HEAD_DIM = 128
N_Q = D_MODEL // HEAD_DIM
N_KV = 2
Q_PER_KV = N_Q // N_KV
C_WIDTH = N_Q * HEAD_DIM
KV_WIDTH = N_KV * HEAD_DIM
ODD_IN = 2 * C_WIDTH + 2 * KV_WIDTH
ROPE_THETA = 10000.0
ROPE_PAIRS = HEAD_DIM // 4
Q_SCALE = HEAD_DIM ** -0.5 * math.log2(math.e)
V_EXT = 2 * HEAD_DIM
ATTN_TILE = 512
ATTN_UNIT = 128

S5_Q = 16
S5_K = S5_Q * B_CH
N_CHUNKS = NTOK // S5_Q
S5_ROWS = BATCH * N_CHUNKS
TILE_CHUNKS = ROW_TILE // S5_Q
SUB_TILE = ROW_TILE // 2
SUB_CHUNKS = SUB_TILE // S5_Q
assert SUB_TILE == CHUNK
ODD_SUB = ROW_TILE
BLOCKS_PER_SAMPLE = N_CHUNKS // SUBLANES
CTX_BLOCKS = CTX_LEN // S5_Q // SUBLANES
N_PAIRS = B_GROUPS // 2
S5_PAIRS_PER_STEP = 2
GROUPS_PER_VREG = LANES // B_CH
N_LAGS = 2 * S5_Q - 1

MOD_ROWS = 8
CTX_ROW = BATCH


def _params(*sem):
    return pltpu.CompilerParams(dimension_semantics=sem, vmem_limit_bytes=VMEM_LIMIT)


def _mod_row(b, i):
    return jnp.where(i == CTX_TILE, CTX_ROW, b)


def _lane_block(rows):
    return lax.broadcasted_iota(jnp.int32, (rows, LANES), 1) // B_CH


def _move_blocks(pieces, blk):
    acc = None
    for dst, (src, src_blk) in enumerate(pieces):
        sh = ((dst - src_blk) % GROUPS_PER_VREG) * B_CH
        rolled = pltpu.roll(src, sh, 1) if sh else src
        acc = rolled if acc is None else jnp.where(blk == dst, rolled, acc)
    return acc


def _merge_blocks(pick, blk):
    acc = pick(0)
    for k in range(1, GROUPS_PER_VREG):
        acc = jnp.where(blk == k, pick(k), acc)
    return acc


def _block_transpose(src, blk):
    n = GROUPS_PER_VREG
    rolled = []
    for r in range(n):
        m = _merge_blocks(lambda b: src[(b + r) % n], blk)
        rolled.append(pltpu.roll(m, r * B_CH, 1) if r else m)
    return [_merge_blocks(lambda k: rolled[(k - b) % n], blk) for b in range(n)]


def _mod_kernel(cond_ref, w_ref, b_ref, o_ref):
    c = cond_ref[...]
    s = c * jax.nn.sigmoid(c)
    o_ref[0] = jnp.dot(s, w_ref[0], preferred_element_type=F32, precision=HIGHEST) + b_ref[0]


def _modulation(cond, w_mod, b_mod):
    nblk = 3 * D_MODEL // D_MODEL
    return pl.pallas_call(
        _mod_kernel,
        grid=(DEPTH, nblk),
        in_specs=[
            pl.BlockSpec((MOD_ROWS, D_MODEL), lambda l, j: (0, 0)),
            pl.BlockSpec((1, D_MODEL, D_MODEL), lambda l, j: (l, 0, j)),
            pl.BlockSpec((1, 1, D_MODEL), lambda l, j: (l, 0, j)),
        ],
        out_specs=pl.BlockSpec((1, MOD_ROWS, D_MODEL), lambda l, j: (l, 0, j)),
        out_shape=jax.ShapeDtypeStruct((DEPTH, MOD_ROWS, 3 * D_MODEL), F32),
        compiler_params=_params("parallel", "parallel"),
        name="modulation",
    )(cond, w_mod, b_mod.reshape(DEPTH, 1, 3 * D_MODEL))


def _prologue(x, mod, g):
    sh = mod[:, :D_MODEL]
    sc = mod[:, D_MODEL:2 * D_MODEL]
    y = x * lax.rsqrt(jnp.mean(x * x, axis=-1, keepdims=True) + EPS)
    return (y * g) * (1 + sc) + sh


def _chunk_perm():
    r = jnp.arange(SUB_TILE)
    source = (r % SUB_CHUNKS) * S5_Q + r // SUB_CHUNKS
    return (source[:, None] == r[None, :]).astype(BF16)


def _even_in_kernel(x_ref, mod_ref, g_ref, win_ref, vg_ref, ws_ref, bs_ref, perm_ref,
                    ya_ref, xs_ref, gb_ref, xg_ref):
    blk = _lane_block(SUB_CHUNKS)
    xg_parts = []
    for sub in range(BIG_TILE // SUB_TILE):
        rows = slice(sub * SUB_TILE, (sub + 1) * SUB_TILE)
        h = _prologue(x_ref[rows, :], mod_ref[sub * SUB_TILE // ROW_TILE], g_ref[...])
        z = jnp.dot(h.astype(BF16), win_ref[...], preferred_element_type=F32)
        u = z[:, 0:A_WIDTH]
        v = z[:, A_WIDTH:2 * A_WIDTH]
        ga = z[:, 2 * A_WIDTH:3 * A_WIDTH]
        xs = z[:, 3 * A_WIDTH:3 * A_WIDTH + B_WIDTH]
        xs_ref[rows, :] = xs
        gb_ref[rows, :] = z[:, 3 * A_WIDTH + B_WIDTH:].astype(BF16)
        mu = jnp.mean(v, axis=-1, keepdims=True)
        vc = v - mu
        var = jnp.mean(vc * vc, axis=-1, keepdims=True)
        vn = ((vc * lax.rsqrt(var + EPS)) * vg_ref[...]).astype(BF16)
        mixed = jnp.concatenate(
            [jnp.dot(ws_ref[g], vn[:, g * A_GROUP_W:(g + 1) * A_GROUP_W], preferred_element_type=F32)
             for g in range(A_GROUPS)], axis=1) + bs_ref[...]
        ya_ref[rows, :] = ((u * mixed) * (ga * jax.nn.sigmoid(ga))).astype(BF16)

        r = jnp.dot(perm_ref[...], xs.astype(BF16), preferred_element_type=F32)
        parts = [[None] * (S5_K // LANES) for _ in range(B_GROUPS)]
        for col in range(B_WIDTH // LANES):
            for j in range(S5_K // LANES):
                steps = [r[(j * GROUPS_PER_VREG + k) * SUB_CHUNKS:(j * GROUPS_PER_VREG + k + 1) * SUB_CHUNKS,
                           col * LANES:(col + 1) * LANES] for k in range(GROUPS_PER_VREG)]
                for b, a in enumerate(_block_transpose(steps, blk)):
                    parts[col * GROUPS_PER_VREG + b][j] = a
        xg_parts.append([jnp.concatenate(p, axis=1) for p in parts])
        if len(xg_parts) == 2:
            first = (sub - 1) * SUB_CHUNKS
            for g in range(B_GROUPS):
                xg_ref[g, first:first + 2 * SUB_CHUNKS, :] = jnp.concatenate(
                    [xg_parts[0][g], xg_parts[1][g]], axis=0).astype(BF16)
            xg_parts = []


def _tile(w):
    return pl.BlockSpec((BIG_TILE, w), lambda t: (t, 0))


def _full(*shape):
    return pl.BlockSpec(shape, lambda t: (0,) * len(shape))


_MOD_SPEC = pl.BlockSpec((PIECES, 1, 3 * D_MODEL), lambda t: (t, 0, 0))
_GROUP_SPEC = pl.BlockSpec((B_GROUPS, BIG_TILE // S5_Q, S5_K), lambda t: (0, t, 0))


def _even_in(xc, mod_p, norm_g, w_in, v_g, w_s, b_full, perm):
    return pl.pallas_call(
        _even_in_kernel,
        grid=(N_BIG,),
        in_specs=[
            _tile(D_MODEL), _MOD_SPEC, _full(1, D_MODEL), _full(D_MODEL, EVEN_IN), _full(1, A_WIDTH),
            _full(A_GROUPS, CHUNK, CHUNK), _full(CHUNK, A_WIDTH), _full(SUB_TILE, SUB_TILE),
        ],
        out_specs=[_tile(A_WIDTH), _tile(B_WIDTH), _tile(B_WIDTH), _GROUP_SPEC],
        out_shape=[
            jax.ShapeDtypeStruct((BATCH * NTOK, A_WIDTH), BF16),
            jax.ShapeDtypeStruct((BATCH * NTOK, B_WIDTH), F32),
            jax.ShapeDtypeStruct((BATCH * NTOK, B_WIDTH), BF16),
            jax.ShapeDtypeStruct((B_GROUPS, S5_ROWS, S5_K), BF16),
        ],
        compiler_params=_params("parallel"),
        name="even_in",
    )(xc, mod_p, norm_g, w_in, v_g, w_s, b_full, perm)


def _s5_kernel(xg_ref, prow_ref, pmat_ref, yg_ref, sl_ref, bst_ref, toep_ref, cst_ref, lhs_ref):
    for n in range(S5_PAIRS_PER_STEP):
        two = pl.ds(2 * n, 2)
        _s5_pair(xg_ref.at[two], prow_ref.at[n], pmat_ref.at[n], yg_ref.at[two], sl_ref.at[n],
                 bst_ref.at[n], toep_ref.at[two], cst_ref.at[two], lhs_ref.at[n])


def _s5_pair(xg_ref, prow_ref, pmat_ref, yg_ref, sl_ref, bst_ref, toep_ref, cst_ref, lhs_ref):
    lane = lax.broadcasted_iota(jnp.int32, (S5_Q, LANES), 1)
    in_group = (lane < B_STATE, lane >= B_STATE)
    blk = lane // B_CH
    zero = jnp.zeros((S5_Q, LANES), F32)

    def cmul(ar, ai, br, bi):
        return ar * br - ai * bi, ar * bi + ai * br

    w_in, c_out, scan_consts, c_rows = [], [], [], []
    for d in range(2):
        lam_re = prow_ref[d, 0:1, :]
        lam_im = prow_ref[d, 1:2, :]
        dt = jnp.exp(prow_ref[d, 2:3, :])
        z_re, z_im = lam_re * dt, lam_im * dt
        mag = jnp.exp(z_re)
        lb_re, lb_im = mag * jnp.cos(z_im), mag * jnp.sin(z_im)
        den = lam_re * lam_re + lam_im * lam_im
        n_re, n_im = lb_re - 1.0, lb_im
        f_re = (n_re * lam_re + n_im * lam_im) / den
        f_im = (n_im * lam_re - n_re * lam_im) / den
        b_re, b_im = pmat_ref[d, 0], pmat_ref[d, 1]
        c_re, c_im = pmat_ref[d, 2], pmat_ref[d, 3]
        bb_re, bb_im = cmul(f_re, f_im, b_re, b_im)
        c_rows.append((c_re, c_im))

        def powers(steps):
            m = jnp.exp(steps * z_re)
            return m * jnp.cos(steps * z_im), m * jnp.sin(steps * z_im)

        j = lax.broadcasted_iota(jnp.int32, (3 * SUBLANES, LANES), 0).astype(F32)
        p_re, p_im = powers(j)
        row = lambda a, k: a[k:k + 1, :]
        w_in.append([cmul(bb_re, bb_im, row(p_re, k), row(p_im, k)) for k in range(S5_Q)])
        c_out.append([cmul(c_re, c_im, row(p_re, k), row(p_im, k)) for k in range(S5_Q + 1)])

        i8 = lax.broadcasted_iota(jnp.int32, (SUBLANES, LANES), 0)
        order = i8 if d == 0 else (SUBLANES - 1) - i8
        apow = powers((order * S5_Q).astype(F32))
        bc = lambda r: powers(jnp.full((SUBLANES, LANES), r * S5_Q, F32))

        def doubling(r):
            inside = i8 >= r if d == 0 else i8 < SUBLANES - r
            return tuple(jnp.where(inside, a, 0.0) for a in bc(r))

        scan_consts.append((apow, [doubling(1), doubling(2), doubling(4)], bc(SUBLANES)))

    for e in range(2):
        sel = lambda a: jnp.where(in_group[e], a, 0.0)
        for s in range(S5_Q):
            fr, fi = w_in[0][S5_Q - 1 - s]
            br, bi = w_in[1][s]
            r0 = e * S5_K + s * B_CH
            bst_ref[r0:r0 + B_CH, :] = jnp.concatenate(
                [sel(fr), sel(fi), sel(br), sel(bi)], axis=1).astype(BF16)

    for e in range(2):
        sel = lambda a: jnp.where(in_group[e], a, 0.0)
        for t in range(S5_Q):
            fr, fi = c_out[0][t + 1]
            br, bi = c_out[1][S5_Q - t]
            cst_ref[e, t * B_CH:(t + 1) * B_CH, :] = jnp.concatenate(
                [sel(fr), sel(-fi), sel(br), sel(-bi)], axis=1).astype(BF16)

    for a in range(N_LAGS + 1):
        lag = a - (S5_Q - 1)
        f = w_in[0][lag] if 0 <= lag < S5_Q else (zero, zero)
        b = w_in[1][-lag] if -S5_Q < lag <= 0 else (zero, zero)
        lhs_ref[a * B_CH:(a + 1) * B_CH, :] = jnp.concatenate([f[0], f[1], b[0], b[1]], axis=1)
    cq_rows = []
    for e in range(2):
        sel = lambda a: jnp.where(in_group[e], a, 0.0)
        (fr, fi), (br, bi) = c_rows
        cq_rows.append(jnp.concatenate([sel(fr), sel(-fi), sel(br), sel(-bi)], axis=1))
    cq = jnp.concatenate(cq_rows + [jnp.zeros((LANES - 2 * B_CH, 4 * LANES), F32)], axis=0)
    kst = lax.dot_general(lhs_ref[...], cq, (((1,), (1,)), ((), ())),
                          preferred_element_type=F32, precision=HIGHEST)
    for e in range(2):
        g_cols = []
        for col in range((N_LAGS + 1) // GROUPS_PER_VREG):
            pieces = [(kst[(col * GROUPS_PER_VREG + k) * B_CH:(col * GROUPS_PER_VREG + k + 1) * B_CH, :], e)
                      for k in range(GROUPS_PER_VREG)]
            g_cols.append(_move_blocks(pieces, blk))
        for s in range(S5_Q):
            c0, sh = divmod((S5_Q - 1 - s) * B_CH, LANES)
            if sh == 0:
                cols = g_cols[c0:c0 + 2]
            else:
                rolled = [pltpu.roll(g_cols[c0 + k], LANES - sh, 1) for k in range(3)]
                cols = [jnp.where(lane < LANES - sh, rolled[k], rolled[k + 1]) for k in range(2)]
            toep_ref[e, s * B_CH:(s + 1) * B_CH, :] = jnp.concatenate(cols, axis=1).astype(BF16)

    x0 = xg_ref[0]
    x1 = xg_ref[1]
    sl_ref[...] = jnp.dot(jnp.concatenate([x0, x1], axis=1), bst_ref[...], preferred_element_type=F32)

    i8 = lax.broadcasted_iota(jnp.int32, (SUBLANES, LANES), 0)

    def scan_block(p, carry, consts, fwd):
        (ap_re, ap_im), doubling, (a8_re, a8_im) = consts
        p_re, p_im = p

        def earlier(a, r):
            return pltpu.roll(a, r if fwd else SUBLANES - r, 0)

        for r, (a_re, a_im) in zip((1, 2, 4), doubling):
            m_re, m_im = cmul(a_re, a_im, earlier(p_re, r), earlier(p_im, r))
            p_re, p_im = p_re + m_re, p_im + m_im
        c_re, c_im = carry
        e_re, e_im = cmul(ap_re, ap_im, c_re, c_im)
        first = i8 == (0 if fwd else SUBLANES - 1)
        entering = (jnp.where(first, e_re, earlier(p_re, 1) + e_re),
                    jnp.where(first, e_im, earlier(p_im, 1) + e_im))
        last = SUBLANES - 1 if fwd else 0
        n_re, n_im = cmul(a8_re, a8_im, c_re, c_im)
        bcast = lambda a: jnp.broadcast_to(a[last:last + 1, :], (SUBLANES, LANES))
        return entering, (bcast(p_re) + n_re, bcast(p_im) + n_im)

    z8 = jnp.zeros((SUBLANES, LANES), F32)
    carry = [[(z8, z8), (z8, z8)] for _ in range(BATCH)]
    for j in range(BLOCKS_PER_SAMPLE):
        jf = BLOCKS_PER_SAMPLE - CTX_BLOCKS + j if j < CTX_BLOCKS else j - CTX_BLOCKS
        jb = BLOCKS_PER_SAMPLE - 1 - j
        for b in range(BATCH):
            for d, jd in enumerate((jf, jb)):
                row0 = (b * BLOCKS_PER_SAMPLE + jd) * SUBLANES
                rows, col = slice(row0, row0 + SUBLANES), d * 2 * LANES
                p = (sl_ref[rows, col:col + LANES], sl_ref[rows, col + LANES:col + 2 * LANES])
                entering, carry[b][d] = scan_block(p, carry[b][d], scan_consts[d], d == 0)
                sl_ref[rows, col:col + LANES] = entering[0]
                sl_ref[rows, col + LANES:col + 2 * LANES] = entering[1]

    sp = sl_ref[...].astype(BF16)
    nt = (((1,), (1,)), ((), ()))
    yg_ref[0] = (jnp.dot(x0, toep_ref[0], preferred_element_type=F32)
                 + lax.dot_general(sp, cst_ref[0], nt, preferred_element_type=F32)).astype(BF16)
    yg_ref[1] = (jnp.dot(x1, toep_ref[1], preferred_element_type=F32)
                 + lax.dot_general(sp, cst_ref[1], nt, preferred_element_type=F32)).astype(BF16)


def _s5_params(lam_re, lam_im, log_dt, b_re, b_im, c_re, c_im):
    def rows(a):
        return a.astype(F32).reshape(2, N_PAIRS, 1, 2 * B_STATE).transpose(1, 0, 2, 3)

    dt = jnp.broadcast_to(log_dt[..., None], lam_re.shape)
    pad = jnp.zeros((N_PAIRS, 2, SUBLANES - 3, LANES), F32)
    prow = jnp.concatenate([rows(lam_re), rows(lam_im), rows(dt), pad], axis=2)

    def mats(a, channel_axis):
        a = a.astype(F32)
        if channel_axis == 3:
            a = a.transpose(0, 1, 3, 2)
        a = a.reshape(2, N_PAIRS, 2, B_CH, B_STATE).transpose(1, 0, 3, 2, 4)
        return a.reshape(N_PAIRS, 2, B_CH, 2 * B_STATE)

    pmat = jnp.stack([mats(b_re, 3), mats(b_im, 3), mats(c_re, 2), mats(c_im, 2)], axis=2)
    return prow, pmat


def _s5(xg, prow, pmat):
    n = S5_PAIRS_PER_STEP
    return pl.pallas_call(
        _s5_kernel,
        grid=(N_PAIRS // S5_PAIRS_PER_STEP,),
        in_specs=[
            pl.BlockSpec((2 * n, S5_ROWS, S5_K), lambda q: (q, 0, 0)),
            pl.BlockSpec((n, 2, SUBLANES, LANES), lambda q: (q, 0, 0, 0)),
            pl.BlockSpec((n, 2, 4, B_CH, LANES), lambda q: (q, 0, 0, 0, 0)),
        ],
        out_specs=pl.BlockSpec((2 * n, S5_ROWS, S5_K), lambda q: (q, 0, 0)),
        out_shape=jax.ShapeDtypeStruct((B_GROUPS, S5_ROWS, S5_K), BF16),
        scratch_shapes=[
            pltpu.VMEM((n, S5_ROWS, 4 * LANES), F32),
            pltpu.VMEM((n, 2 * S5_K, 4 * LANES), BF16),
            pltpu.VMEM((2 * n, S5_K, S5_K), BF16),
            pltpu.VMEM((2 * n, S5_K, 4 * LANES), BF16),
            pltpu.VMEM((n, (N_LAGS + 1) * B_CH, 4 * LANES), F32),
        ],
        compiler_params=_params("parallel"),
        name="s5_chunked",
    )(xg, prow, pmat)


def _even_out_kernel(ya_ref, xs_ref, gb_ref, yg_ref, x_ref, mod_ref, d_ref, wglu_ref, bglu_ref,
                     wout_ref, perm_ref, o_ref):
    blk = _lane_block(SUB_CHUNKS)
    for sub in range(BIG_TILE // SUB_TILE):
        rows = slice(sub * SUB_TILE, (sub + 1) * SUB_TILE)
        chunks = slice(sub * SUB_CHUNKS, (sub + 1) * SUB_CHUNKS)
        gt = mod_ref[sub * SUB_TILE // ROW_TILE][:, 2 * D_MODEL:]
        steps = [[None] * (B_WIDTH // LANES) for _ in range(S5_Q)]
        for col in range(B_WIDTH // LANES):
            for j in range(S5_K // LANES):
                groups = [yg_ref[col * GROUPS_PER_VREG + k, chunks, j * LANES:(j + 1) * LANES].astype(F32)
                          for k in range(GROUPS_PER_VREG)]
                for b, a in enumerate(_block_transpose(groups, blk)):
                    steps[j * GROUPS_PER_VREG + b][col] = a
        ys = jnp.dot(perm_ref[...],
                     jnp.concatenate([jnp.concatenate(s, axis=1) for s in steps], axis=0).astype(BF16),
                     preferred_element_type=F32)

        y = ys + d_ref[...] * xs_ref[rows, :]
        y = jax.nn.gelu(y)
        t = jnp.dot(y.astype(BF16), wglu_ref[...], preferred_element_type=F32) + bglu_ref[...]
        y = y * jax.nn.sigmoid(t)
        gb = gb_ref[rows, :].astype(F32)
        yb = (y * (gb * jax.nn.sigmoid(gb))).astype(BF16)
        mix = (jnp.dot(ya_ref[rows, :], wout_ref[0:A_WIDTH, :], preferred_element_type=F32)
               + jnp.dot(yb, wout_ref[A_WIDTH:, :], preferred_element_type=F32))
        o_ref[rows, :] = x_ref[rows, :] + gt * mix


def _even_out(ya, xs, gb, yg, xc, mod_p, d_skip, w_glu, b_glu, w_out, perm):
    return pl.pallas_call(
        _even_out_kernel,
        grid=(N_BIG,),
        in_specs=[
            _tile(A_WIDTH), _tile(B_WIDTH), _tile(B_WIDTH), _GROUP_SPEC, _tile(D_MODEL), _MOD_SPEC,
            _full(1, B_WIDTH), _full(B_WIDTH, B_WIDTH), _full(1, B_WIDTH), _full(D_MODEL, D_MODEL),
            _full(SUB_TILE, SUB_TILE),
        ],
        out_specs=_tile(D_MODEL),
        out_shape=jax.ShapeDtypeStruct((BATCH * NTOK, D_MODEL), F32),
        compiler_params=_params("parallel"),
        name="even_out",
    )(ya, xs, gb, yg, xc, mod_p, d_skip, w_glu, b_glu, w_out, perm)


_HEAD_ORDER = np.concatenate([np.arange(0, 32), np.arange(64, 96), np.arange(32, 64), np.arange(96, 128)])


def _rope_tables():
    t = np.arange(SEQ)
    freqs = ROPE_THETA ** (-np.arange(ROPE_PAIRS, dtype=np.float64) / ROPE_PAIRS)
    ang = np.concatenate([(t // GRID_W)[:, None] * freqs, (t % GRID_W)[:, None] * freqs], axis=1)
    cos = np.concatenate([np.cos(ang), np.cos(ang)], axis=1)
    sin = np.concatenate([-np.sin(ang), np.sin(ang)], axis=1)
    pad = lambda a, v: np.concatenate([a, np.full((CTX_LEN, HEAD_DIM), v)], axis=0).astype(np.float32)
    return jnp.asarray(pad(cos, 1.0)), jnp.asarray(pad(sin, 0.0))


def _permute_heads(w_in, q_g, k_g):
    n_heads = N_Q + N_KV
    cols = (np.arange(n_heads)[:, None] * HEAD_DIM + _HEAD_ORDER[None, :]).reshape(-1)
    cols = np.concatenate([cols, np.arange(n_heads * HEAD_DIM, ODD_IN)])
    return w_in[:, cols], q_g[_HEAD_ORDER], k_g[_HEAD_ORDER]


def _odd_in_kernel(x_ref, mod_ref, g_ref, win_ref, qg_ref, kg_ref, *refs):
    tabs, (q_ref, k_ref, v_ref, gate_ref) = refs[:2 * PIECES], refs[2 * PIECES:]
    ones = jnp.ones((ROW_TILE, HEAD_DIM), BF16)
    for piece in range(PIECES):
        rows = slice(piece * ROW_TILE, (piece + 1) * ROW_TILE)
        h = _prologue(x_ref[rows, :], mod_ref[piece], g_ref[...])
        z = jnp.dot(h.astype(BF16), win_ref[...], preferred_element_type=F32)
        cos = tabs[2 * piece][...]
        sin = tabs[2 * piece + 1][...]

        def norm_rope(xh, gain, scale):
            xn = (xh * lax.rsqrt(jnp.mean(xh * xh, axis=-1, keepdims=True) + EPS)) * gain
            out = xn * cos + pltpu.roll(xn, HEAD_DIM // 2, 1) * sin
            return (out * scale).astype(BF16) if scale is not None else out.astype(BF16)

        for hq in range(N_Q):
            q_ref[rows, hq * HEAD_DIM:(hq + 1) * HEAD_DIM] = norm_rope(
                z[:, hq * HEAD_DIM:(hq + 1) * HEAD_DIM], qg_ref[...], Q_SCALE)
        for hk in range(N_KV):
            lo = C_WIDTH + hk * HEAD_DIM
            k_ref[rows, hk * HEAD_DIM:(hk + 1) * HEAD_DIM] = norm_rope(
                z[:, lo:lo + HEAD_DIM], kg_ref[...], None)
            lo = C_WIDTH + KV_WIDTH + hk * HEAD_DIM
            v_ref[rows, hk * V_EXT:hk * V_EXT + HEAD_DIM] = z[:, lo:lo + HEAD_DIM].astype(BF16)
            v_ref[rows, hk * V_EXT + HEAD_DIM:(hk + 1) * V_EXT] = ones
        gate_ref[rows, :] = z[:, C_WIDTH + 2 * KV_WIDTH:].astype(BF16)


def _odd_in(xc, mod_p, norm_g, w_in, q_g, k_g, cos, sin):
    tabs, tab_specs = [], []
    for p in range(PIECES):
        spec = pl.BlockSpec((ROW_TILE, HEAD_DIM), lambda t, p=p: ((t * PIECES + p) % N_TILES, 0))
        tabs += [cos, sin]
        tab_specs += [spec, spec]
    return pl.pallas_call(
        _odd_in_kernel,
        grid=(N_BIG,),
        in_specs=[
            _tile(D_MODEL), _MOD_SPEC, _full(1, D_MODEL), _full(D_MODEL, ODD_IN),
            _full(1, HEAD_DIM), _full(1, HEAD_DIM), *tab_specs,
        ],
        out_specs=[_tile(C_WIDTH), _tile(KV_WIDTH), _tile(N_KV * V_EXT), _tile(C_WIDTH)],
        out_shape=[
            jax.ShapeDtypeStruct((BATCH * NTOK, C_WIDTH), BF16),
            jax.ShapeDtypeStruct((BATCH * NTOK, KV_WIDTH), BF16),
            jax.ShapeDtypeStruct((BATCH * NTOK, N_KV * V_EXT), BF16),
            jax.ShapeDtypeStruct((BATCH * NTOK, C_WIDTH), BF16),
        ],
        compiler_params=_params("parallel"),
        name="odd_in",
    )(xc, mod_p, norm_g, w_in, q_g, k_g, *tabs)


_NT = (((1,), (1,)), ((), ()))


def _softmax_weights(q, k):
    s = lax.dot_general(q, k, _NT, preferred_element_type=F32)
    return jnp.exp2(s - jnp.max(s, axis=-1, keepdims=True)).astype(BF16)


def _weighted_values(p, v):
    ov = jnp.dot(p, v, preferred_element_type=F32)
    return (ov[:, :HEAD_DIM] / ov[:, HEAD_DIM:HEAD_DIM + 1]).astype(BF16)


def _attn_lat_kernel(q_ref, k_ref, v_ref, o_ref, p_ref):
    @pl.when(pl.program_id(0) == 0)
    def _():
        p_ref[...] = jnp.ones(p_ref.shape, BF16)

    k = k_ref[0]
    v = v_ref[0]
    for h in range(Q_PER_KV):
        cols = slice(h * HEAD_DIM, (h + 1) * HEAD_DIM)
        for r in range(ATTN_TILE // ATTN_UNIT):
            rows = slice(r * ATTN_UNIT, (r + 1) * ATTN_UNIT)
            p_new = _softmax_weights(q_ref[0, rows, cols], k)
            o_ref[0, rows, cols] = _weighted_values(p_ref[h, rows, :], v)
            p_ref[h, rows, :] = p_new


def _attention_latent(q, k, v):
    n_tiles = SEQ // ATTN_TILE
    n_units = BATCH * N_KV * n_tiles
    width = Q_PER_KV * HEAD_DIM

    def unit(u):
        return u // (N_KV * n_tiles), (u // n_tiles) % N_KV, u % n_tiles

    def q_map(t):
        b, g, i = unit(jnp.minimum(t, n_units - 1))
        return b, i, g

    def k_map(t):
        b, g, _ = unit(jnp.minimum(t, n_units - 1))
        return b, 0, g

    def v_map(t):
        b, g, _ = unit(jnp.maximum(t - 1, 0))
        return b, 0, g

    def o_map(t):
        b, g, i = unit(jnp.maximum(t - 1, 0))
        return b, i, g

    return pl.pallas_call(
        _attn_lat_kernel,
        grid=(n_units + 1,),
        in_specs=[
            pl.BlockSpec((1, ATTN_TILE, width), q_map),
            pl.BlockSpec((1, NTOK, HEAD_DIM), k_map),
            pl.BlockSpec((1, NTOK, V_EXT), v_map),
        ],
        out_specs=pl.BlockSpec((1, ATTN_TILE, width), o_map),
        out_shape=jax.ShapeDtypeStruct((BATCH, SEQ, C_WIDTH), BF16),
        scratch_shapes=[pltpu.VMEM((Q_PER_KV, ATTN_TILE, NTOK), BF16)],
        compiler_params=_params("arbitrary"),
        name="attention_latent",
    )(q, k, v)


def _attn_ctx_kernel(q_ref, k_ref, v_ref, o_ref):
    for h in range(N_Q):
        g = h // Q_PER_KV
        p = _softmax_weights(q_ref[0, :, h * HEAD_DIM:(h + 1) * HEAD_DIM],
                             k_ref[0, :, g * HEAD_DIM:(g + 1) * HEAD_DIM])
        o_ref[0, :, h * HEAD_DIM:(h + 1) * HEAD_DIM] = _weighted_values(
            p, v_ref[0, :, g * V_EXT:(g + 1) * V_EXT])


def _attention_context(q, k, v):
    spec = lambda w: pl.BlockSpec((1, CTX_LEN, w), lambda b: (b, CTX_TILE, 0))
    return pl.pallas_call(
        _attn_ctx_kernel,
        grid=(BATCH,),
        in_specs=[spec(C_WIDTH), spec(KV_WIDTH), spec(N_KV * V_EXT)],
        out_specs=pl.BlockSpec((1, CTX_LEN, C_WIDTH), lambda b: (b, 0, 0)),
        out_shape=jax.ShapeDtypeStruct((BATCH, CTX_LEN, C_WIDTH), BF16),
        compiler_params=_params("parallel"),
        name="attention_context",
    )(q, k, v)


def _odd_out_kernel(*refs, final):
    if final:
        o_ref, gate_ref, x_ref, mod_ref, wout_ref, fg_ref, out_ref = refs
        attn = lambda piece: o_ref[piece * ROW_TILE:(piece + 1) * ROW_TILE, :]
    else:
        o_refs, (gate_ref, x_ref, mod_ref, wout_ref, fg_ref, out_ref) = refs[:2 * PIECES], refs[2 * PIECES:]

        def attn(piece):
            is_ctx = (pl.program_id(0) * PIECES + piece) % N_TILES == CTX_TILE
            return jnp.where(is_ctx, o_refs[2 * piece + 1][...], o_refs[2 * piece][...])

    for piece in range(PIECES):
        rows = slice(piece * ROW_TILE, (piece + 1) * ROW_TILE)
        g = gate_ref[rows, :].astype(F32)
        a = (attn(piece).astype(F32) * (g * jax.nn.sigmoid(g))).astype(BF16)
        gt = mod_ref[piece % mod_ref.shape[0]][:, 2 * D_MODEL:]
        x = x_ref[rows, :] + gt * jnp.dot(a, wout_ref[...], preferred_element_type=F32)
        if final:
            x = (x * lax.rsqrt(jnp.mean(x * x, axis=-1, keepdims=True) + EPS)) * fg_ref[...]
        out_ref[rows, :] = x


def _odd_out(o_lat, o_ctx, gate, xc, mod_p, w_out, final_g):
    o_args, o_specs = [], []
    for p in range(PIECES):
        sample = lambda t, p=p: (t * PIECES + p) // N_TILES
        lat_tile = lambda t, p=p: jnp.minimum((t * PIECES + p) % N_TILES, LAT_TILES - 1)
        o_args += [o_lat, o_ctx]
        o_specs += [
            pl.BlockSpec((None, ROW_TILE, C_WIDTH), lambda t, s=sample, i=lat_tile: (s(t), i(t), 0)),
            pl.BlockSpec((None, CTX_LEN, C_WIDTH), lambda t, s=sample: (s(t), 0, 0)),
        ]
    return pl.pallas_call(
        functools.partial(_odd_out_kernel, final=False),
        grid=(N_BIG,),
        in_specs=[*o_specs, _tile(C_WIDTH), _tile(D_MODEL), _MOD_SPEC,
                  _full(D_MODEL, D_MODEL), _full(1, D_MODEL)],
        out_specs=_tile(D_MODEL),
        out_shape=jax.ShapeDtypeStruct((BATCH * NTOK, D_MODEL), F32),
        compiler_params=_params("parallel"),
        name="odd_out",
    )(*o_args, gate, xc, mod_p, w_out, final_g)


def _odd_out_final(o, gate, xc, mod_l, w_out, final_g):
    tile = lambda w: pl.BlockSpec((None, BIG_TILE, w), lambda b, j: (b, j, 0))
    full = lambda *s: pl.BlockSpec(s, lambda b, j: (0,) * len(s))
    per_sample = lambda a, w: a.reshape(BATCH, NTOK, w)
    return pl.pallas_call(
        functools.partial(_odd_out_kernel, final=True),
        grid=(BATCH, SEQ // BIG_TILE),
        in_specs=[tile(C_WIDTH), tile(C_WIDTH), tile(D_MODEL),
                  pl.BlockSpec((1, 1, 3 * D_MODEL), lambda b, j: (b, 0, 0)),
                  full(D_MODEL, D_MODEL), full(1, D_MODEL)],
        out_specs=tile(D_MODEL),
        out_shape=jax.ShapeDtypeStruct((BATCH, SEQ, D_MODEL), F32),
        compiler_params=_params("parallel", "parallel"),
        name="odd_out_final",
    )(o, per_sample(gate, C_WIDTH), per_sample(xc, D_MODEL), mod_l, w_out, final_g)


def kernel(x, c, ctx, c_ctx, norm_g, w_mod, b_mod, we_in, we_out, gm_v_g, gm_w_s, gm_b_s,
           s5_lam_re, s5_lam_im, s5_log_dt, s5_b_re, s5_b_im, s5_c_re, s5_c_im, s5_d,
           s5_w_glu, s5_b_glu, wo_in, wo_out, q_norm_g, k_norm_g, final_g):
    cond = jnp.concatenate([c, c_ctx[None], jnp.zeros((MOD_ROWS - BATCH - 1, D_MODEL), F32)], axis=0)
    mods = _modulation(cond, w_mod, b_mod).reshape(DEPTH, MOD_ROWS, 1, 3 * D_MODEL)
    xc = jnp.concatenate([x, ctx], axis=1).reshape(BATCH * NTOK, D_MODEL)
    piece_rows = np.where(np.arange(N_TILES)[None, :] == CTX_TILE, CTX_ROW, np.arange(BATCH)[:, None]).reshape(-1)
    cos, sin = _rope_tables()
    perm = _chunk_perm()
    row = lambda a: a.reshape(1, -1)
    per_sample = lambda a: a.reshape(BATCH, NTOK, a.shape[-1])

    for layer in range(DEPTH):
        i = layer // 2
        mod_p = mods[layer][piece_rows]
        if layer % 2 == 0:
            b_full = jnp.repeat(gm_b_s[i].T, A_GROUP_W, axis=1)
            ya, xs, gb, xg = _even_in(xc, mod_p, row(norm_g[layer]), we_in[i].astype(BF16),
                                      row(gm_v_g[i]), gm_w_s[i].astype(BF16), b_full, perm)
            prow, pmat = _s5_params(s5_lam_re[i], s5_lam_im[i], s5_log_dt[i], s5_b_re[i],
                                    s5_b_im[i], s5_c_re[i], s5_c_im[i])
            yg = _s5(xg, prow, pmat)
            xc = _even_out(ya, xs, gb, yg, xc, mod_p, row(s5_d[i]), s5_w_glu[i].astype(BF16),
                           row(s5_b_glu[i]), we_out[i].astype(BF16), perm.T)
        else:
            w_in, q_g, k_g = _permute_heads(wo_in[i], q_norm_g[i], k_norm_g[i])
            q, k, v, gate = _odd_in(xc, mod_p, row(norm_g[layer]), w_in.astype(BF16),
                                    row(q_g), row(k_g), cos, sin)
            q, k, v = per_sample(q), per_sample(k), per_sample(v)
            o = _attention_latent(q, k, v)
            if layer == DEPTH - 1:
                return _odd_out_final(o, gate, xc, mods[layer], wo_out[i].astype(BF16), row(final_g))
            xc = _odd_out(o, _attention_context(q, k, v), gate, xc, mod_p, wo_out[i].astype(BF16),
                          row(final_g))
```

```python
import functools
import math

import jax
import jax.numpy as jnp
import numpy as np
from jax import lax
from jax.experimental import pallas as pl
from jax.experimental.pallas import tpu as pltpu

F32 = jnp.float32
BF16 = jnp.bfloat16
HIGHEST = lax.Precision.HIGHEST

D_MODEL = 1024
BATCH = 4
SEQ = 4096
DEPTH = 4
GRID_W = 64
CTX_LEN = 256
EPS = 1e-6
NTOK = CTX_LEN + SEQ

LANES = 128
SUBLANES = 8
ROW_TILE = 256
N_TILES = NTOK // ROW_TILE
CTX_TILE = N_TILES - 1
LAT_TILES = N_TILES - 1
BIG_TILE = 1024
PIECES = BIG_TILE // ROW_TILE
N_BIG = BATCH * NTOK // BIG_TILE
assert N_BIG * BIG_TILE == BATCH * NTOK and SEQ % BIG_TILE == 0
VMEM_LIMIT = 56 * 1024 * 1024

CHUNK = 128
A_WIDTH = D_MODEL // 2
A_GROUPS = 4
A_GROUP_W = A_WIDTH // A_GROUPS
B_WIDTH = D_MODEL // 2
B_CH = 16
B_GROUPS = B_WIDTH // B_CH
B_STATE = 64
EVEN_IN = 3 * A_WIDTH + 2 * B_WIDTH
HEAD_DIM = 128
N_Q = D_MODEL // HEAD_DIM
N_KV = 2
Q_PER_KV = N_Q // N_KV
C_WIDTH = N_Q * HEAD_DIM
KV_WIDTH = N_KV * HEAD_DIM
ODD_IN = 2 * C_WIDTH + 2 * KV_WIDTH
ROPE_THETA = 10000.0
ROPE_PAIRS = HEAD_DIM // 4
Q_SCALE = HEAD_DIM ** -0.5 * math.log2(math.e)
V_EXT = 2 * HEAD_DIM
ATTN_TILE = 512
ATTN_UNIT = 128

S5_Q = 16
S5_K = S5_Q * B_CH
N_CHUNKS = NTOK // S5_Q
S5_ROWS = BATCH * N_CHUNKS
SUB_TILE = CHUNK
SUB_CHUNKS = SUB_TILE // S5_Q
RELAYOUT_ROWS = 16 * S5_Q
BLOCKS_PER_SAMPLE = N_CHUNKS // SUBLANES
CTX_BLOCKS = CTX_LEN // S5_Q // SUBLANES
N_PAIRS = B_GROUPS // 2
S5_PAIRS_PER_STEP = 2
GROUPS_PER_VREG = LANES // B_CH
N_LAGS = 2 * S5_Q - 1

MOD_ROWS = 8
CTX_ROW = BATCH


def _params(*sem):
    return pltpu.CompilerParams(dimension_semantics=sem, vmem_limit_bytes=VMEM_LIMIT)


def _mod_row(b, i):
    return jnp.where(i == CTX_TILE, CTX_ROW, b)


def _lane_block(rows):
    return lax.broadcasted_iota(jnp.int32, (rows, LANES), 1) // B_CH


def _move_blocks(pieces, blk):
    acc = None
    for dst, (src, src_blk) in enumerate(pieces):
        sh = ((dst - src_blk) % GROUPS_PER_VREG) * B_CH
        rolled = pltpu.roll(src, sh, 1) if sh else src
        acc = rolled if acc is None else jnp.where(blk == dst, rolled, acc)
    return acc


def _merge_blocks(pick, blk):
    acc = pick(0)
    for k in range(1, GROUPS_PER_VREG):
        acc = jnp.where(blk == k, pick(k), acc)
    return acc


def _block_transpose(src, blk):
    n = GROUPS_PER_VREG
    rolled = []
    for r in range(n):
        m = _merge_blocks(lambda b: src[(b + r) % n], blk)
        rolled.append(pltpu.roll(m, r * B_CH, 1) if r else m)
    return [_merge_blocks(lambda k: rolled[(k - b) % n], blk) for b in range(n)]


def _mod_kernel(cond_ref, w_ref, b_ref, o_ref):
    c = cond_ref[...]
    s = c * jax.nn.sigmoid(c)
    o_ref[0] = jnp.dot(s, w_ref[0], preferred_element_type=F32, precision=HIGHEST) + b_ref[0]


def _modulation(cond, w_mod, b_mod):
    nblk = 3 * D_MODEL // D_MODEL
    return pl.pallas_call(
        _mod_kernel,
        grid=(DEPTH, nblk),
        in_specs=[
            pl.BlockSpec((MOD_ROWS, D_MODEL), lambda l, j: (0, 0)),
            pl.BlockSpec((1, D_MODEL, D_MODEL), lambda l, j: (l, 0, j)),
            pl.BlockSpec((1, 1, D_MODEL), lambda l, j: (l, 0, j)),
        ],
        out_specs=pl.BlockSpec((1, MOD_ROWS, D_MODEL), lambda l, j: (l, 0, j)),
        out_shape=jax.ShapeDtypeStruct((DEPTH, MOD_ROWS, 3 * D_MODEL), F32),
        compiler_params=_params("parallel", "parallel"),
        name="modulation",
    )(cond, w_mod, b_mod.reshape(DEPTH, 1, 3 * D_MODEL))


def _prologue(x, mod, g):
    sh = mod[:, :D_MODEL]
    sc = mod[:, D_MODEL:2 * D_MODEL]
    y = x * lax.rsqrt(jnp.mean(x * x, axis=-1, keepdims=True) + EPS)
    return (y * g) * (1 + sc) + sh


def _chunk_perm():
    r = np.arange(RELAYOUT_ROWS)
    swapped = (r % S5_Q) * S5_Q + r // S5_Q
    return jnp.asarray(swapped[:, None] == r[None, :], BF16)


def _even_in_kernel(x_ref, mod_ref, g_ref, win_ref, vg_ref, ws_ref, bs_ref, perm_ref,
                    ya_ref, xs_ref, gb_ref, xg_ref):
    blk = _lane_block(S5_Q)
    xs_prev = None
    for sub in range(BIG_TILE // SUB_TILE):
        rows = slice(sub * SUB_TILE, (sub + 1) * SUB_TILE)
        h = _prologue(x_ref[rows, :], mod_ref[sub * SUB_TILE // ROW_TILE], g_ref[...])
        z = jnp.dot(h.astype(BF16), win_ref[...], preferred_element_type=F32)
        u = z[:, 0:A_WIDTH]
        v = z[:, A_WIDTH:2 * A_WIDTH]
        ga = z[:, 2 * A_WIDTH:3 * A_WIDTH]
        xs = z[:, 3 * A_WIDTH:3 * A_WIDTH + B_WIDTH]
        xs_ref[rows, :] = xs
        gb_ref[rows, :] = z[:, 3 * A_WIDTH + B_WIDTH:].astype(BF16)
        mu = jnp.mean(v, axis=-1, keepdims=True)
        vc = v - mu
        var = jnp.mean(vc * vc, axis=-1, keepdims=True)
        vn = ((vc * lax.rsqrt(var + EPS)) * vg_ref[...]).astype(BF16)
        mixed = jnp.concatenate(
            [jnp.dot(ws_ref[g], vn[:, g * A_GROUP_W:(g + 1) * A_GROUP_W], preferred_element_type=F32)
             for g in range(A_GROUPS)], axis=1) + bs_ref[...]
        ya_ref[rows, :] = ((u * mixed) * (ga * jax.nn.sigmoid(ga))).astype(BF16)

        if sub % 2 == 0:
            xs_prev = xs.astype(BF16)
            continue
        r = jnp.dot(perm_ref[...], jnp.concatenate([xs_prev, xs.astype(BF16)], axis=0),
                    preferred_element_type=F32)
        parts = [[None] * (S5_K // LANES) for _ in range(B_GROUPS)]
        for col in range(B_WIDTH // LANES):
            for j in range(S5_K // LANES):
                steps = [r[(j * GROUPS_PER_VREG + k) * S5_Q:(j * GROUPS_PER_VREG + k + 1) * S5_Q,
                           col * LANES:(col + 1) * LANES] for k in range(GROUPS_PER_VREG)]
                for b, a in enumerate(_block_transpose(steps, blk)):
                    parts[col * GROUPS_PER_VREG + b][j] = a
        first = (sub - 1) * SUB_CHUNKS
        for g in range(B_GROUPS):
            xg_ref[g, first:first + RELAYOUT_ROWS // S5_Q, :] = jnp.concatenate(parts[g], axis=1).astype(BF16)


def _tile(w):
    return pl.BlockSpec((BIG_TILE, w), lambda t: (t, 0))


def _full(*shape):
    return pl.BlockSpec(shape, lambda t: (0,) * len(shape))


_MOD_SPEC = pl.BlockSpec((PIECES, 1, 3 * D_MODEL), lambda t: (t, 0, 0))
_GROUP_SPEC = pl.BlockSpec((B_GROUPS, BIG_TILE // S5_Q, S5_K), lambda t: (0, t, 0))


def _even_in(xc, mod_p, norm_g, w_in, v_g, w_s, b_full, perm):
    return pl.pallas_call(
        _even_in_kernel,
        grid=(N_BIG,),
        in_specs=[
            _tile(D_MODEL), _MOD_SPEC, _full(1, D_MODEL), _full(D_MODEL, EVEN_IN), _full(1, A_WIDTH),
            _full(A_GROUPS, CHUNK, CHUNK), _full(CHUNK, A_WIDTH), _full(RELAYOUT_ROWS, RELAYOUT_ROWS),
        ],
        out_specs=[_tile(A_WIDTH), _tile(B_WIDTH), _tile(B_WIDTH), _GROUP_SPEC],
        out_shape=[
            jax.ShapeDtypeStruct((BATCH * NTOK, A_WIDTH), BF16),
            jax.ShapeDtypeStruct((BATCH * NTOK, B_WIDTH), F32),
            jax.ShapeDtypeStruct((BATCH * NTOK, B_WIDTH), BF16),
            jax.ShapeDtypeStruct((B_GROUPS, S5_ROWS, S5_K), BF16),
        ],
        compiler_params=_params("parallel"),
        name="even_in",
    )(xc, mod_p, norm_g, w_in, v_g, w_s, b_full, perm)


def _s5_kernel(xg_ref, prow_ref, pmat_ref, yg_ref, sl_ref, bst_ref, toep_ref, cst_ref, lhs_ref):
    for n in range(S5_PAIRS_PER_STEP):
        two = pl.ds(2 * n, 2)
        _s5_pair(xg_ref.at[two], prow_ref.at[n], pmat_ref.at[n], yg_ref.at[two], sl_ref.at[n],
                 bst_ref.at[n], toep_ref.at[two], cst_ref.at[two], lhs_ref.at[n])


def _s5_pair(xg_ref, prow_ref, pmat_ref, yg_ref, sl_ref, bst_ref, toep_ref, cst_ref, lhs_ref):
    lane = lax.broadcasted_iota(jnp.int32, (S5_Q, LANES), 1)
    in_group = (lane < B_STATE, lane >= B_STATE)
    blk = lane // B_CH
    zero = jnp.zeros((S5_Q, LANES), F32)

    def cmul(ar, ai, br, bi):
        return ar * br - ai * bi, ar * bi + ai * br

    w_in, c_out, scan_consts, c_rows = [], [], [], []
    for d in range(2):
        lam_re = prow_ref[d, 0:1, :]
        lam_im = prow_ref[d, 1:2, :]
        dt = jnp.exp(prow_ref[d, 2:3, :])
        z_re, z_im = lam_re * dt, lam_im * dt
        mag = jnp.exp(z_re)
        lb_re, lb_im = mag * jnp.cos(z_im), mag * jnp.sin(z_im)
        den = lam_re * lam_re + lam_im * lam_im
        n_re, n_im = lb_re - 1.0, lb_im
        f_re = (n_re * lam_re + n_im * lam_im) / den
        f_im = (n_im * lam_re - n_re * lam_im) / den
        b_re, b_im = pmat_ref[d, 0], pmat_ref[d, 1]
        c_re, c_im = pmat_ref[d, 2], pmat_ref[d, 3]
        bb_re, bb_im = cmul(f_re, f_im, b_re, b_im)
        c_rows.append((c_re, c_im))

        def powers(steps):
            m = jnp.exp(steps * z_re)
            return m * jnp.cos(steps * z_im), m * jnp.sin(steps * z_im)

        j = lax.broadcasted_iota(jnp.int32, (3 * SUBLANES, LANES), 0).astype(F32)
        p_re, p_im = powers(j)
        row = lambda a, k: a[k:k + 1, :]
        w_in.append([cmul(bb_re, bb_im, row(p_re, k), row(p_im, k)) for k in range(S5_Q)])
        c_out.append([cmul(c_re, c_im, row(p_re, k), row(p_im, k)) for k in range(S5_Q + 1)])

        i8 = lax.broadcasted_iota(jnp.int32, (SUBLANES, LANES), 0)
        order = i8 if d == 0 else (SUBLANES - 1) - i8
        apow = powers((order * S5_Q).astype(F32))
        bc = lambda r: powers(jnp.full((SUBLANES, LANES), r * S5_Q, F32))

        def doubling(r):
            inside = i8 >= r if d == 0 else i8 < SUBLANES - r
            return tuple(jnp.where(inside, a, 0.0) for a in bc(r))

        scan_consts.append((apow, [doubling(1), doubling(2), doubling(4)], bc(SUBLANES)))

    for e in range(2):
        sel = lambda a: jnp.where(in_group[e], a, 0.0)
        for s in range(S5_Q):
            fr, fi = w_in[0][S5_Q - 1 - s]
            br, bi = w_in[1][s]
            r0 = e * S5_K + s * B_CH
            bst_ref[r0:r0 + B_CH, :] = jnp.concatenate(
                [sel(fr), sel(fi), sel(br), sel(bi)], axis=1).astype(BF16)

    for e in range(2):
        sel = lambda a: jnp.where(in_group[e], a, 0.0)
        for t in range(S5_Q):
            fr, fi = c_out[0][t + 1]
            br, bi = c_out[1][S5_Q - t]
            cst_ref[e, t * B_CH:(t + 1) * B_CH, :] = jnp.concatenate(
                [sel(fr), sel(-fi), sel(br), sel(-bi)], axis=1).astype(BF16)

    for a in range(N_LAGS + 1):
        lag = a - (S5_Q - 1)
        f = w_in[0][lag] if 0 <= lag < S5_Q else (zero, zero)
        b = w_in[1][-lag] if -S5_Q < lag <= 0 else (zero, zero)
        lhs_ref[a * B_CH:(a + 1) * B_CH, :] = jnp.concatenate([f[0], f[1], b[0], b[1]], axis=1)
    cq_rows = []
    for e in range(2):
        sel = lambda a: jnp.where(in_group[e], a, 0.0)
        (fr, fi), (br, bi) = c_rows
        cq_rows.append(jnp.concatenate([sel(fr), sel(-fi), sel(br), sel(-bi)], axis=1))
    cq = jnp.concatenate(cq_rows + [jnp.zeros((LANES - 2 * B_CH, 4 * LANES), F32)], axis=0)
    kst = lax.dot_general(lhs_ref[...], cq, (((1,), (1,)), ((), ())),
                          preferred_element_type=F32, precision=HIGHEST)
    for e in range(2):
        g_cols = []
        for col in range((N_LAGS + 1) // GROUPS_PER_VREG):
            pieces = [(kst[(col * GROUPS_PER_VREG + k) * B_CH:(col * GROUPS_PER_VREG + k + 1) * B_CH, :], e)
                      for k in range(GROUPS_PER_VREG)]
            g_cols.append(_move_blocks(pieces, blk))
        for s in range(S5_Q):
            c0, sh = divmod((S5_Q - 1 - s) * B_CH, LANES)
            if sh == 0:
                cols = g_cols[c0:c0 + 2]
            else:
                rolled = [pltpu.roll(g_cols[c0 + k], LANES - sh, 1) for k in range(3)]
                cols = [jnp.where(lane < LANES - sh, rolled[k], rolled[k + 1]) for k in range(2)]
            toep_ref[e, s * B_CH:(s + 1) * B_CH, :] = jnp.concatenate(cols, axis=1).astype(BF16)

    x0 = xg_ref[0]
    x1 = xg_ref[1]
    sl_ref[...] = jnp.dot(jnp.concatenate([x0, x1], axis=1), bst_ref[...], preferred_element_type=F32)

    i8 = lax.broadcasted_iota(jnp.int32, (SUBLANES, LANES), 0)

    def scan_block(p, carry, consts, fwd):
        (ap_re, ap_im), doubling, (a8_re, a8_im) = consts
        p_re, p_im = p

        def earlier(a, r):
            return pltpu.roll(a, r if fwd else SUBLANES - r, 0)

        for r, (a_re, a_im) in zip((1, 2, 4), doubling):
            m_re, m_im = cmul(a_re, a_im, earlier(p_re, r), earlier(p_im, r))
            p_re, p_im = p_re + m_re, p_im + m_im
        c_re, c_im = carry
        e_re, e_im = cmul(ap_re, ap_im, c_re, c_im)
        first = i8 == (0 if fwd else SUBLANES - 1)
        entering = (jnp.where(first, e_re, earlier(p_re, 1) + e_re),
                    jnp.where(first, e_im, earlier(p_im, 1) + e_im))
        last = SUBLANES - 1 if fwd else 0
        n_re, n_im = cmul(a8_re, a8_im, c_re, c_im)
        bcast = lambda a: jnp.broadcast_to(a[last:last + 1, :], (SUBLANES, LANES))
        return entering, (bcast(p_re) + n_re, bcast(p_im) + n_im)

    z8 = jnp.zeros((SUBLANES, LANES), F32)
    carry = [[(z8, z8), (z8, z8)] for _ in range(BATCH)]
    for j in range(BLOCKS_PER_SAMPLE):
        jf = BLOCKS_PER_SAMPLE - CTX_BLOCKS + j if j < CTX_BLOCKS else j - CTX_BLOCKS
        jb = BLOCKS_PER_SAMPLE - 1 - j
        for b in range(BATCH):
            for d, jd in enumerate((jf, jb)):
                row0 = (b * BLOCKS_PER_SAMPLE + jd) * SUBLANES
                rows, col = slice(row0, row0 + SUBLANES), d * 2 * LANES
                p = (sl_ref[rows, col:col + LANES], sl_ref[rows, col + LANES:col + 2 * LANES])
                entering, carry[b][d] = scan_block(p, carry[b][d], scan_consts[d], d == 0)
                sl_ref[rows, col:col + LANES] = entering[0]
                sl_ref[rows, col + LANES:col + 2 * LANES] = entering[1]

    sp = sl_ref[...].astype(BF16)
    nt = (((1,), (1,)), ((), ()))
    yg_ref[0] = (jnp.dot(x0, toep_ref[0], preferred_element_type=F32)
                 + lax.dot_general(sp, cst_ref[0], nt, preferred_element_type=F32)).astype(BF16)
    yg_ref[1] = (jnp.dot(x1, toep_ref[1], preferred_element_type=F32)
                 + lax.dot_general(sp, cst_ref[1], nt, preferred_element_type=F32)).astype(BF16)


def _s5_params(lam_re, lam_im, log_dt, b_re, b_im, c_re, c_im):
    def rows(a):
        return a.astype(F32).reshape(2, N_PAIRS, 1, 2 * B_STATE).transpose(1, 0, 2, 3)

    dt = jnp.broadcast_to(log_dt[..., None], lam_re.shape)
    pad = jnp.zeros((N_PAIRS, 2, SUBLANES - 3, LANES), F32)
    prow = jnp.concatenate([rows(lam_re), rows(lam_im), rows(dt), pad], axis=2)

    def mats(a, channel_axis):
        a = a.astype(F32)
        if channel_axis == 3:
            a = a.transpose(0, 1, 3, 2)
        a = a.reshape(2, N_PAIRS, 2, B_CH, B_STATE).transpose(1, 0, 3, 2, 4)
        return a.reshape(N_PAIRS, 2, B_CH, 2 * B_STATE)

    pmat = jnp.stack([mats(b_re, 3), mats(b_im, 3), mats(c_re, 2), mats(c_im, 2)], axis=2)
    return prow, pmat


def _s5(xg, prow, pmat):
    n = S5_PAIRS_PER_STEP
    return pl.pallas_call(
        _s5_kernel,
        grid=(N_PAIRS // S5_PAIRS_PER_STEP,),
        in_specs=[
            pl.BlockSpec((2 * n, S5_ROWS, S5_K), lambda q: (q, 0, 0)),
            pl.BlockSpec((n, 2, SUBLANES, LANES), lambda q: (q, 0, 0, 0)),
            pl.BlockSpec((n, 2, 4, B_CH, LANES), lambda q: (q, 0, 0, 0, 0)),
        ],
        out_specs=pl.BlockSpec((2 * n, S5_ROWS, S5_K), lambda q: (q, 0, 0)),
        out_shape=jax.ShapeDtypeStruct((B_GROUPS, S5_ROWS, S5_K), BF16),
        scratch_shapes=[
            pltpu.VMEM((n, S5_ROWS, 4 * LANES), F32),
            pltpu.VMEM((n, 2 * S5_K, 4 * LANES), BF16),
            pltpu.VMEM((2 * n, S5_K, S5_K), BF16),
            pltpu.VMEM((2 * n, S5_K, 4 * LANES), BF16),
            pltpu.VMEM((n, (N_LAGS + 1) * B_CH, 4 * LANES), F32),
        ],
        compiler_params=_params("parallel"),
        name="s5_chunked",
    )(xg, prow, pmat)


def _even_out_kernel(ya_ref, xs_ref, gb_ref, yg_ref, x_ref, mod_ref, d_ref, wglu_ref, bglu_ref,
                     wout_ref, perm_ref, o_ref):
    blk = _lane_block(S5_Q)
    for sub in range(BIG_TILE // RELAYOUT_ROWS):
        rows = slice(sub * RELAYOUT_ROWS, (sub + 1) * RELAYOUT_ROWS)
        chunks = slice(sub * S5_Q, (sub + 1) * S5_Q)
        gt = mod_ref[sub * RELAYOUT_ROWS // ROW_TILE][:, 2 * D_MODEL:]
        steps = [[None] * (B_WIDTH // LANES) for _ in range(S5_Q)]
        for col in range(B_WIDTH // LANES):
            for j in range(S5_K // LANES):
                groups = [yg_ref[col * GROUPS_PER_VREG + k, chunks, j * LANES:(j + 1) * LANES].astype(F32)
                          for k in range(GROUPS_PER_VREG)]
                for b, a in enumerate(_block_transpose(groups, blk)):
                    steps[j * GROUPS_PER_VREG + b][col] = a
        ys = jnp.dot(perm_ref[...],
                     jnp.concatenate([jnp.concatenate(s, axis=1) for s in steps], axis=0).astype(BF16),
                     preferred_element_type=F32)

        y = ys + d_ref[...] * xs_ref[rows, :]
        y = jax.nn.gelu(y)
        t = jnp.dot(y.astype(BF16), wglu_ref[...], preferred_element_type=F32) + bglu_ref[...]
        y = y * jax.nn.sigmoid(t)
        gb = gb_ref[rows, :].astype(F32)
        yb = (y * (gb * jax.nn.sigmoid(gb))).astype(BF16)
        mix = (jnp.dot(ya_ref[rows, :], wout_ref[0:A_WIDTH, :], preferred_element_type=F32)
               + jnp.dot(yb, wout_ref[A_WIDTH:, :], preferred_element_type=F32))
        o_ref[rows, :] = x_ref[rows, :] + gt * mix


def _even_out(ya, xs, gb, yg, xc, mod_p, d_skip, w_glu, b_glu, w_out, perm):
    return pl.pallas_call(
        _even_out_kernel,
        grid=(N_BIG,),
        in_specs=[
            _tile(A_WIDTH), _tile(B_WIDTH), _tile(B_WIDTH), _GROUP_SPEC, _tile(D_MODEL), _MOD_SPEC,
            _full(1, B_WIDTH), _full(B_WIDTH, B_WIDTH), _full(1, B_WIDTH), _full(D_MODEL, D_MODEL),
            _full(RELAYOUT_ROWS, RELAYOUT_ROWS),
        ],
        out_specs=_tile(D_MODEL),
        out_shape=jax.ShapeDtypeStruct((BATCH * NTOK, D_MODEL), F32),
        compiler_params=_params("parallel"),
        name="even_out",
    )(ya, xs, gb, yg, xc, mod_p, d_skip, w_glu, b_glu, w_out, perm)


_HEAD_ORDER = np.concatenate([np.arange(0, 32), np.arange(64, 96), np.arange(32, 64), np.arange(96, 128)])


def _rope_tables():
    t = np.arange(SEQ)
    freqs = ROPE_THETA ** (-np.arange(ROPE_PAIRS, dtype=np.float64) / ROPE_PAIRS)
    ang = np.concatenate([(t // GRID_W)[:, None] * freqs, (t % GRID_W)[:, None] * freqs], axis=1)
    cos = np.concatenate([np.cos(ang), np.cos(ang)], axis=1)
    sin = np.concatenate([-np.sin(ang), np.sin(ang)], axis=1)
    pad = lambda a, v: np.concatenate([a, np.full((CTX_LEN, HEAD_DIM), v)], axis=0).astype(np.float32)
    return jnp.asarray(pad(cos, 1.0)), jnp.asarray(pad(sin, 0.0))


def _permute_heads(w_in, q_g, k_g):
    n_heads = N_Q + N_KV
    cols = (np.arange(n_heads)[:, None] * HEAD_DIM + _HEAD_ORDER[None, :]).reshape(-1)
    cols = np.concatenate([cols, np.arange(n_heads * HEAD_DIM, ODD_IN)])
    return w_in[:, cols], q_g[_HEAD_ORDER], k_g[_HEAD_ORDER]


def _odd_in_kernel(x_ref, mod_ref, g_ref, win_ref, qg_ref, kg_ref, *refs):
    tabs, (q_ref, k_ref, v_ref, gate_ref) = refs[:2 * PIECES], refs[2 * PIECES:]
    ones = jnp.ones((ROW_TILE, HEAD_DIM), BF16)
    for piece in range(PIECES):
        rows = slice(piece * ROW_TILE, (piece + 1) * ROW_TILE)
        h = _prologue(x_ref[rows, :], mod_ref[piece], g_ref[...])
        z = jnp.dot(h.astype(BF16), win_ref[...], preferred_element_type=F32)
        cos = tabs[2 * piece][...]
        sin = tabs[2 * piece + 1][...]

        def norm_rope(xh, gain, scale):
            xn = (xh * lax.rsqrt(jnp.mean(xh * xh, axis=-1, keepdims=True) + EPS)) * gain
            out = xn * cos + pltpu.roll(xn, HEAD_DIM // 2, 1) * sin
            return (out * scale).astype(BF16) if scale is not None else out.astype(BF16)

        for hq in range(N_Q):
            q_ref[rows, hq * HEAD_DIM:(hq + 1) * HEAD_DIM] = norm_rope(
                z[:, hq * HEAD_DIM:(hq + 1) * HEAD_DIM], qg_ref[...], Q_SCALE)
        for hk in range(N_KV):
            lo = C_WIDTH + hk * HEAD_DIM
            k_ref[rows, hk * HEAD_DIM:(hk + 1) * HEAD_DIM] = norm_rope(
                z[:, lo:lo + HEAD_DIM], kg_ref[...], None)
            lo = C_WIDTH + KV_WIDTH + hk * HEAD_DIM
            v_ref[rows, hk * V_EXT:hk * V_EXT + HEAD_DIM] = z[:, lo:lo + HEAD_DIM].astype(BF16)
            v_ref[rows, hk * V_EXT + HEAD_DIM:(hk + 1) * V_EXT] = ones
        gate_ref[rows, :] = z[:, C_WIDTH + 2 * KV_WIDTH:].astype(BF16)


def _odd_in(xc, mod_p, norm_g, w_in, q_g, k_g, cos, sin):
    tabs, tab_specs = [], []
    for p in range(PIECES):
        spec = pl.BlockSpec((ROW_TILE, HEAD_DIM), lambda t, p=p: ((t * PIECES + p) % N_TILES, 0))
        tabs += [cos, sin]
        tab_specs += [spec, spec]
    return pl.pallas_call(
        _odd_in_kernel,
        grid=(N_BIG,),
        in_specs=[
            _tile(D_MODEL), _MOD_SPEC, _full(1, D_MODEL), _full(D_MODEL, ODD_IN),
            _full(1, HEAD_DIM), _full(1, HEAD_DIM), *tab_specs,
        ],
        out_specs=[_tile(C_WIDTH), _tile(KV_WIDTH), _tile(N_KV * V_EXT), _tile(C_WIDTH)],
        out_shape=[
            jax.ShapeDtypeStruct((BATCH * NTOK, C_WIDTH), BF16),
            jax.ShapeDtypeStruct((BATCH * NTOK, KV_WIDTH), BF16),
            jax.ShapeDtypeStruct((BATCH * NTOK, N_KV * V_EXT), BF16),
            jax.ShapeDtypeStruct((BATCH * NTOK, C_WIDTH), BF16),
        ],
        compiler_params=_params("parallel"),
        name="odd_in",
    )(xc, mod_p, norm_g, w_in, q_g, k_g, *tabs)


_NT = (((1,), (1,)), ((), ()))


def _softmax_weights(q, k):
    s = lax.dot_general(q, k, _NT, preferred_element_type=F32)
    return jnp.exp2(s - jnp.max(s, axis=-1, keepdims=True)).astype(BF16)


def _weighted_values(p, v):
    ov = jnp.dot(p, v, preferred_element_type=F32)
    return (ov[:, :HEAD_DIM] / ov[:, HEAD_DIM:HEAD_DIM + 1]).astype(BF16)


def _attn_lat_kernel(q_ref, k_ref, v_ref, o_ref, p_ref):
    @pl.when(pl.program_id(0) == 0)
    def _():
        p_ref[...] = jnp.ones(p_ref.shape, BF16)

    k = k_ref[0]
    v = v_ref[0]
    for h in range(Q_PER_KV):
        cols = slice(h * HEAD_DIM, (h + 1) * HEAD_DIM)
        for r in range(ATTN_TILE // ATTN_UNIT):
            rows = slice(r * ATTN_UNIT, (r + 1) * ATTN_UNIT)
            p_new = _softmax_weights(q_ref[0, rows, cols], k)
            o_ref[0, rows, cols] = _weighted_values(p_ref[h, rows, :], v)
            p_ref[h, rows, :] = p_new


def _attention_latent(q, k, v):
    n_tiles = SEQ // ATTN_TILE
    n_units = BATCH * N_KV * n_tiles
    width = Q_PER_KV * HEAD_DIM

    def unit(u):
        return u // (N_KV * n_tiles), (u // n_tiles) % N_KV, u % n_tiles

    def q_map(t):
        b, g, i = unit(jnp.minimum(t, n_units - 1))
        return b, i, g

    def k_map(t):
        b, g, _ = unit(jnp.minimum(t, n_units - 1))
        return b, 0, g

    def v_map(t):
        b, g, _ = unit(jnp.maximum(t - 1, 0))
        return b, 0, g

    def o_map(t):
        b, g, i = unit(jnp.maximum(t - 1, 0))
        return b, i, g

    return pl.pallas_call(
        _attn_lat_kernel,
        grid=(n_units + 1,),
        in_specs=[
            pl.BlockSpec((1, ATTN_TILE, width), q_map),
            pl.BlockSpec((1, NTOK, HEAD_DIM), k_map),
            pl.BlockSpec((1, NTOK, V_EXT), v_map),
        ],
        out_specs=pl.BlockSpec((1, ATTN_TILE, width), o_map),
        out_shape=jax.ShapeDtypeStruct((BATCH, SEQ, C_WIDTH), BF16),
        scratch_shapes=[pltpu.VMEM((Q_PER_KV, ATTN_TILE, NTOK), BF16)],
        compiler_params=_params("arbitrary"),
        name="attention_latent",
    )(q, k, v)


def _attn_ctx_kernel(q_ref, k_ref, v_ref, o_ref):
    for h in range(N_Q):
        g = h // Q_PER_KV
        p = _softmax_weights(q_ref[0, :, h * HEAD_DIM:(h + 1) * HEAD_DIM],
                             k_ref[0, :, g * HEAD_DIM:(g + 1) * HEAD_DIM])
        o_ref[0, :, h * HEAD_DIM:(h + 1) * HEAD_DIM] = _weighted_values(
            p, v_ref[0, :, g * V_EXT:(g + 1) * V_EXT])


def _attention_context(q, k, v):
    spec = lambda w: pl.BlockSpec((1, CTX_LEN, w), lambda b: (b, CTX_TILE, 0))
    return pl.pallas_call(
        _attn_ctx_kernel,
        grid=(BATCH,),
        in_specs=[spec(C_WIDTH), spec(KV_WIDTH), spec(N_KV * V_EXT)],
        out_specs=pl.BlockSpec((1, CTX_LEN, C_WIDTH), lambda b: (b, 0, 0)),
        out_shape=jax.ShapeDtypeStruct((BATCH, CTX_LEN, C_WIDTH), BF16),
        compiler_params=_params("parallel"),
        name="attention_context",
    )(q, k, v)


def _odd_out_kernel(*refs, final):
    if final:
        o_ref, gate_ref, x_ref, mod_ref, wout_ref, fg_ref, out_ref = refs
        attn = lambda piece: o_ref[piece * ROW_TILE:(piece + 1) * ROW_TILE, :]
    else:
        o_refs, (gate_ref, x_ref, mod_ref, wout_ref, fg_ref, out_ref) = refs[:2 * PIECES], refs[2 * PIECES:]

        def attn(piece):
            is_ctx = (pl.program_id(0) * PIECES + piece) % N_TILES == CTX_TILE
            return jnp.where(is_ctx, o_refs[2 * piece + 1][...], o_refs[2 * piece][...])

    for piece in range(PIECES):
        rows = slice(piece * ROW_TILE, (piece + 1) * ROW_TILE)
        g = gate_ref[rows, :].astype(F32)
        a = (attn(piece).astype(F32) * (g * jax.nn.sigmoid(g))).astype(BF16)
        gt = mod_ref[piece % mod_ref.shape[0]][:, 2 * D_MODEL:]
        x = x_ref[rows, :] + gt * jnp.dot(a, wout_ref[...], preferred_element_type=F32)
        if final:
            x = (x * lax.rsqrt(jnp.mean(x * x, axis=-1, keepdims=True) + EPS)) * fg_ref[...]
        out_ref[rows, :] = x


def _odd_out(o_lat, o_ctx, gate, xc, mod_p, w_out, final_g):
    o_args, o_specs = [], []
    for p in range(PIECES):
        sample = lambda t, p=p: (t * PIECES + p) // N_TILES
        lat_tile = lambda t, p=p: jnp.minimum((t * PIECES + p) % N_TILES, LAT_TILES - 1)
        o_args += [o_lat, o_ctx]
        o_specs += [
            pl.BlockSpec((None, ROW_TILE, C_WIDTH), lambda t, s=sample, i=lat_tile: (s(t), i(t), 0)),
            pl.BlockSpec((None, CTX_LEN, C_WIDTH), lambda t, s=sample: (s(t), 0, 0)),
        ]
    return pl.pallas_call(
        functools.partial(_odd_out_kernel, final=False),
        grid=(N_BIG,),
        in_specs=[*o_specs, _tile(C_WIDTH), _tile(D_MODEL), _MOD_SPEC,
                  _full(D_MODEL, D_MODEL), _full(1, D_MODEL)],
        out_specs=_tile(D_MODEL),
        out_shape=jax.ShapeDtypeStruct((BATCH * NTOK, D_MODEL), F32),
        compiler_params=_params("parallel"),
        name="odd_out",
    )(*o_args, gate, xc, mod_p, w_out, final_g)


def _odd_out_final(o, gate, xc, mod_l, w_out, final_g):
    tile = lambda w: pl.BlockSpec((None, BIG_TILE, w), lambda b, j: (b, j, 0))
    full = lambda *s: pl.BlockSpec(s, lambda b, j: (0,) * len(s))
    per_sample = lambda a, w: a.reshape(BATCH, NTOK, w)
    return pl.pallas_call(
        functools.partial(_odd_out_kernel, final=True),
        grid=(BATCH, SEQ // BIG_TILE),
        in_specs=[tile(C_WIDTH), tile(C_WIDTH), tile(D_MODEL),
                  pl.BlockSpec((1, 1, 3 * D_MODEL), lambda b, j: (b, 0, 0)),
                  full(D_MODEL, D_MODEL), full(1, D_MODEL)],
        out_specs=tile(D_MODEL),
        out_shape=jax.ShapeDtypeStruct((BATCH, SEQ, D_MODEL), F32),
        compiler_params=_params("parallel", "parallel"),
        name="odd_out_final",
    )(o, per_sample(gate, C_WIDTH), per_sample(xc, D_MODEL), mod_l, w_out, final_g)


def kernel(x, c, ctx, c_ctx, norm_g, w_mod, b_mod, we_in, we_out, gm_v_g, gm_w_s, gm_b_s,
           s5_lam_re, s5_lam_im, s5_log_dt, s5_b_re, s5_b_im, s5_c_re, s5_c_im, s5_d,
           s5_w_glu, s5_b_glu, wo_in, wo_out, q_norm_g, k_norm_g, final_g):
    cond = jnp.concatenate([c, c_ctx[None], jnp.zeros((MOD_ROWS - BATCH - 1, D_MODEL), F32)], axis=0)
    mods = _modulation(cond, w_mod, b_mod).reshape(DEPTH, MOD_ROWS, 1, 3 * D_MODEL)
    xc = jnp.concatenate([x, ctx], axis=1).reshape(BATCH * NTOK, D_MODEL)
    piece_rows = np.where(np.arange(N_TILES)[None, :] == CTX_TILE, CTX_ROW, np.arange(BATCH)[:, None]).reshape(-1)
    cos, sin = _rope_tables()
    perm = _chunk_perm()
    row = lambda a: a.reshape(1, -1)
    per_sample = lambda a: a.reshape(BATCH, NTOK, a.shape[-1])

    for layer in range(DEPTH):
        i = layer // 2
        mod_p = mods[layer][piece_rows]
        if layer % 2 == 0:
            b_full = jnp.repeat(gm_b_s[i].T, A_GROUP_W, axis=1)
            ya, xs, gb, xg = _even_in(xc, mod_p, row(norm_g[layer]), we_in[i].astype(BF16),
                                      row(gm_v_g[i]), gm_w_s[i].astype(BF16), b_full, perm)
            prow, pmat = _s5_params(s5_lam_re[i], s5_lam_im[i], s5_log_dt[i], s5_b_re[i],
                                    s5_b_im[i], s5_c_re[i], s5_c_im[i])
            yg = _s5(xg, prow, pmat)
            xc = _even_out(ya, xs, gb, yg, xc, mod_p, row(s5_d[i]), s5_w_glu[i].astype(BF16),
                           row(s5_b_glu[i]), we_out[i].astype(BF16), perm)
        else:
            w_in, q_g, k_g = _permute_heads(wo_in[i], q_norm_g[i], k_norm_g[i])
            q, k, v, gate = _odd_in(xc, mod_p, row(norm_g[layer]), w_in.astype(BF16),
                                    row(q_g), row(k_g), cos, sin)
            q, k, v = per_sample(q), per_sample(k), per_sample(v)
            o = _attention_latent(q, k, v)
            if layer == DEPTH - 1:
                return _odd_out_final(o, gate, xc, mods[layer], wo_out[i].astype(BF16), row(final_g))
            xc = _odd_out(o, _attention_context(q, k, v), gate, xc, mod_p, wo_out[i].astype(BF16),
                          row(final_g))
```

```python
import functools
import math

import jax
import jax.numpy as jnp
import numpy as np
from jax import lax
from jax.experimental import pallas as pl
from jax.experimental.pallas import tpu as pltpu

F32 = jnp.float32
BF16 = jnp.bfloat16
HIGHEST = lax.Precision.HIGHEST

D_MODEL = 1024
BATCH = 4
SEQ = 4096
DEPTH = 4
GRID_W = 64
CTX_LEN = 256
EPS = 1e-6
NTOK = CTX_LEN + SEQ

LANES = 128
SUBLANES = 8
ROW_TILE = 256
N_TILES = NTOK // ROW_TILE
CTX_TILE = N_TILES - 1
LAT_TILES = N_TILES - 1
BIG_TILE = 1024
PIECES = BIG_TILE // ROW_TILE
N_BIG = BATCH * NTOK // BIG_TILE
assert N_BIG * BIG_TILE == BATCH * NTOK and SEQ % BIG_TILE == 0
VMEM_LIMIT = 56 * 1024 * 1024

CHUNK = 128
A_WIDTH = D_MODEL // 2
A_GROUPS = 4
A_GROUP_W = A_WIDTH // A_GROUPS
B_WIDTH = D_MODEL // 2
B_CH = 16
B_GROUPS = B_WIDTH // B_CH
B_STATE = 64
EVEN_IN = 3 * A_WIDTH + 2 * B_WIDTH
HEAD_DIM = 128
N_Q = D_MODEL // HEAD_DIM
N_KV = 2
Q_PER_KV = N_Q // N_KV
C_WIDTH = N_Q * HEAD_DIM
KV_WIDTH = N_KV * HEAD_DIM
ODD_IN = 2 * C_WIDTH + 2 * KV_WIDTH
ROPE_THETA = 10000.0
ROPE_PAIRS = HEAD_DIM // 4
Q_SCALE = HEAD_DIM ** -0.5 * math.log2(math.e)
V_EXT = 2 * HEAD_DIM
ATTN_TILE = 1024
ATTN_UNIT = 128

S5_Q = 16
S5_K = S5_Q * B_CH
N_CHUNKS = NTOK // S5_Q
S5_ROWS = BATCH * N_CHUNKS
SUB_TILE = CHUNK
SUB_CHUNKS = SUB_TILE // S5_Q
RELAYOUT_ROWS = 16 * S5_Q
BLOCKS_PER_SAMPLE = N_CHUNKS // SUBLANES
CTX_BLOCKS = CTX_LEN // S5_Q // SUBLANES
N_PAIRS = B_GROUPS // 2
S5_PAIRS_PER_STEP = 2
GROUPS_PER_VREG = LANES // B_CH
N_LAGS = 2 * S5_Q - 1

MOD_ROWS = 8
CTX_ROW = BATCH


def _params(*sem):
    return pltpu.CompilerParams(dimension_semantics=sem, vmem_limit_bytes=VMEM_LIMIT)


def _lane_block(rows):
    return lax.broadcasted_iota(jnp.int32, (rows, LANES), 1) // B_CH


def _move_blocks(pieces, blk):
    acc = None
    for dst, (src, src_blk) in enumerate(pieces):
        sh = ((dst - src_blk) % GROUPS_PER_VREG) * B_CH
        rolled = pltpu.roll(src, sh, 1) if sh else src
        acc = rolled if acc is None else jnp.where(blk == dst, rolled, acc)
    return acc


def _merge_blocks(pick, blk):
    acc = pick(0)
    for k in range(1, GROUPS_PER_VREG):
        acc = jnp.where(blk == k, pick(k), acc)
    return acc


def _block_transpose(src, blk):
    n = GROUPS_PER_VREG
    rolled = []
    for r in range(n):
        m = _merge_blocks(lambda b: src[(b + r) % n], blk)
        rolled.append(pltpu.roll(m, r * B_CH, 1) if r else m)
    return [_merge_blocks(lambda k: rolled[(k - b) % n], blk) for b in range(n)]


def _mod_kernel(cond_ref, w_ref, b_ref, o_ref):
    c = cond_ref[...]
    s = c * jax.nn.sigmoid(c)
    o_ref[0] = jnp.dot(s, w_ref[0], preferred_element_type=F32, precision=HIGHEST) + b_ref[0]


def _modulation(cond, w_mod, b_mod):
    nblk = 3 * D_MODEL // D_MODEL
    return pl.pallas_call(
        _mod_kernel,
        grid=(DEPTH, nblk),
        in_specs=[
            pl.BlockSpec((MOD_ROWS, D_MODEL), lambda l, j: (0, 0)),
            pl.BlockSpec((1, D_MODEL, D_MODEL), lambda l, j: (l, 0, j)),
            pl.BlockSpec((1, 1, D_MODEL), lambda l, j: (l, 0, j)),
        ],
        out_specs=pl.BlockSpec((1, MOD_ROWS, D_MODEL), lambda l, j: (l, 0, j)),
        out_shape=jax.ShapeDtypeStruct((DEPTH, MOD_ROWS, 3 * D_MODEL), F32),
        compiler_params=_params("parallel", "parallel"),
        name="modulation",
    )(cond, w_mod, b_mod.reshape(DEPTH, 1, 3 * D_MODEL))


def _prologue(x, mod, g):
    sh = mod[:, :D_MODEL]
    sc = mod[:, D_MODEL:2 * D_MODEL]
    y = x * lax.rsqrt(jnp.mean(x * x, axis=-1, keepdims=True) + EPS)
    return (y * g) * (1 + sc) + sh


def _chunk_perm(rows):
    r = np.arange(rows)
    chunks = rows // S5_Q
    source = (r % chunks) * S5_Q + r // chunks
    return jnp.asarray(source[:, None] == r[None, :], BF16)


def _even_in_kernel(x_ref, mod_ref, g_ref, win_ref, vg_ref, ws_ref, bs_ref, perm_ref,
                    ya_ref, xs_ref, gb_ref, xg_ref):
    blk = _lane_block(SUB_CHUNKS)
    xg_parts = []
    for sub in range(BIG_TILE // SUB_TILE):
        rows = slice(sub * SUB_TILE, (sub + 1) * SUB_TILE)
        h = _prologue(x_ref[rows, :], mod_ref[sub * SUB_TILE // ROW_TILE], g_ref[...])
        z = jnp.dot(h.astype(BF16), win_ref[...], preferred_element_type=F32)
        u = z[:, 0:A_WIDTH]
        v = z[:, A_WIDTH:2 * A_WIDTH]
        ga = z[:, 2 * A_WIDTH:3 * A_WIDTH]
        xs = z[:, 3 * A_WIDTH:3 * A_WIDTH + B_WIDTH]
        xs_ref[rows, :] = xs
        gb_ref[rows, :] = z[:, 3 * A_WIDTH + B_WIDTH:].astype(BF16)
        mu = jnp.mean(v, axis=-1, keepdims=True)
        vc = v - mu
        var = jnp.mean(vc * vc, axis=-1, keepdims=True)
        vn = ((vc * lax.rsqrt(var + EPS)) * vg_ref[...]).astype(BF16)
        mixed = jnp.concatenate(
            [jnp.dot(ws_ref[g], vn[:, g * A_GROUP_W:(g + 1) * A_GROUP_W], preferred_element_type=F32)
             for g in range(A_GROUPS)], axis=1) + bs_ref[...]
        ya_ref[rows, :] = ((u * mixed) * (ga * jax.nn.sigmoid(ga))).astype(BF16)

        r = jnp.dot(perm_ref[...], xs.astype(BF16), preferred_element_type=F32)
        parts = [[None] * (S5_K // LANES) for _ in range(B_GROUPS)]
        for col in range(B_WIDTH // LANES):
            for j in range(S5_K // LANES):
                steps = [r[(j * GROUPS_PER_VREG + k) * SUB_CHUNKS:(j * GROUPS_PER_VREG + k + 1) * SUB_CHUNKS,
                           col * LANES:(col + 1) * LANES] for k in range(GROUPS_PER_VREG)]
                for b, a in enumerate(_block_transpose(steps, blk)):
                    parts[col * GROUPS_PER_VREG + b][j] = a
        xg_parts.append([jnp.concatenate(p, axis=1) for p in parts])
        if len(xg_parts) == 2:
            first = (sub - 1) * SUB_CHUNKS
            for g in range(B_GROUPS):
                xg_ref[g, first:first + 2 * SUB_CHUNKS, :] = jnp.concatenate(
                    [xg_parts[0][g], xg_parts[1][g]], axis=0).astype(BF16)
            xg_parts = []


def _tile(w):
    return pl.BlockSpec((BIG_TILE, w), lambda t: (t, 0))


def _full(*shape):
    return pl.BlockSpec(shape, lambda t: (0,) * len(shape))


def _layer(i, *shape):
    return pl.BlockSpec((None, *shape), lambda *_: (i,) + (0,) * len(shape))


_MOD_SPEC = pl.BlockSpec((PIECES, 1, 3 * D_MODEL), lambda t: (t, 0, 0))
_GROUP_SPEC = pl.BlockSpec((B_GROUPS, BIG_TILE // S5_Q, S5_K), lambda t: (0, t, 0))


def _even_in(i, xc, mod_p, norm_g, w_in, v_g, w_s, b_full, perm):
    return pl.pallas_call(
        _even_in_kernel,
        grid=(N_BIG,),
        in_specs=[
            _tile(D_MODEL), _MOD_SPEC, _full(1, D_MODEL), _layer(i, D_MODEL, EVEN_IN), _full(1, A_WIDTH),
            _layer(i, A_GROUPS, CHUNK, CHUNK), _full(CHUNK, A_WIDTH), _full(SUB_TILE, SUB_TILE),
        ],
        out_specs=[_tile(A_WIDTH), _tile(B_WIDTH), _tile(B_WIDTH), _GROUP_SPEC],
        out_shape=[
            jax.ShapeDtypeStruct((BATCH * NTOK, A_WIDTH), BF16),
            jax.ShapeDtypeStruct((BATCH * NTOK, B_WIDTH), F32),
            jax.ShapeDtypeStruct((BATCH * NTOK, B_WIDTH), BF16),
            jax.ShapeDtypeStruct((B_GROUPS, S5_ROWS, S5_K), BF16),
        ],
        compiler_params=_params("parallel"),
        name="even_in",
    )(xc, mod_p, norm_g, w_in, v_g, w_s, b_full, perm)


def _s5_kernel(xg_ref, prow_ref, pmat_ref, yg_ref, sl_ref, bst_ref, toep_ref, cst_ref, lhs_ref):
    for n in range(S5_PAIRS_PER_STEP):
        two = pl.ds(2 * n, 2)
        _s5_pair(xg_ref.at[two], prow_ref.at[n], pmat_ref.at[n], yg_ref.at[two], sl_ref.at[n],
                 bst_ref.at[n], toep_ref.at[two], cst_ref.at[two], lhs_ref.at[n])


def _s5_pair(xg_ref, prow_ref, pmat_ref, yg_ref, sl_ref, bst_ref, toep_ref, cst_ref, lhs_ref):
    lane = lax.broadcasted_iota(jnp.int32, (S5_Q, LANES), 1)
    in_group = (lane < B_STATE, lane >= B_STATE)
    blk = lane // B_CH
    zero = jnp.zeros((S5_Q, LANES), F32)

    def cmul(ar, ai, br, bi):
        return ar * br - ai * bi, ar * bi + ai * br

    w_in, c_out, scan_consts, c_rows = [], [], [], []
    for d in range(2):
        lam_re = prow_ref[d, 0:1, :]
        lam_im = prow_ref[d, 1:2, :]
        dt = jnp.exp(prow_ref[d, 2:3, :])
        z_re, z_im = lam_re * dt, lam_im * dt
        mag = jnp.exp(z_re)
        lb_re, lb_im = mag * jnp.cos(z_im), mag * jnp.sin(z_im)
        den = lam_re * lam_re + lam_im * lam_im
        n_re, n_im = lb_re - 1.0, lb_im
        f_re = (n_re * lam_re + n_im * lam_im) / den
        f_im = (n_im * lam_re - n_re * lam_im) / den
        b_re, b_im = pmat_ref[d, 0], pmat_ref[d, 1]
        c_re, c_im = pmat_ref[d, 2], pmat_ref[d, 3]
        bb_re, bb_im = cmul(f_re, f_im, b_re, b_im)
        c_rows.append((c_re, c_im))

        def powers(steps):
            m = jnp.exp(steps * z_re)
            return m * jnp.cos(steps * z_im), m * jnp.sin(steps * z_im)

        j = lax.broadcasted_iota(jnp.int32, (3 * SUBLANES, LANES), 0).astype(F32)
        p_re, p_im = powers(j)
        row = lambda a, k: a[k:k + 1, :]
        w_in.append([cmul(bb_re, bb_im, row(p_re, k), row(p_im, k)) for k in range(S5_Q)])
        c_out.append([cmul(c_re, c_im, row(p_re, k), row(p_im, k)) for k in range(S5_Q + 1)])

        i8 = lax.broadcasted_iota(jnp.int32, (SUBLANES, LANES), 0)
        order = i8 if d == 0 else (SUBLANES - 1) - i8
        apow = powers((order * S5_Q).astype(F32))
        bc = lambda r: powers(jnp.full((SUBLANES, LANES), r * S5_Q, F32))

        def doubling(r):
            inside = i8 >= r if d == 0 else i8 < SUBLANES - r
            return tuple(jnp.where(inside, a, 0.0) for a in bc(r))

        scan_consts.append((apow, [doubling(1), doubling(2), doubling(4)], bc(SUBLANES)))

    for e in range(2):
        sel = lambda a: jnp.where(in_group[e], a, 0.0)
        for s in range(S5_Q):
            fr, fi = w_in[0][S5_Q - 1 - s]
            br, bi = w_in[1][s]
            r0 = e * S5_K + s * B_CH
            bst_ref[r0:r0 + B_CH, :] = jnp.concatenate(
                [sel(fr), sel(fi), sel(br), sel(bi)], axis=1).astype(BF16)

    for e in range(2):
        sel = lambda a: jnp.where(in_group[e], a, 0.0)
        for t in range(S5_Q):
            fr, fi = c_out[0][t + 1]
            br, bi = c_out[1][S5_Q - t]
            cst_ref[e, t * B_CH:(t + 1) * B_CH, :] = jnp.concatenate(
                [sel(fr), sel(-fi), sel(br), sel(-bi)], axis=1).astype(BF16)

    for a in range(N_LAGS + 1):
        lag = a - (S5_Q - 1)
        f = w_in[0][lag] if 0 <= lag < S5_Q else (zero, zero)
        b = w_in[1][-lag] if -S5_Q < lag <= 0 else (zero, zero)
        lhs_ref[a * B_CH:(a + 1) * B_CH, :] = jnp.concatenate([f[0], f[1], b[0], b[1]], axis=1)
    cq_rows = []
    for e in range(2):
        sel = lambda a: jnp.where(in_group[e], a, 0.0)
        (fr, fi), (br, bi) = c_rows
        cq_rows.append(jnp.concatenate([sel(fr), sel(-fi), sel(br), sel(-bi)], axis=1))
    cq = jnp.concatenate(cq_rows + [jnp.zeros((LANES - 2 * B_CH, 4 * LANES), F32)], axis=0)
    kst = lax.dot_general(lhs_ref[...], cq, (((1,), (1,)), ((), ())),
                          preferred_element_type=F32, precision=HIGHEST)
    for e in range(2):
        g_cols = []
        for col in range((N_LAGS + 1) // GROUPS_PER_VREG):
            pieces = [(kst[(col * GROUPS_PER_VREG + k) * B_CH:(col * GROUPS_PER_VREG + k + 1) * B_CH, :], e)
                      for k in range(GROUPS_PER_VREG)]
            g_cols.append(_move_blocks(pieces, blk))
        for s in range(S5_Q):
            c0, sh = divmod((S5_Q - 1 - s) * B_CH, LANES)
            if sh == 0:
                cols = g_cols[c0:c0 + 2]
            else:
                rolled = [pltpu.roll(g_cols[c0 + k], LANES - sh, 1) for k in range(3)]
                cols = [jnp.where(lane < LANES - sh, rolled[k], rolled[k + 1]) for k in range(2)]
            toep_ref[e, s * B_CH:(s + 1) * B_CH, :] = jnp.concatenate(cols, axis=1).astype(BF16)

    x0 = xg_ref[0]
    x1 = xg_ref[1]
    sl_ref[...] = jnp.dot(jnp.concatenate([x0, x1], axis=1), bst_ref[...], preferred_element_type=F32)

    i8 = lax.broadcasted_iota(jnp.int32, (SUBLANES, LANES), 0)

    def scan_block(p, carry, consts, fwd):
        (ap_re, ap_im), doubling, (a8_re, a8_im) = consts
        p_re, p_im = p

        def earlier(a, r):
            return pltpu.roll(a, r if fwd else SUBLANES - r, 0)

        for r, (a_re, a_im) in zip((1, 2, 4), doubling):
            m_re, m_im = cmul(a_re, a_im, earlier(p_re, r), earlier(p_im, r))
            p_re, p_im = p_re + m_re, p_im + m_im
        c_re, c_im = carry
        e_re, e_im = cmul(ap_re, ap_im, c_re, c_im)
        first = i8 == (0 if fwd else SUBLANES - 1)
        entering = (jnp.where(first, e_re, earlier(p_re, 1) + e_re),
                    jnp.where(first, e_im, earlier(p_im, 1) + e_im))
        last = SUBLANES - 1 if fwd else 0
        n_re, n_im = cmul(a8_re, a8_im, c_re, c_im)
        bcast = lambda a: jnp.broadcast_to(a[last:last + 1, :], (SUBLANES, LANES))
        return entering, (bcast(p_re) + n_re, bcast(p_im) + n_im)

    z8 = jnp.zeros((SUBLANES, LANES), F32)
    carry = [[(z8, z8), (z8, z8)] for _ in range(BATCH)]
    for j in range(BLOCKS_PER_SAMPLE):
        jf = BLOCKS_PER_SAMPLE - CTX_BLOCKS + j if j < CTX_BLOCKS else j - CTX_BLOCKS
        jb = BLOCKS_PER_SAMPLE - 1 - j
        for b in range(BATCH):
            for d, jd in enumerate((jf, jb)):
                row0 = (b * BLOCKS_PER_SAMPLE + jd) * SUBLANES
                rows, col = slice(row0, row0 + SUBLANES), d * 2 * LANES
                p = (sl_ref[rows, col:col + LANES], sl_ref[rows, col + LANES:col + 2 * LANES])
                entering, carry[b][d] = scan_block(p, carry[b][d], scan_consts[d], d == 0)
                sl_ref[rows, col:col + LANES] = entering[0]
                sl_ref[rows, col + LANES:col + 2 * LANES] = entering[1]

    sp = sl_ref[...].astype(BF16)
    nt = (((1,), (1,)), ((), ()))
    yg_ref[0] = (jnp.dot(x0, toep_ref[0], preferred_element_type=F32)
                 + lax.dot_general(sp, cst_ref[0], nt, preferred_element_type=F32)).astype(BF16)
    yg_ref[1] = (jnp.dot(x1, toep_ref[1], preferred_element_type=F32)
                 + lax.dot_general(sp, cst_ref[1], nt, preferred_element_type=F32)).astype(BF16)


def _s5_params(lam_re, lam_im, log_dt, b_re, b_im, c_re, c_im):
    def rows(a):
        return a.astype(F32).reshape(2, N_PAIRS, 1, 2 * B_STATE).transpose(1, 0, 2, 3)

    dt = jnp.broadcast_to(log_dt[..., None], lam_re.shape)
    pad = jnp.zeros((N_PAIRS, 2, SUBLANES - 3, LANES), F32)
    prow = jnp.concatenate([rows(lam_re), rows(lam_im), rows(dt), pad], axis=2)

    def mats(a, channel_axis):
        a = a.astype(F32)
        if channel_axis == 3:
            a = a.transpose(0, 1, 3, 2)
        a = a.reshape(2, N_PAIRS, 2, B_CH, B_STATE).transpose(1, 0, 3, 2, 4)
        return a.reshape(N_PAIRS, 2, B_CH, 2 * B_STATE)

    pmat = jnp.stack([mats(b_re, 3), mats(b_im, 3), mats(c_re, 2), mats(c_im, 2)], axis=2)
    return prow, pmat


def _s5(xg, prow, pmat):
    n = S5_PAIRS_PER_STEP
    return pl.pallas_call(
        _s5_kernel,
        grid=(N_PAIRS // S5_PAIRS_PER_STEP,),
        in_specs=[
            pl.BlockSpec((2 * n, S5_ROWS, S5_K), lambda q: (q, 0, 0)),
            pl.BlockSpec((n, 2, SUBLANES, LANES), lambda q: (q, 0, 0, 0)),
            pl.BlockSpec((n, 2, 4, B_CH, LANES), lambda q: (q, 0, 0, 0, 0)),
        ],
        out_specs=pl.BlockSpec((2 * n, S5_ROWS, S5_K), lambda q: (q, 0, 0)),
        out_shape=jax.ShapeDtypeStruct((B_GROUPS, S5_ROWS, S5_K), BF16),
        scratch_shapes=[
            pltpu.VMEM((n, S5_ROWS, 4 * LANES), F32),
            pltpu.VMEM((n, 2 * S5_K, 4 * LANES), BF16),
            pltpu.VMEM((2 * n, S5_K, S5_K), BF16),
            pltpu.VMEM((2 * n, S5_K, 4 * LANES), BF16),
            pltpu.VMEM((n, (N_LAGS + 1) * B_CH, 4 * LANES), F32),
        ],
        compiler_params=_params("parallel"),
        name="s5_chunked",
    )(xg, prow, pmat)


def _even_out_kernel(ya_ref, xs_ref, gb_ref, yg_ref, x_ref, mod_ref, d_ref, wglu_ref, bglu_ref,
                     wout_ref, perm_ref, o_ref):
    blk = _lane_block(S5_Q)
    for sub in range(BIG_TILE // RELAYOUT_ROWS):
        rows = slice(sub * RELAYOUT_ROWS, (sub + 1) * RELAYOUT_ROWS)
        chunks = slice(sub * S5_Q, (sub + 1) * S5_Q)
        gt = mod_ref[sub * RELAYOUT_ROWS // ROW_TILE][:, 2 * D_MODEL:]
        steps = [[None] * (B_WIDTH // LANES) for _ in range(S5_Q)]
        for col in range(B_WIDTH // LANES):
            for j in range(S5_K // LANES):
                groups = [yg_ref[col * GROUPS_PER_VREG + k, chunks, j * LANES:(j + 1) * LANES].astype(F32)
                          for k in range(GROUPS_PER_VREG)]
                for b, a in enumerate(_block_transpose(groups, blk)):
                    steps[j * GROUPS_PER_VREG + b][col] = a
        ys = jnp.dot(perm_ref[...],
                     jnp.concatenate([jnp.concatenate(s, axis=1) for s in steps], axis=0).astype(BF16),
                     preferred_element_type=F32)

        y = ys + d_ref[...] * xs_ref[rows, :]
        y = jax.nn.gelu(y)
        t = jnp.dot(y.astype(BF16), wglu_ref[...], preferred_element_type=F32) + bglu_ref[...]
        y = y * jax.nn.sigmoid(t)
        gb = gb_ref[rows, :].astype(F32)
        yb = (y * (gb * jax.nn.sigmoid(gb))).astype(BF16)
        mix = (jnp.dot(ya_ref[rows, :], wout_ref[0:A_WIDTH, :], preferred_element_type=F32)
               + jnp.dot(yb, wout_ref[A_WIDTH:, :], preferred_element_type=F32))
        o_ref[rows, :] = x_ref[rows, :] + gt * mix


def _even_out(i, ya, xs, gb, yg, xc, mod_p, d_skip, w_glu, b_glu, w_out, perm):
    return pl.pallas_call(
        _even_out_kernel,
        grid=(N_BIG,),
        in_specs=[
            _tile(A_WIDTH), _tile(B_WIDTH), _tile(B_WIDTH), _GROUP_SPEC, _tile(D_MODEL), _MOD_SPEC,
            _full(1, B_WIDTH), _layer(i, B_WIDTH, B_WIDTH), _full(1, B_WIDTH), _layer(i, D_MODEL, D_MODEL),
            _full(RELAYOUT_ROWS, RELAYOUT_ROWS),
        ],
        out_specs=_tile(D_MODEL),
        out_shape=jax.ShapeDtypeStruct((BATCH * NTOK, D_MODEL), F32),
        compiler_params=_params("parallel"),
        name="even_out",
    )(ya, xs, gb, yg, xc, mod_p, d_skip, w_glu, b_glu, w_out, perm)


_HEAD_ORDER = np.concatenate([np.arange(0, 32), np.arange(64, 96), np.arange(32, 64), np.arange(96, 128)])


def _rope_tables():
    t = np.arange(SEQ)
    freqs = ROPE_THETA ** (-np.arange(ROPE_PAIRS, dtype=np.float64) / ROPE_PAIRS)
    ang = np.concatenate([(t // GRID_W)[:, None] * freqs, (t % GRID_W)[:, None] * freqs], axis=1)
    cos = np.concatenate([np.cos(ang), np.cos(ang)], axis=1)
    sin = np.concatenate([-np.sin(ang), np.sin(ang)], axis=1)
    pad = lambda a, v: np.concatenate([a, np.full((CTX_LEN, HEAD_DIM), v)], axis=0).astype(np.float32)
    return jnp.asarray(pad(cos, 1.0)), jnp.asarray(pad(sin, 0.0))


def _permute_heads(w_in, q_g, k_g):
    n_heads = N_Q + N_KV
    cols = (np.arange(n_heads)[:, None] * HEAD_DIM + _HEAD_ORDER[None, :]).reshape(-1)
    cols = np.concatenate([cols, np.arange(n_heads * HEAD_DIM, ODD_IN)])
    return w_in[..., cols], q_g[..., _HEAD_ORDER], k_g[..., _HEAD_ORDER]


def _odd_in_kernel(x_ref, mod_ref, g_ref, win_ref, qg_ref, kg_ref, *refs):
    tabs, (q_ref, k_ref, v_ref, gate_ref) = refs[:2 * PIECES], refs[2 * PIECES:]
    ones = jnp.ones((ROW_TILE, HEAD_DIM), BF16)
    for piece in range(PIECES):
        rows = slice(piece * ROW_TILE, (piece + 1) * ROW_TILE)
        h = _prologue(x_ref[rows, :], mod_ref[piece], g_ref[...])
        z = jnp.dot(h.astype(BF16), win_ref[...], preferred_element_type=F32)
        cos = tabs[2 * piece][...]
        sin = tabs[2 * piece + 1][...]

        def norm_rope(xh, gain, scale):
            xn = (xh * lax.rsqrt(jnp.mean(xh * xh, axis=-1, keepdims=True) + EPS)) * gain
            out = xn * cos + pltpu.roll(xn, HEAD_DIM // 2, 1) * sin
            return (out * scale).astype(BF16) if scale is not None else out.astype(BF16)

        for hq in range(N_Q):
            q_ref[rows, hq * HEAD_DIM:(hq + 1) * HEAD_DIM] = norm_rope(
                z[:, hq * HEAD_DIM:(hq + 1) * HEAD_DIM], qg_ref[...], Q_SCALE)
        for hk in range(N_KV):
            lo = C_WIDTH + hk * HEAD_DIM
            k_ref[rows, hk * HEAD_DIM:(hk + 1) * HEAD_DIM] = norm_rope(
                z[:, lo:lo + HEAD_DIM], kg_ref[...], None)
            lo = C_WIDTH + KV_WIDTH + hk * HEAD_DIM
            v_ref[rows, hk * V_EXT:hk * V_EXT + HEAD_DIM] = z[:, lo:lo + HEAD_DIM].astype(BF16)
            v_ref[rows, hk * V_EXT + HEAD_DIM:(hk + 1) * V_EXT] = ones
        gate_ref[rows, :] = z[:, C_WIDTH + 2 * KV_WIDTH:].astype(BF16)


def _odd_in(i, xc, mod_p, norm_g, w_in, q_g, k_g, cos, sin):
    tabs, tab_specs = [], []
    for p in range(PIECES):
        spec = pl.BlockSpec((ROW_TILE, HEAD_DIM), lambda t, p=p: ((t * PIECES + p) % N_TILES, 0))
        tabs += [cos, sin]
        tab_specs += [spec, spec]
    return pl.pallas_call(
        _odd_in_kernel,
        grid=(N_BIG,),
        in_specs=[
            _tile(D_MODEL), _MOD_SPEC, _full(1, D_MODEL), _layer(i, D_MODEL, ODD_IN),
            _full(1, HEAD_DIM), _full(1, HEAD_DIM), *tab_specs,
        ],
        out_specs=[_tile(C_WIDTH), _tile(KV_WIDTH), _tile(N_KV * V_EXT), _tile(C_WIDTH)],
        out_shape=[
            jax.ShapeDtypeStruct((BATCH * NTOK, C_WIDTH), BF16),
            jax.ShapeDtypeStruct((BATCH * NTOK, KV_WIDTH), BF16),
            jax.ShapeDtypeStruct((BATCH * NTOK, N_KV * V_EXT), BF16),
            jax.ShapeDtypeStruct((BATCH * NTOK, C_WIDTH), BF16),
        ],
        compiler_params=_params("parallel"),
        name="odd_in",
    )(xc, mod_p, norm_g, w_in, q_g, k_g, *tabs)


_NT = (((1,), (1,)), ((), ()))


def _softmax_weights(q, k):
    s = lax.dot_general(q, k, _NT, preferred_element_type=F32)
    return jnp.exp2(s - jnp.max(s, axis=-1, keepdims=True)).astype(BF16)


def _weighted_values(p, v):
    ov = jnp.dot(p, v, preferred_element_type=F32)
    return (ov[:, :HEAD_DIM] / ov[:, HEAD_DIM:HEAD_DIM + 1]).astype(BF16)


def _attn_lat_kernel(q_ref, k_ref, v_ref, o_ref, p_ref):
    @pl.when(pl.program_id(0) == 0)
    def _():
        p_ref[...] = jnp.ones(p_ref.shape, BF16)

    k = k_ref[0]
    v = v_ref[0]
    for h in range(Q_PER_KV):
        cols = slice(h * HEAD_DIM, (h + 1) * HEAD_DIM)
        for r in range(ATTN_TILE // ATTN_UNIT):
            rows = slice(r * ATTN_UNIT, (r + 1) * ATTN_UNIT)
            p_new = _softmax_weights(q_ref[0, rows, cols], k)
            o_ref[0, rows, cols] = _weighted_values(p_ref[h, rows, :], v)
            p_ref[h, rows, :] = p_new


def _attention_latent(q, k, v):
    n_tiles = SEQ // ATTN_TILE
    n_units = BATCH * N_KV * n_tiles
    width = Q_PER_KV * HEAD_DIM

    def unit(u):
        return u // (N_KV * n_tiles), (u // n_tiles) % N_KV, u % n_tiles

    def q_map(t):
        b, g, i = unit(jnp.minimum(t, n_units - 1))
        return b, i, g

    def k_map(t):
        b, g, _ = unit(jnp.minimum(t, n_units - 1))
        return b, 0, g

    def v_map(t):
        b, g, _ = unit(jnp.maximum(t - 1, 0))
        return b, 0, g

    def o_map(t):
        b, g, i = unit(jnp.maximum(t - 1, 0))
        return b, i, g

    return pl.pallas_call(
        _attn_lat_kernel,
        grid=(n_units + 1,),
        in_specs=[
            pl.BlockSpec((1, ATTN_TILE, width), q_map),
            pl.BlockSpec((1, NTOK, HEAD_DIM), k_map),
            pl.BlockSpec((1, NTOK, V_EXT), v_map),
        ],
        out_specs=pl.BlockSpec((1, ATTN_TILE, width), o_map),
        out_shape=jax.ShapeDtypeStruct((BATCH, SEQ, C_WIDTH), BF16),
        scratch_shapes=[pltpu.VMEM((Q_PER_KV, ATTN_TILE, NTOK), BF16)],
        compiler_params=_params("arbitrary"),
        name="attention_latent",
    )(q, k, v)


def _attn_ctx_kernel(q_ref, k_ref, v_ref, o_ref):
    for h in range(N_Q):
        g = h // Q_PER_KV
        p = _softmax_weights(q_ref[0, :, h * HEAD_DIM:(h + 1) * HEAD_DIM],
                             k_ref[0, :, g * HEAD_DIM:(g + 1) * HEAD_DIM])
        o_ref[0, :, h * HEAD_DIM:(h + 1) * HEAD_DIM] = _weighted_values(
            p, v_ref[0, :, g * V_EXT:(g + 1) * V_EXT])


def _attention_context(q, k, v):
    spec = lambda w: pl.BlockSpec((1, CTX_LEN, w), lambda b: (b, CTX_TILE, 0))
    return pl.pallas_call(
        _attn_ctx_kernel,
        grid=(BATCH,),
        in_specs=[spec(C_WIDTH), spec(KV_WIDTH), spec(N_KV * V_EXT)],
        out_specs=pl.BlockSpec((1, CTX_LEN, C_WIDTH), lambda b: (b, 0, 0)),
        out_shape=jax.ShapeDtypeStruct((BATCH, CTX_LEN, C_WIDTH), BF16),
        compiler_params=_params("parallel"),
        name="attention_context",
    )(q, k, v)


def _odd_out_kernel(*refs, final):
    if final:
        o_ref, gate_ref, x_ref, mod_ref, wout_ref, fg_ref, out_ref = refs
        attn = lambda piece: o_ref[piece * ROW_TILE:(piece + 1) * ROW_TILE, :]
    else:
        o_refs, (gate_ref, x_ref, mod_ref, wout_ref, fg_ref, out_ref) = refs[:2 * PIECES], refs[2 * PIECES:]

        def attn(piece):
            is_ctx = (pl.program_id(0) * PIECES + piece) % N_TILES == CTX_TILE
            return jnp.where(is_ctx, o_refs[2 * piece + 1][...], o_refs[2 * piece][...])

    for piece in range(PIECES):
        rows = slice(piece * ROW_TILE, (piece + 1) * ROW_TILE)
        g = gate_ref[rows, :].astype(F32)
        a = (attn(piece).astype(F32) * (g * jax.nn.sigmoid(g))).astype(BF16)
        gt = mod_ref[piece % mod_ref.shape[0]][:, 2 * D_MODEL:]
        x = x_ref[rows, :] + gt * jnp.dot(a, wout_ref[...], preferred_element_type=F32)
        if final:
            x = (x * lax.rsqrt(jnp.mean(x * x, axis=-1, keepdims=True) + EPS)) * fg_ref[...]
        out_ref[rows, :] = x


def _odd_out(i, o_lat, o_ctx, gate, xc, mod_p, w_out, final_g):
    o_args, o_specs = [], []
    for p in range(PIECES):
        sample = lambda t, p=p: (t * PIECES + p) // N_TILES
        lat_tile = lambda t, p=p: jnp.minimum((t * PIECES + p) % N_TILES, LAT_TILES - 1)
        o_args += [o_lat, o_ctx]
        o_specs += [
            pl.BlockSpec((None, ROW_TILE, C_WIDTH), lambda t, s=sample, i=lat_tile: (s(t), i(t), 0)),
            pl.BlockSpec((None, CTX_LEN, C_WIDTH), lambda t, s=sample: (s(t), 0, 0)),
        ]
    return pl.pallas_call(
        functools.partial(_odd_out_kernel, final=False),
        grid=(N_BIG,),
        in_specs=[*o_specs, _tile(C_WIDTH), _tile(D_MODEL), _MOD_SPEC,
                  _layer(i, D_MODEL, D_MODEL), _full(1, D_MODEL)],
        out_specs=_tile(D_MODEL),
        out_shape=jax.ShapeDtypeStruct((BATCH * NTOK, D_MODEL), F32),
        compiler_params=_params("parallel"),
        name="odd_out",
    )(*o_args, gate, xc, mod_p, w_out, final_g)


def _odd_out_final(i, o, gate, xc, mod_l, w_out, final_g):
    tile = lambda w: pl.BlockSpec((None, BIG_TILE, w), lambda b, j: (b, j, 0))
    full = lambda *s: pl.BlockSpec(s, lambda b, j: (0,) * len(s))
    per_sample = lambda a, w: a.reshape(BATCH, NTOK, w)
    return pl.pallas_call(
        functools.partial(_odd_out_kernel, final=True),
        grid=(BATCH, SEQ // BIG_TILE),
        in_specs=[tile(C_WIDTH), tile(C_WIDTH), tile(D_MODEL),
                  pl.BlockSpec((1, 1, 3 * D_MODEL), lambda b, j: (b, 0, 0)),
                  _layer(i, D_MODEL, D_MODEL), full(1, D_MODEL)],
        out_specs=tile(D_MODEL),
        out_shape=jax.ShapeDtypeStruct((BATCH, SEQ, D_MODEL), F32),
        compiler_params=_params("parallel", "parallel"),
        name="odd_out_final",
    )(o, per_sample(gate, C_WIDTH), per_sample(xc, D_MODEL), mod_l, w_out, final_g)


def kernel(x, c, ctx, c_ctx, norm_g, w_mod, b_mod, we_in, we_out, gm_v_g, gm_w_s, gm_b_s,
           s5_lam_re, s5_lam_im, s5_log_dt, s5_b_re, s5_b_im, s5_c_re, s5_c_im, s5_d,
           s5_w_glu, s5_b_glu, wo_in, wo_out, q_norm_g, k_norm_g, final_g):
    cond = jnp.concatenate([c, c_ctx[None], jnp.zeros((MOD_ROWS - BATCH - 1, D_MODEL), F32)], axis=0)
    mods = _modulation(cond, w_mod, b_mod).reshape(DEPTH, MOD_ROWS, 1, 3 * D_MODEL)
    xc = jnp.concatenate([x, ctx], axis=1).reshape(BATCH * NTOK, D_MODEL)
    piece_rows = np.where(np.arange(N_TILES)[None, :] == CTX_TILE, CTX_ROW, np.arange(BATCH)[:, None]).reshape(-1)
    cos, sin = _rope_tables()
    perm_in = _chunk_perm(SUB_TILE)
    perm_out = _chunk_perm(RELAYOUT_ROWS).T
    row = lambda a: a.reshape(1, -1)
    per_sample = lambda a: a.reshape(BATCH, NTOK, a.shape[-1])
    we_in, we_out, w_glu, w_s, wo_out = (w.astype(BF16) for w in (we_in, we_out, s5_w_glu, gm_w_s, wo_out))
    wo_in, q_g, k_g = _permute_heads(wo_in, q_norm_g, k_norm_g)
    wo_in = wo_in.astype(BF16)

    for layer in range(DEPTH):
        i = layer // 2
        mod_p = mods[layer][piece_rows]
        if layer % 2 == 0:
            b_full = jnp.repeat(gm_b_s[i].T, A_GROUP_W, axis=1)
            ya, xs, gb, xg = _even_in(i, xc, mod_p, row(norm_g[layer]), we_in, row(gm_v_g[i]), w_s,
                                      b_full, perm_in)
            prow, pmat = _s5_params(s5_lam_re[i], s5_lam_im[i], s5_log_dt[i], s5_b_re[i],
                                    s5_b_im[i], s5_c_re[i], s5_c_im[i])
            yg = _s5(xg, prow, pmat)
            xc = _even_out(i, ya, xs, gb, yg, xc, mod_p, row(s5_d[i]), w_glu, row(s5_b_glu[i]), we_out,
                           perm_out)
        else:
            q, k, v, gate = _odd_in(i, xc, mod_p, row(norm_g[layer]), wo_in, row(q_g[i]), row(k_g[i]),
                                    cos, sin)
            q, k, v = per_sample(q), per_sample(k), per_sample(v)
            o = _attention_latent(q, k, v)
            if layer == DEPTH - 1:
                return _odd_out_final(i, o, gate, xc, mods[layer], wo_out, row(final_g))
            xc = _odd_out(i, o, _attention_context(q, k, v), gate, xc, mod_p, wo_out, row(final_g))
```

```python
import functools
import math

import jax
import jax.numpy as jnp
import numpy as np
from jax import lax
from jax.experimental import pallas as pl
from jax.experimental.pallas import tpu as pltpu

F32 = jnp.float32
BF16 = jnp.bfloat16
HIGHEST = lax.Precision.HIGHEST

D_MODEL = 1024
BATCH = 4
SEQ = 4096
DEPTH = 4
GRID_W = 64
CTX_LEN = 256
EPS = 1e-6
NTOK = CTX_LEN + SEQ

LANES = 128
SUBLANES = 8
ROW_TILE = 256
N_TILES = NTOK // ROW_TILE
CTX_TILE = N_TILES - 1
LAT_TILES = N_TILES - 1
BIG_TILE = 1024
PIECES = BIG_TILE // ROW_TILE
N_BIG = BATCH * NTOK // BIG_TILE
assert N_BIG * BIG_TILE == BATCH * NTOK and SEQ % BIG_TILE == 0
VMEM_LIMIT = 56 * 1024 * 1024

CHUNK = 128
A_WIDTH = D_MODEL // 2
A_GROUPS = 4
A_GROUP_W = A_WIDTH // A_GROUPS
B_WIDTH = D_MODEL // 2
B_CH = 16
B_GROUPS = B_WIDTH // B_CH
B_STATE = 64
EVEN_IN = 3 * A_WIDTH + 2 * B_WIDTH
HEAD_DIM = 128
N_Q = D_MODEL // HEAD_DIM
N_KV = 2
Q_PER_KV = N_Q // N_KV
C_WIDTH = N_Q * HEAD_DIM
KV_WIDTH = N_KV * HEAD_DIM
ODD_IN = 2 * C_WIDTH + 2 * KV_WIDTH
ROPE_THETA = 10000.0
ROPE_PAIRS = HEAD_DIM // 4
Q_SCALE = HEAD_DIM ** -0.5 * math.log2(math.e)
V_EXT = 2 * HEAD_DIM
ATTN_TILE = 1024
ATTN_UNIT = 128

S5_Q = 16
S5_K = S5_Q * B_CH
N_CHUNKS = NTOK // S5_Q
S5_ROWS = BATCH * N_CHUNKS
SUB_TILE = CHUNK
SUB_CHUNKS = SUB_TILE // S5_Q
RELAYOUT_ROWS = 16 * S5_Q
BLOCKS_PER_SAMPLE = N_CHUNKS // SUBLANES
CTX_BLOCKS = CTX_LEN // S5_Q // SUBLANES
N_PAIRS = B_GROUPS // 2
S5_PAIRS_PER_STEP = 2
GROUPS_PER_VREG = LANES // B_CH
N_LAGS = 2 * S5_Q - 1

MOD_ROWS = 8
CTX_ROW = BATCH


def _params(*sem):
    return pltpu.CompilerParams(dimension_semantics=sem, vmem_limit_bytes=VMEM_LIMIT)


def _lane_block(rows):
    return lax.broadcasted_iota(jnp.int32, (rows, LANES), 1) // B_CH


def _move_blocks(pieces, blk):
    acc = None
    for dst, (src, src_blk) in enumerate(pieces):
        sh = ((dst - src_blk) % GROUPS_PER_VREG) * B_CH
        rolled = pltpu.roll(src, sh, 1) if sh else src
        acc = rolled if acc is None else jnp.where(blk == dst, rolled, acc)
    return acc


def _merge_blocks(pick, blk):
    acc = pick(0)
    for k in range(1, GROUPS_PER_VREG):
        acc = jnp.where(blk == k, pick(k), acc)
    return acc


def _block_transpose(src, blk):
    n = GROUPS_PER_VREG
    rolled = []
    for r in range(n):
        m = _merge_blocks(lambda b: src[(b + r) % n], blk)
        rolled.append(pltpu.roll(m, r * B_CH, 1) if r else m)
    return [_merge_blocks(lambda k: rolled[(k - b) % n], blk) for b in range(n)]


def _mod_kernel(cond_ref, w_ref, b_ref, o_ref):
    c = cond_ref[...]
    s = c * jax.nn.sigmoid(c)
    o_ref[0] = jnp.dot(s, w_ref[0], preferred_element_type=F32, precision=HIGHEST) + b_ref[0]


def _modulation(cond, w_mod, b_mod):
    nblk = 3 * D_MODEL // D_MODEL
    return pl.pallas_call(
        _mod_kernel,
        grid=(DEPTH, nblk),
        in_specs=[
            pl.BlockSpec((MOD_ROWS, D_MODEL), lambda l, j: (0, 0)),
            pl.BlockSpec((1, D_MODEL, D_MODEL), lambda l, j: (l, 0, j)),
            pl.BlockSpec((1, 1, D_MODEL), lambda l, j: (l, 0, j)),
        ],
        out_specs=pl.BlockSpec((1, MOD_ROWS, D_MODEL), lambda l, j: (l, 0, j)),
        out_shape=jax.ShapeDtypeStruct((DEPTH, MOD_ROWS, 3 * D_MODEL), F32),
        compiler_params=_params("parallel", "parallel"),
        name="modulation",
    )(cond, w_mod, b_mod.reshape(DEPTH, 1, 3 * D_MODEL))


def _prologue(x, mod, g):
    sh = mod[:, :D_MODEL]
    sc = mod[:, D_MODEL:2 * D_MODEL]
    y = x * lax.rsqrt(jnp.mean(x * x, axis=-1, keepdims=True) + EPS)
    return (y * g) * (1 + sc) + sh


def _chunk_perm(rows):
    r = np.arange(rows)
    chunks = rows // S5_Q
    source = (r % chunks) * S5_Q + r // chunks
    return jnp.asarray(source[:, None] == r[None, :], BF16)


def _even_in_kernel(x_ref, mod_ref, g_ref, win_ref, vg_ref, ws_ref, bs_ref, perm_ref,
                    ya_ref, xs_ref, gb_ref, xg_ref):
    blk = _lane_block(SUB_CHUNKS)
    xg_parts = []
    for sub in range(BIG_TILE // SUB_TILE):
        rows = slice(sub * SUB_TILE, (sub + 1) * SUB_TILE)
        h = _prologue(x_ref[rows, :], mod_ref[sub * SUB_TILE // ROW_TILE], g_ref[...])
        z = jnp.dot(h.astype(BF16), win_ref[...], preferred_element_type=F32)
        u = z[:, 0:A_WIDTH]
        v = z[:, A_WIDTH:2 * A_WIDTH]
        ga = z[:, 2 * A_WIDTH:3 * A_WIDTH]
        xs = z[:, 3 * A_WIDTH:3 * A_WIDTH + B_WIDTH]
        xs_ref[rows, :] = xs
        gb_ref[rows, :] = z[:, 3 * A_WIDTH + B_WIDTH:].astype(BF16)
        mu = jnp.mean(v, axis=-1, keepdims=True)
        vc = v - mu
        var = jnp.mean(vc * vc, axis=-1, keepdims=True)
        vn = ((vc * lax.rsqrt(var + EPS)) * vg_ref[...]).astype(BF16)
        mixed = jnp.concatenate(
            [jnp.dot(ws_ref[g], vn[:, g * A_GROUP_W:(g + 1) * A_GROUP_W], preferred_element_type=F32)
             for g in range(A_GROUPS)], axis=1) + bs_ref[...]
        ya_ref[rows, :] = ((u * mixed) * (ga * jax.nn.sigmoid(ga))).astype(BF16)

        r = jnp.dot(perm_ref[...], xs.astype(BF16), preferred_element_type=F32)
        parts = [[None] * (S5_K // LANES) for _ in range(B_GROUPS)]
        for col in range(B_WIDTH // LANES):
            for j in range(S5_K // LANES):
                steps = [r[(j * GROUPS_PER_VREG + k) * SUB_CHUNKS:(j * GROUPS_PER_VREG + k + 1) * SUB_CHUNKS,
                           col * LANES:(col + 1) * LANES] for k in range(GROUPS_PER_VREG)]
                for b, a in enumerate(_block_transpose(steps, blk)):
                    parts[col * GROUPS_PER_VREG + b][j] = a
        xg_parts.append([jnp.concatenate(p, axis=1) for p in parts])
        if len(xg_parts) == 2:
            first = (sub - 1) * SUB_CHUNKS
            for g in range(B_GROUPS):
                xg_ref[g, first:first + 2 * SUB_CHUNKS, :] = jnp.concatenate(
                    [xg_parts[0][g], xg_parts[1][g]], axis=0).astype(BF16)
            xg_parts = []


def _tile(w):
    return pl.BlockSpec((BIG_TILE, w), lambda t: (t, 0))


def _full(*shape):
    return pl.BlockSpec(shape, lambda t: (0,) * len(shape))


def _layer(i, *shape):
    return pl.BlockSpec((None, *shape), lambda *_: (i,) + (0,) * len(shape))


_MOD_SPEC = pl.BlockSpec((PIECES, 1, 3 * D_MODEL), lambda t: (t, 0, 0))
_GROUP_SPEC = pl.BlockSpec((B_GROUPS, BIG_TILE // S5_Q, S5_K), lambda t: (0, t, 0))


def _even_in(i, xc, mod_p, norm_g, w_in, v_g, w_s, b_full, perm):
    return pl.pallas_call(
        _even_in_kernel,
        grid=(N_BIG,),
        in_specs=[
            _tile(D_MODEL), _MOD_SPEC, _full(1, D_MODEL), _layer(i, D_MODEL, EVEN_IN), _full(1, A_WIDTH),
            _layer(i, A_GROUPS, CHUNK, CHUNK), _full(CHUNK, A_WIDTH), _full(SUB_TILE, SUB_TILE),
        ],
        out_specs=[_tile(A_WIDTH), _tile(B_WIDTH), _tile(B_WIDTH), _GROUP_SPEC],
        out_shape=[
            jax.ShapeDtypeStruct((BATCH * NTOK, A_WIDTH), BF16),
            jax.ShapeDtypeStruct((BATCH * NTOK, B_WIDTH), F32),
            jax.ShapeDtypeStruct((BATCH * NTOK, B_WIDTH), BF16),
            jax.ShapeDtypeStruct((B_GROUPS, S5_ROWS, S5_K), BF16),
        ],
        compiler_params=_params("parallel"),
        name="even_in",
    )(xc, mod_p, norm_g, w_in, v_g, w_s, b_full, perm)


def _s5_kernel(xg_ref, prow_ref, pmat_ref, yg_ref, sl_ref, bst_ref, toep_ref, cst_ref, lhs_ref):
    for n in range(S5_PAIRS_PER_STEP):
        two = pl.ds(2 * n, 2)
        _s5_pair(xg_ref.at[two], prow_ref.at[n], pmat_ref.at[n], yg_ref.at[two], sl_ref.at[n],
                 bst_ref.at[n], toep_ref.at[two], cst_ref.at[two], lhs_ref.at[n])


def _s5_pair(xg_ref, prow_ref, pmat_ref, yg_ref, sl_ref, bst_ref, toep_ref, cst_ref, lhs_ref):
    lane = lax.broadcasted_iota(jnp.int32, (S5_Q, LANES), 1)
    in_group = (lane < B_STATE, lane >= B_STATE)
    blk = lane // B_CH
    zero = jnp.zeros((S5_Q, LANES), F32)

    def cmul(ar, ai, br, bi):
        return ar * br - ai * bi, ar * bi + ai * br

    w_in, c_out, scan_consts, c_rows = [], [], [], []
    for d in range(2):
        lam_re = prow_ref[d, 0:1, :]
        lam_im = prow_ref[d, 1:2, :]
        dt = jnp.exp(prow_ref[d, 2:3, :])
        z_re, z_im = lam_re * dt, lam_im * dt
        mag = jnp.exp(z_re)
        lb_re, lb_im = mag * jnp.cos(z_im), mag * jnp.sin(z_im)
        den = lam_re * lam_re + lam_im * lam_im
        n_re, n_im = lb_re - 1.0, lb_im
        f_re = (n_re * lam_re + n_im * lam_im) / den
        f_im = (n_im * lam_re - n_re * lam_im) / den
        b_re, b_im = pmat_ref[d, 0], pmat_ref[d, 1]
        c_re, c_im = pmat_ref[d, 2], pmat_ref[d, 3]
        bb_re, bb_im = cmul(f_re, f_im, b_re, b_im)
        c_rows.append((c_re, c_im))

        def powers(steps):
            m = jnp.exp(steps * z_re)
            return m * jnp.cos(steps * z_im), m * jnp.sin(steps * z_im)

        j = lax.broadcasted_iota(jnp.int32, (3 * SUBLANES, LANES), 0).astype(F32)
        p_re, p_im = powers(j)
        row = lambda a, k: a[k:k + 1, :]
        w_in.append([cmul(bb_re, bb_im, row(p_re, k), row(p_im, k)) for k in range(S5_Q)])
        c_out.append([cmul(c_re, c_im, row(p_re, k), row(p_im, k)) for k in range(S5_Q + 1)])

        i8 = lax.broadcasted_iota(jnp.int32, (SUBLANES, LANES), 0)
        order = i8 if d == 0 else (SUBLANES - 1) - i8
        apow = powers((order * S5_Q).astype(F32))
        bc = lambda r: powers(jnp.full((SUBLANES, LANES), r * S5_Q, F32))

        def doubling(r):
            inside = i8 >= r if d == 0 else i8 < SUBLANES - r
            return tuple(jnp.where(inside, a, 0.0) for a in bc(r))

        scan_consts.append((apow, [doubling(1), doubling(2), doubling(4)], bc(SUBLANES)))

    for e in range(2):
        sel = lambda a: jnp.where(in_group[e], a, 0.0)
        for s in range(S5_Q):
            fr, fi = w_in[0][S5_Q - 1 - s]
            br, bi = w_in[1][s]
            r0 = e * S5_K + s * B_CH
            bst_ref[r0:r0 + B_CH, :] = jnp.concatenate(
                [sel(fr), sel(fi), sel(br), sel(bi)], axis=1).astype(BF16)

    for e in range(2):
        sel = lambda a: jnp.where(in_group[e], a, 0.0)
        for t in range(S5_Q):
            fr, fi = c_out[0][t + 1]
            br, bi = c_out[1][S5_Q - t]
            cst_ref[e, t * B_CH:(t + 1) * B_CH, :] = jnp.concatenate(
                [sel(fr), sel(-fi), sel(br), sel(-bi)], axis=1).astype(BF16)

    for a in range(N_LAGS + 1):
        lag = a - (S5_Q - 1)
        f = w_in[0][lag] if 0 <= lag < S5_Q else (zero, zero)
        b = w_in[1][-lag] if -S5_Q < lag <= 0 else (zero, zero)
        lhs_ref[a * B_CH:(a + 1) * B_CH, :] = jnp.concatenate([f[0], f[1], b[0], b[1]], axis=1)
    cq_rows = []
    for e in range(2):
        sel = lambda a: jnp.where(in_group[e], a, 0.0)
        (fr, fi), (br, bi) = c_rows
        cq_rows.append(jnp.concatenate([sel(fr), sel(-fi), sel(br), sel(-bi)], axis=1))
    cq = jnp.concatenate(cq_rows + [jnp.zeros((LANES - 2 * B_CH, 4 * LANES), F32)], axis=0)
    kst = lax.dot_general(lhs_ref[...], cq, (((1,), (1,)), ((), ())),
                          preferred_element_type=F32, precision=HIGHEST)
    for e in range(2):
        g_cols = []
        for col in range((N_LAGS + 1) // GROUPS_PER_VREG):
            pieces = [(kst[(col * GROUPS_PER_VREG + k) * B_CH:(col * GROUPS_PER_VREG + k + 1) * B_CH, :], e)
                      for k in range(GROUPS_PER_VREG)]
            g_cols.append(_move_blocks(pieces, blk))
        for s in range(S5_Q):
            c0, sh = divmod((S5_Q - 1 - s) * B_CH, LANES)
            if sh == 0:
                cols = g_cols[c0:c0 + 2]
            else:
                rolled = [pltpu.roll(g_cols[c0 + k], LANES - sh, 1) for k in range(3)]
                cols = [jnp.where(lane < LANES - sh, rolled[k], rolled[k + 1]) for k in range(2)]
            toep_ref[e, s * B_CH:(s + 1) * B_CH, :] = jnp.concatenate(cols, axis=1).astype(BF16)

    x0 = xg_ref[0]
    x1 = xg_ref[1]
    sl_ref[...] = jnp.dot(jnp.concatenate([x0, x1], axis=1), bst_ref[...], preferred_element_type=F32)

    i8 = lax.broadcasted_iota(jnp.int32, (SUBLANES, LANES), 0)

    def scan_block(p, carry, consts, fwd):
        (ap_re, ap_im), doubling, (a8_re, a8_im) = consts
        p_re, p_im = p

        def earlier(a, r):
            return pltpu.roll(a, r if fwd else SUBLANES - r, 0)

        for r, (a_re, a_im) in zip((1, 2, 4), doubling):
            m_re, m_im = cmul(a_re, a_im, earlier(p_re, r), earlier(p_im, r))
            p_re, p_im = p_re + m_re, p_im + m_im
        c_re, c_im = carry
        e_re, e_im = cmul(ap_re, ap_im, c_re, c_im)
        first = i8 == (0 if fwd else SUBLANES - 1)
        entering = (jnp.where(first, e_re, earlier(p_re, 1) + e_re),
                    jnp.where(first, e_im, earlier(p_im, 1) + e_im))
        last = SUBLANES - 1 if fwd else 0
        n_re, n_im = cmul(a8_re, a8_im, c_re, c_im)
        bcast = lambda a: jnp.broadcast_to(a[last:last + 1, :], (SUBLANES, LANES))
        return entering, (bcast(p_re) + n_re, bcast(p_im) + n_im)

    z8 = jnp.zeros((SUBLANES, LANES), F32)
    carry = [[(z8, z8), (z8, z8)] for _ in range(BATCH)]
    for j in range(BLOCKS_PER_SAMPLE):
        jf = BLOCKS_PER_SAMPLE - CTX_BLOCKS + j if j < CTX_BLOCKS else j - CTX_BLOCKS
        jb = BLOCKS_PER_SAMPLE - 1 - j
        for b in range(BATCH):
            for d, jd in enumerate((jf, jb)):
                row0 = (b * BLOCKS_PER_SAMPLE + jd) * SUBLANES
                rows, col = slice(row0, row0 + SUBLANES), d * 2 * LANES
                p = (sl_ref[rows, col:col + LANES], sl_ref[rows, col + LANES:col + 2 * LANES])
                entering, carry[b][d] = scan_block(p, carry[b][d], scan_consts[d], d == 0)
                sl_ref[rows, col:col + LANES] = entering[0]
                sl_ref[rows, col + LANES:col + 2 * LANES] = entering[1]

    sp = sl_ref[...].astype(BF16)
    nt = (((1,), (1,)), ((), ()))
    yg_ref[0] = (jnp.dot(x0, toep_ref[0], preferred_element_type=F32)
                 + lax.dot_general(sp, cst_ref[0], nt, preferred_element_type=F32)).astype(BF16)
    yg_ref[1] = (jnp.dot(x1, toep_ref[1], preferred_element_type=F32)
                 + lax.dot_general(sp, cst_ref[1], nt, preferred_element_type=F32)).astype(BF16)


def _s5_params(lam_re, lam_im, log_dt, b_re, b_im, c_re, c_im):
    def rows(a):
        return a.astype(F32).reshape(2, N_PAIRS, 1, 2 * B_STATE).transpose(1, 0, 2, 3)

    dt = jnp.broadcast_to(log_dt[..., None], lam_re.shape)
    pad = jnp.zeros((N_PAIRS, 2, SUBLANES - 3, LANES), F32)
    prow = jnp.concatenate([rows(lam_re), rows(lam_im), rows(dt), pad], axis=2)

    def mats(a, channel_axis):
        a = a.astype(F32)
        if channel_axis == 3:
            a = a.transpose(0, 1, 3, 2)
        a = a.reshape(2, N_PAIRS, 2, B_CH, B_STATE).transpose(1, 0, 3, 2, 4)
        return a.reshape(N_PAIRS, 2, B_CH, 2 * B_STATE)

    pmat = jnp.stack([mats(b_re, 3), mats(b_im, 3), mats(c_re, 2), mats(c_im, 2)], axis=2)
    return prow, pmat


def _s5(xg, prow, pmat):
    n = S5_PAIRS_PER_STEP
    return pl.pallas_call(
        _s5_kernel,
        grid=(N_PAIRS // S5_PAIRS_PER_STEP,),
        in_specs=[
            pl.BlockSpec((2 * n, S5_ROWS, S5_K), lambda q: (q, 0, 0)),
            pl.BlockSpec((n, 2, SUBLANES, LANES), lambda q: (q, 0, 0, 0)),
            pl.BlockSpec((n, 2, 4, B_CH, LANES), lambda q: (q, 0, 0, 0, 0)),
        ],
        out_specs=pl.BlockSpec((2 * n, S5_ROWS, S5_K), lambda q: (q, 0, 0)),
        out_shape=jax.ShapeDtypeStruct((B_GROUPS, S5_ROWS, S5_K), BF16),
        scratch_shapes=[
            pltpu.VMEM((n, S5_ROWS, 4 * LANES), F32),
            pltpu.VMEM((n, 2 * S5_K, 4 * LANES), BF16),
            pltpu.VMEM((2 * n, S5_K, S5_K), BF16),
            pltpu.VMEM((2 * n, S5_K, 4 * LANES), BF16),
            pltpu.VMEM((n, (N_LAGS + 1) * B_CH, 4 * LANES), F32),
        ],
        compiler_params=_params("parallel"),
        name="s5_chunked",
    )(xg, prow, pmat)


def _even_out_kernel(ya_ref, xs_ref, gb_ref, yg_ref, x_ref, mod_ref, d_ref, wglu_ref, bglu_ref,
                     wout_ref, perm_ref, o_ref):
    blk = _lane_block(S5_Q)
    for sub in range(BIG_TILE // RELAYOUT_ROWS):
        rows = slice(sub * RELAYOUT_ROWS, (sub + 1) * RELAYOUT_ROWS)
        chunks = slice(sub * S5_Q, (sub + 1) * S5_Q)
        gt = mod_ref[sub * RELAYOUT_ROWS // ROW_TILE][:, 2 * D_MODEL:]
        steps = [[None] * (B_WIDTH // LANES) for _ in range(S5_Q)]
        for col in range(B_WIDTH // LANES):
            for j in range(S5_K // LANES):
                groups = [yg_ref[col * GROUPS_PER_VREG + k, chunks, j * LANES:(j + 1) * LANES].astype(F32)
                          for k in range(GROUPS_PER_VREG)]
                for b, a in enumerate(_block_transpose(groups, blk)):
                    steps[j * GROUPS_PER_VREG + b][col] = a
        ys = jnp.dot(perm_ref[...],
                     jnp.concatenate([jnp.concatenate(s, axis=1) for s in steps], axis=0).astype(BF16),
                     preferred_element_type=F32)

        y = ys + d_ref[...] * xs_ref[rows, :]
        y = jax.nn.gelu(y)
        t = jnp.dot(y.astype(BF16), wglu_ref[...], preferred_element_type=F32) + bglu_ref[...]
        y = y * jax.nn.sigmoid(t)
        gb = gb_ref[rows, :].astype(F32)
        yb = (y * (gb * jax.nn.sigmoid(gb))).astype(BF16)
        mix = (jnp.dot(ya_ref[rows, :], wout_ref[0:A_WIDTH, :], preferred_element_type=F32)
               + jnp.dot(yb, wout_ref[A_WIDTH:, :], preferred_element_type=F32))
        o_ref[rows, :] = x_ref[rows, :] + gt * mix


def _even_out(i, ya, xs, gb, yg, xc, mod_p, d_skip, w_glu, b_glu, w_out, perm):
    return pl.pallas_call(
        _even_out_kernel,
        grid=(N_BIG,),
        in_specs=[
            _tile(A_WIDTH), _tile(B_WIDTH), _tile(B_WIDTH), _GROUP_SPEC, _tile(D_MODEL), _MOD_SPEC,
            _full(1, B_WIDTH), _layer(i, B_WIDTH, B_WIDTH), _full(1, B_WIDTH), _layer(i, D_MODEL, D_MODEL),
            _full(RELAYOUT_ROWS, RELAYOUT_ROWS),
        ],
        out_specs=_tile(D_MODEL),
        out_shape=jax.ShapeDtypeStruct((BATCH * NTOK, D_MODEL), F32),
        compiler_params=_params("parallel"),
        name="even_out",
    )(ya, xs, gb, yg, xc, mod_p, d_skip, w_glu, b_glu, w_out, perm)


_HEAD_ORDER = np.concatenate([np.arange(0, 32), np.arange(64, 96), np.arange(32, 64), np.arange(96, 128)])


def _rope_tables():
    t = np.arange(SEQ)
    freqs = ROPE_THETA ** (-np.arange(ROPE_PAIRS, dtype=np.float64) / ROPE_PAIRS)
    ang = np.concatenate([(t // GRID_W)[:, None] * freqs, (t % GRID_W)[:, None] * freqs], axis=1)
    cos = np.concatenate([np.cos(ang), np.cos(ang)], axis=1)
    sin = np.concatenate([-np.sin(ang), np.sin(ang)], axis=1)
    pad = lambda a, v: np.concatenate([a, np.full((CTX_LEN, HEAD_DIM), v)], axis=0).astype(np.float32)
    return jnp.asarray(pad(cos, 1.0)), jnp.asarray(pad(sin, 0.0))


def _permute_heads(w_in, q_g, k_g):
    n_heads = N_Q + N_KV
    cols = (np.arange(n_heads)[:, None] * HEAD_DIM + _HEAD_ORDER[None, :]).reshape(-1)
    cols = np.concatenate([cols, np.arange(n_heads * HEAD_DIM, ODD_IN)])
    return w_in[..., cols], q_g[..., _HEAD_ORDER], k_g[..., _HEAD_ORDER]


def _odd_in_kernel(x_ref, mod_ref, g_ref, win_ref, qg_ref, kg_ref, *refs):
    tabs, (q_ref, k_ref, v_ref, gate_ref) = refs[:2 * PIECES], refs[2 * PIECES:]
    ones = jnp.ones((ROW_TILE, HEAD_DIM), BF16)
    for piece in range(PIECES):
        rows = slice(piece * ROW_TILE, (piece + 1) * ROW_TILE)
        h = _prologue(x_ref[rows, :], mod_ref[piece], g_ref[...])
        z = jnp.dot(h.astype(BF16), win_ref[...], preferred_element_type=F32)
        cos = tabs[2 * piece][...]
        sin = tabs[2 * piece + 1][...]

        def norm_rope(xh, gain, scale):
            xn = (xh * lax.rsqrt(jnp.mean(xh * xh, axis=-1, keepdims=True) + EPS)) * gain
            out = xn * cos + pltpu.roll(xn, HEAD_DIM // 2, 1) * sin
            return (out * scale).astype(BF16) if scale is not None else out.astype(BF16)

        for hq in range(N_Q):
            q_ref[rows, hq * HEAD_DIM:(hq + 1) * HEAD_DIM] = norm_rope(
                z[:, hq * HEAD_DIM:(hq + 1) * HEAD_DIM], qg_ref[...], Q_SCALE)
        for hk in range(N_KV):
            lo = C_WIDTH + hk * HEAD_DIM
            k_ref[rows, hk * HEAD_DIM:(hk + 1) * HEAD_DIM] = norm_rope(
                z[:, lo:lo + HEAD_DIM], kg_ref[...], None)
            lo = C_WIDTH + KV_WIDTH + hk * HEAD_DIM
            v_ref[rows, hk * V_EXT:hk * V_EXT + HEAD_DIM] = z[:, lo:lo + HEAD_DIM].astype(BF16)
            v_ref[rows, hk * V_EXT + HEAD_DIM:(hk + 1) * V_EXT] = ones
        gate_ref[rows, :] = z[:, C_WIDTH + 2 * KV_WIDTH:].astype(BF16)


def _odd_in(i, xc, mod_p, norm_g, w_in, q_g, k_g, cos, sin):
    tabs, tab_specs = [], []
    for p in range(PIECES):
        spec = pl.BlockSpec((ROW_TILE, HEAD_DIM), lambda t, p=p: ((t * PIECES + p) % N_TILES, 0))
        tabs += [cos, sin]
        tab_specs += [spec, spec]
    return pl.pallas_call(
        _odd_in_kernel,
        grid=(N_BIG,),
        in_specs=[
            _tile(D_MODEL), _MOD_SPEC, _full(1, D_MODEL), _layer(i, D_MODEL, ODD_IN),
            _full(1, HEAD_DIM), _full(1, HEAD_DIM), *tab_specs,
        ],
        out_specs=[_tile(C_WIDTH), _tile(KV_WIDTH), _tile(N_KV * V_EXT), _tile(C_WIDTH)],
        out_shape=[
            jax.ShapeDtypeStruct((BATCH * NTOK, C_WIDTH), BF16),
            jax.ShapeDtypeStruct((BATCH * NTOK, KV_WIDTH), BF16),
            jax.ShapeDtypeStruct((BATCH * NTOK, N_KV * V_EXT), BF16),
            jax.ShapeDtypeStruct((BATCH * NTOK, C_WIDTH), BF16),
        ],
        compiler_params=_params("parallel"),
        name="odd_in",
    )(xc, mod_p, norm_g, w_in, q_g, k_g, *tabs)


_NT = (((1,), (1,)), ((), ()))


def _softmax_weights(q, k):
    s = lax.dot_general(q, k, _NT, preferred_element_type=F32)
    return jnp.exp2(s - jnp.max(s, axis=-1, keepdims=True)).astype(BF16)


def _weighted_values(p, v):
    ov = jnp.dot(p, v, preferred_element_type=F32)
    return (ov[:, :HEAD_DIM] / ov[:, HEAD_DIM:HEAD_DIM + 1]).astype(BF16)


def _attn_lat_kernel(q_ref, k_ref, v_ref, o_ref, p_ref):
    @pl.when(pl.program_id(0) == 0)
    def _():
        p_ref[...] = jnp.ones(p_ref.shape, BF16)

    k = k_ref[0]
    v = v_ref[0]
    for h in range(Q_PER_KV):
        cols = slice(h * HEAD_DIM, (h + 1) * HEAD_DIM)
        for r in range(ATTN_TILE // ATTN_UNIT):
            rows = slice(r * ATTN_UNIT, (r + 1) * ATTN_UNIT)
            p_new = _softmax_weights(q_ref[0, rows, cols], k)
            o_ref[0, rows, cols] = _weighted_values(p_ref[h, rows, :], v)
            p_ref[h, rows, :] = p_new


def _attention_latent(q, k, v):
    n_tiles = SEQ // ATTN_TILE
    n_units = BATCH * N_KV * n_tiles
    width = Q_PER_KV * HEAD_DIM

    def unit(u):
        return u // (N_KV * n_tiles), (u // n_tiles) % N_KV, u % n_tiles

    def q_map(t):
        b, g, i = unit(jnp.minimum(t, n_units - 1))
        return b, i, g

    def k_map(t):
        b, g, _ = unit(jnp.minimum(t, n_units - 1))
        return b, 0, g

    def v_map(t):
        b, g, _ = unit(jnp.maximum(t - 1, 0))
        return b, 0, g

    def o_map(t):
        b, g, i = unit(jnp.maximum(t - 1, 0))
        return b, i, g

    return pl.pallas_call(
        _attn_lat_kernel,
        grid=(n_units + 1,),
        in_specs=[
            pl.BlockSpec((1, ATTN_TILE, width), q_map),
            pl.BlockSpec((1, NTOK, HEAD_DIM), k_map),
            pl.BlockSpec((1, NTOK, V_EXT), v_map),
        ],
        out_specs=pl.BlockSpec((1, ATTN_TILE, width), o_map),
        out_shape=jax.ShapeDtypeStruct((BATCH, SEQ, C_WIDTH), BF16),
        scratch_shapes=[pltpu.VMEM((Q_PER_KV, ATTN_TILE, NTOK), BF16)],
        compiler_params=_params("arbitrary"),
        name="attention_latent",
    )(q, k, v)


def _attn_ctx_kernel(q_ref, k_ref, v_ref, o_ref):
    for h in range(N_Q):
        g = h // Q_PER_KV
        p = _softmax_weights(q_ref[0, :, h * HEAD_DIM:(h + 1) * HEAD_DIM],
                             k_ref[0, :, g * HEAD_DIM:(g + 1) * HEAD_DIM])
        o_ref[0, :, h * HEAD_DIM:(h + 1) * HEAD_DIM] = _weighted_values(
            p, v_ref[0, :, g * V_EXT:(g + 1) * V_EXT])


def _attention_context(q, k, v):
    spec = lambda w: pl.BlockSpec((1, CTX_LEN, w), lambda b: (b, CTX_TILE, 0))
    return pl.pallas_call(
        _attn_ctx_kernel,
        grid=(BATCH,),
        in_specs=[spec(C_WIDTH), spec(KV_WIDTH), spec(N_KV * V_EXT)],
        out_specs=pl.BlockSpec((1, CTX_LEN, C_WIDTH), lambda b: (b, 0, 0)),
        out_shape=jax.ShapeDtypeStruct((BATCH, CTX_LEN, C_WIDTH), BF16),
        compiler_params=_params("parallel"),
        name="attention_context",
    )(q, k, v)


def _odd_out_kernel(*refs, final):
    if final:
        o_ref, gate_ref, x_ref, mod_ref, wout_ref, fg_ref, out_ref = refs
        attn = lambda piece: o_ref[piece * ROW_TILE:(piece + 1) * ROW_TILE, :]
    else:
        o_refs, (gate_ref, x_ref, mod_ref, wout_ref, fg_ref, out_ref) = refs[:2 * PIECES], refs[2 * PIECES:]

        def attn(piece):
            is_ctx = (pl.program_id(0) * PIECES + piece) % N_TILES == CTX_TILE
            return jnp.where(is_ctx, o_refs[2 * piece + 1][...], o_refs[2 * piece][...])

    for piece in range(PIECES):
        rows = slice(piece * ROW_TILE, (piece + 1) * ROW_TILE)
        g = gate_ref[rows, :].astype(F32)
        a = (attn(piece).astype(F32) * (g * jax.nn.sigmoid(g))).astype(BF16)
        gt = mod_ref[piece % mod_ref.shape[0]][:, 2 * D_MODEL:]
        x = x_ref[rows, :] + gt * jnp.dot(a, wout_ref[...], preferred_element_type=F32)
        if final:
            x = (x * lax.rsqrt(jnp.mean(x * x, axis=-1, keepdims=True) + EPS)) * fg_ref[...]
        out_ref[rows, :] = x


def _odd_out(i, o_lat, o_ctx, gate, xc, mod_p, w_out, final_g):
    o_args, o_specs = [], []
    for p in range(PIECES):
        sample = lambda t, p=p: (t * PIECES + p) // N_TILES
        lat_tile = lambda t, p=p: jnp.minimum((t * PIECES + p) % N_TILES, LAT_TILES - 1)
        o_args += [o_lat, o_ctx]
        o_specs += [
            pl.BlockSpec((None, ROW_TILE, C_WIDTH), lambda t, s=sample, i=lat_tile: (s(t), i(t), 0)),
            pl.BlockSpec((None, CTX_LEN, C_WIDTH), lambda t, s=sample: (s(t), 0, 0)),
        ]
    return pl.pallas_call(
        functools.partial(_odd_out_kernel, final=False),
        grid=(N_BIG,),
        in_specs=[*o_specs, _tile(C_WIDTH), _tile(D_MODEL), _MOD_SPEC,
                  _layer(i, D_MODEL, D_MODEL), _full(1, D_MODEL)],
        out_specs=_tile(D_MODEL),
        out_shape=jax.ShapeDtypeStruct((BATCH * NTOK, D_MODEL), F32),
        compiler_params=_params("parallel"),
        name="odd_out",
    )(*o_args, gate, xc, mod_p, w_out, final_g)


def _odd_out_final(i, o, gate, xc, mod_l, w_out, final_g):
    tile = lambda w: pl.BlockSpec((None, BIG_TILE, w), lambda b, j: (b, j, 0))
    full = lambda *s: pl.BlockSpec(s, lambda b, j: (0,) * len(s))
    per_sample = lambda a, w: a.reshape(BATCH, NTOK, w)
    return pl.pallas_call(
        functools.partial(_odd_out_kernel, final=True),
        grid=(BATCH, SEQ // BIG_TILE),
        in_specs=[tile(C_WIDTH), tile(C_WIDTH), tile(D_MODEL),
                  pl.BlockSpec((1, 1, 3 * D_MODEL), lambda b, j: (b, 0, 0)),
                  _layer(i, D_MODEL, D_MODEL), full(1, D_MODEL)],
        out_specs=tile(D_MODEL),
        out_shape=jax.ShapeDtypeStruct((BATCH, SEQ, D_MODEL), F32),
        compiler_params=_params("parallel", "parallel"),
        name="odd_out_final",
    )(o, per_sample(gate, C_WIDTH), per_sample(xc, D_MODEL), mod_l, w_out, final_g)


def kernel(x, c, ctx, c_ctx, norm_g, w_mod, b_mod, we_in, we_out, gm_v_g, gm_w_s, gm_b_s,
           s5_lam_re, s5_lam_im, s5_log_dt, s5_b_re, s5_b_im, s5_c_re, s5_c_im, s5_d,
           s5_w_glu, s5_b_glu, wo_in, wo_out, q_norm_g, k_norm_g, final_g):
    cond = jnp.concatenate([c, c_ctx[None], jnp.zeros((MOD_ROWS - BATCH - 1, D_MODEL), F32)], axis=0)
    mods = _modulation(cond, w_mod, b_mod).reshape(DEPTH, MOD_ROWS, 1, 3 * D_MODEL)
    xc = jnp.concatenate([x, ctx], axis=1).reshape(BATCH * NTOK, D_MODEL)
    piece_rows = np.where(np.arange(N_TILES)[None, :] == CTX_TILE, CTX_ROW, np.arange(BATCH)[:, None]).reshape(-1)
    cos, sin = _rope_tables()
    perm_in = _chunk_perm(SUB_TILE)
    perm_out = _chunk_perm(RELAYOUT_ROWS).T
    row = lambda a: a.reshape(1, -1)
    per_sample = lambda a: a.reshape(BATCH, NTOK, a.shape[-1])
    we_in, we_out, w_glu, w_s, wo_out = (w.astype(BF16) for w in (we_in, we_out, s5_w_glu, gm_w_s, wo_out))

    for layer in range(DEPTH):
        i = layer // 2
        mod_p = mods[layer][piece_rows]
        if layer % 2 == 0:
            b_full = jnp.repeat(gm_b_s[i].T, A_GROUP_W, axis=1)
            ya, xs, gb, xg = _even_in(i, xc, mod_p, row(norm_g[layer]), we_in, row(gm_v_g[i]), w_s,
                                      b_full, perm_in)
            prow, pmat = _s5_params(s5_lam_re[i], s5_lam_im[i], s5_log_dt[i], s5_b_re[i],
                                    s5_b_im[i], s5_c_re[i], s5_c_im[i])
            yg = _s5(xg, prow, pmat)
            xc = _even_out(i, ya, xs, gb, yg, xc, mod_p, row(s5_d[i]), w_glu, row(s5_b_glu[i]), we_out,
                           perm_out)
        else:
            w_in, q_g, k_g = _permute_heads(wo_in[i], q_norm_g[i], k_norm_g[i])
            q, k, v, gate = _odd_in(0, xc, mod_p, row(norm_g[layer]), w_in.astype(BF16)[None], row(q_g),
                                    row(k_g), cos, sin)
            q, k, v = per_sample(q), per_sample(k), per_sample(v)
            o = _attention_latent(q, k, v)
            if layer == DEPTH - 1:
                return _odd_out_final(i, o, gate, xc, mods[layer], wo_out, row(final_g))
            xc = _odd_out(i, o, _attention_context(q, k, v), gate, xc, mod_p, wo_out, row(final_g))
```

```python
import functools
import math

import jax
import jax.numpy as jnp
import numpy as np
from jax import lax
from jax.experimental import pallas as pl
from jax.experimental.pallas import tpu as pltpu

F32 = jnp.float32
BF16 = jnp.bfloat16
HIGHEST = lax.Precision.HIGHEST

D_MODEL = 1024
BATCH = 4
SEQ = 4096
DEPTH = 4
GRID_W = 64
CTX_LEN = 256
EPS = 1e-6
NTOK = CTX_LEN + SEQ

LANES = 128
SUBLANES = 8
ROW_TILE = 256
N_TILES = NTOK // ROW_TILE
CTX_TILE = N_TILES - 1
LAT_TILES = N_TILES - 1
BIG_TILE = 1024
PIECES = BIG_TILE // ROW_TILE
N_BIG = BATCH * NTOK // BIG_TILE
assert N_BIG * BIG_TILE == BATCH * NTOK and SEQ % BIG_TILE == 0
VMEM_LIMIT = 56 * 1024 * 1024

CHUNK = 128
A_WIDTH = D_MODEL // 2
A_GROUPS = 4
A_GROUP_W = A_WIDTH // A_GROUPS
B_WIDTH = D_MODEL // 2
B_CH = 16
B_GROUPS = B_WIDTH // B_CH
B_STATE = 64
EVEN_IN = 3 * A_WIDTH + 2 * B_WIDTH
HEAD_DIM = 128
N_Q = D_MODEL // HEAD_DIM
N_KV = 2
Q_PER_KV = N_Q // N_KV
C_WIDTH = N_Q * HEAD_DIM
KV_WIDTH = N_KV * HEAD_DIM
ODD_IN = 2 * C_WIDTH + 2 * KV_WIDTH
ROPE_THETA = 10000.0
ROPE_PAIRS = HEAD_DIM // 4
Q_SCALE = HEAD_DIM ** -0.5 * math.log2(math.e)
V_EXT = 2 * HEAD_DIM
ATTN_TILE = 1024
ATTN_UNIT = 128

S5_Q = 16
S5_K = S5_Q * B_CH
N_CHUNKS = NTOK // S5_Q
S5_ROWS = BATCH * N_CHUNKS
SUB_TILE = CHUNK
SUB_CHUNKS = SUB_TILE // S5_Q
RELAYOUT_ROWS = 16 * S5_Q
BLOCKS_PER_SAMPLE = N_CHUNKS // SUBLANES
CTX_BLOCKS = CTX_LEN // S5_Q // SUBLANES
N_PAIRS = B_GROUPS // 2
S5_PAIRS_PER_STEP = 2
GROUPS_PER_VREG = LANES // B_CH
N_LAGS = 2 * S5_Q - 1

MOD_ROWS = 8
CTX_ROW = BATCH


def _params(*sem):
    return pltpu.CompilerParams(dimension_semantics=sem, vmem_limit_bytes=VMEM_LIMIT)


def _lane_block(rows):
    return lax.broadcasted_iota(jnp.int32, (rows, LANES), 1) // B_CH


def _move_blocks(pieces, blk):
    acc = None
    for dst, (src, src_blk) in enumerate(pieces):
        sh = ((dst - src_blk) % GROUPS_PER_VREG) * B_CH
        rolled = pltpu.roll(src, sh, 1) if sh else src
        acc = rolled if acc is None else jnp.where(blk == dst, rolled, acc)
    return acc


def _merge_blocks(pick, blk):
    acc = pick(0)
    for k in range(1, GROUPS_PER_VREG):
        acc = jnp.where(blk == k, pick(k), acc)
    return acc


def _block_transpose(src, blk):
    n = GROUPS_PER_VREG
    rolled = []
    for r in range(n):
        m = _merge_blocks(lambda b: src[(b + r) % n], blk)
        rolled.append(pltpu.roll(m, r * B_CH, 1) if r else m)
    return [_merge_blocks(lambda k: rolled[(k - b) % n], blk) for b in range(n)]


def _mod_kernel(cond_ref, w_ref, b_ref, o_ref):
    c = cond_ref[...]
    s = c * jax.nn.sigmoid(c)
    o_ref[0] = jnp.dot(s, w_ref[0], preferred_element_type=F32, precision=HIGHEST) + b_ref[0]


def _modulation(cond, w_mod, b_mod):
    nblk = 3 * D_MODEL // D_MODEL
    return pl.pallas_call(
        _mod_kernel,
        grid=(DEPTH, nblk),
        in_specs=[
            pl.BlockSpec((MOD_ROWS, D_MODEL), lambda l, j: (0, 0)),
            pl.BlockSpec((1, D_MODEL, D_MODEL), lambda l, j: (l, 0, j)),
            pl.BlockSpec((1, 1, D_MODEL), lambda l, j: (l, 0, j)),
        ],
        out_specs=pl.BlockSpec((1, MOD_ROWS, D_MODEL), lambda l, j: (l, 0, j)),
        out_shape=jax.ShapeDtypeStruct((DEPTH, MOD_ROWS, 3 * D_MODEL), F32),
        compiler_params=_params("parallel", "parallel"),
        name="modulation",
    )(cond, w_mod, b_mod.reshape(DEPTH, 1, 3 * D_MODEL))


def _prologue(x, mod, g):
    sh = mod[:, :D_MODEL]
    sc = mod[:, D_MODEL:2 * D_MODEL]
    y = x * lax.rsqrt(jnp.mean(x * x, axis=-1, keepdims=True) + EPS)
    return (y * g) * (1 + sc) + sh


def _chunk_perm(rows):
    r = np.arange(rows)
    chunks = rows // S5_Q
    source = (r % chunks) * S5_Q + r // chunks
    return jnp.asarray(source[:, None] == r[None, :], BF16)


def _even_in_kernel(*refs, split):
    read_x, refs = _residual_reader(refs, split)
    mod_ref, g_ref, win_ref, vg_ref, ws_ref, bs_ref, perm_ref, ya_ref, xs_ref, gb_ref, xg_ref = refs
    blk = _lane_block(SUB_CHUNKS)
    xg_parts = []
    for sub in range(BIG_TILE // SUB_TILE):
        rows = slice(sub * SUB_TILE, (sub + 1) * SUB_TILE)
        h = _prologue(read_x(rows), mod_ref[sub * SUB_TILE // ROW_TILE], g_ref[...])
        z = jnp.dot(h.astype(BF16), win_ref[...], preferred_element_type=F32)
        u = z[:, 0:A_WIDTH]
        v = z[:, A_WIDTH:2 * A_WIDTH]
        ga = z[:, 2 * A_WIDTH:3 * A_WIDTH]
        xs = z[:, 3 * A_WIDTH:3 * A_WIDTH + B_WIDTH]
        xs_ref[rows, :] = xs
        gb_ref[rows, :] = z[:, 3 * A_WIDTH + B_WIDTH:].astype(BF16)
        mu = jnp.mean(v, axis=-1, keepdims=True)
        vc = v - mu
        var = jnp.mean(vc * vc, axis=-1, keepdims=True)
        vn = ((vc * lax.rsqrt(var + EPS)) * vg_ref[...]).astype(BF16)
        mixed = jnp.concatenate(
            [jnp.dot(ws_ref[g], vn[:, g * A_GROUP_W:(g + 1) * A_GROUP_W], preferred_element_type=F32)
             for g in range(A_GROUPS)], axis=1) + bs_ref[...]
        ya_ref[rows, :] = ((u * mixed) * (ga * jax.nn.sigmoid(ga))).astype(BF16)

        r = jnp.dot(perm_ref[...], xs.astype(BF16), preferred_element_type=F32)
        parts = [[None] * (S5_K // LANES) for _ in range(B_GROUPS)]
        for col in range(B_WIDTH // LANES):
            for j in range(S5_K // LANES):
                steps = [r[(j * GROUPS_PER_VREG + k) * SUB_CHUNKS:(j * GROUPS_PER_VREG + k + 1) * SUB_CHUNKS,
                           col * LANES:(col + 1) * LANES] for k in range(GROUPS_PER_VREG)]
                for b, a in enumerate(_block_transpose(steps, blk)):
                    parts[col * GROUPS_PER_VREG + b][j] = a
        xg_parts.append([jnp.concatenate(p, axis=1) for p in parts])
        if len(xg_parts) == 2:
            first = (sub - 1) * SUB_CHUNKS
            for g in range(B_GROUPS):
                xg_ref[g, first:first + 2 * SUB_CHUNKS, :] = jnp.concatenate(
                    [xg_parts[0][g], xg_parts[1][g]], axis=0).astype(BF16)
            xg_parts = []


def _tile(w):
    return pl.BlockSpec((BIG_TILE, w), lambda t: (t, 0))


def _full(*shape):
    return pl.BlockSpec(shape, lambda t: (0,) * len(shape))


def _layer(i, *shape):
    return pl.BlockSpec((None, *shape), lambda *_: (i,) + (0,) * len(shape))


def _piece_operands(lat, ctx, width):
    args, specs = [], []
    for p in range(PIECES):
        sample = lambda t, p=p: (t * PIECES + p) // N_TILES
        lat_tile = lambda t, p=p: jnp.minimum((t * PIECES + p) % N_TILES, LAT_TILES - 1)
        args += [lat, ctx]
        specs += [
            pl.BlockSpec((None, ROW_TILE, width), lambda t, s=sample, i=lat_tile: (s(t), i(t), 0)),
            pl.BlockSpec((None, CTX_LEN, width), lambda t, s=sample: (s(t), 0, 0)),
        ]
    return args, specs


def _piece_reader(piece_refs):
    def read(rows):
        piece, lo = divmod(rows.start, ROW_TILE)
        inner = slice(lo, lo + rows.stop - rows.start)
        is_ctx = (pl.program_id(0) * PIECES + piece) % N_TILES == CTX_TILE
        return jnp.where(is_ctx, piece_refs[2 * piece + 1][inner, :], piece_refs[2 * piece][inner, :])
    return read


def _residual_reader(refs, split):
    if split:
        return _piece_reader(refs[:2 * PIECES]), refs[2 * PIECES:]
    return (lambda rows: refs[0][rows, :]), refs[1:]


_MOD_SPEC = pl.BlockSpec((PIECES, 1, 3 * D_MODEL), lambda t: (t, 0, 0))
_GROUP_SPEC = pl.BlockSpec((B_GROUPS, BIG_TILE // S5_Q, S5_K), lambda t: (0, t, 0))


def _residual_operands(xc):
    if isinstance(xc, tuple):
        return _piece_operands(*xc, D_MODEL)
    return [xc], [_tile(D_MODEL)]


def _even_in(i, xc, mod_p, norm_g, w_in, v_g, w_s, b_full, perm):
    x_args, x_specs = _residual_operands(xc)
    return pl.pallas_call(
        functools.partial(_even_in_kernel, split=isinstance(xc, tuple)),
        grid=(N_BIG,),
        in_specs=[
            *x_specs, _MOD_SPEC, _full(1, D_MODEL), _layer(i, D_MODEL, EVEN_IN), _full(1, A_WIDTH),
            _layer(i, A_GROUPS, CHUNK, CHUNK), _full(CHUNK, A_WIDTH), _full(SUB_TILE, SUB_TILE),
        ],
        out_specs=[_tile(A_WIDTH), _tile(B_WIDTH), _tile(B_WIDTH), _GROUP_SPEC],
        out_shape=[
            jax.ShapeDtypeStruct((BATCH * NTOK, A_WIDTH), BF16),
            jax.ShapeDtypeStruct((BATCH * NTOK, B_WIDTH), F32),
            jax.ShapeDtypeStruct((BATCH * NTOK, B_WIDTH), BF16),
            jax.ShapeDtypeStruct((B_GROUPS, S5_ROWS, S5_K), BF16),
        ],
        compiler_params=_params("parallel"),
        name="even_in",
    )(*x_args, mod_p, norm_g, w_in, v_g, w_s, b_full, perm)


def _s5_kernel(xg_ref, prow_ref, pmat_ref, yg_ref, sl_ref, bst_ref, toep_ref, cst_ref, lhs_ref):
    for n in range(S5_PAIRS_PER_STEP):
        two = pl.ds(2 * n, 2)
        _s5_pair(xg_ref.at[two], prow_ref.at[n], pmat_ref.at[n], yg_ref.at[two], sl_ref.at[n],
                 bst_ref.at[n], toep_ref.at[two], cst_ref.at[two], lhs_ref.at[n])


def _s5_pair(xg_ref, prow_ref, pmat_ref, yg_ref, sl_ref, bst_ref, toep_ref, cst_ref, lhs_ref):
    lane = lax.broadcasted_iota(jnp.int32, (S5_Q, LANES), 1)
    in_group = (lane < B_STATE, lane >= B_STATE)
    blk = lane // B_CH
    zero = jnp.zeros((S5_Q, LANES), F32)

    def cmul(ar, ai, br, bi):
        return ar * br - ai * bi, ar * bi + ai * br

    w_in, c_out, scan_consts, c_rows = [], [], [], []
    for d in range(2):
        lam_re = prow_ref[d, 0:1, :]
        lam_im = prow_ref[d, 1:2, :]
        dt = jnp.exp(prow_ref[d, 2:3, :])
        z_re, z_im = lam_re * dt, lam_im * dt
        mag = jnp.exp(z_re)
        lb_re, lb_im = mag * jnp.cos(z_im), mag * jnp.sin(z_im)
        den = lam_re * lam_re + lam_im * lam_im
        n_re, n_im = lb_re - 1.0, lb_im
        f_re = (n_re * lam_re + n_im * lam_im) / den
        f_im = (n_im * lam_re - n_re * lam_im) / den
        b_re, b_im = pmat_ref[d, 0], pmat_ref[d, 1]
        c_re, c_im = pmat_ref[d, 2], pmat_ref[d, 3]
        bb_re, bb_im = cmul(f_re, f_im, b_re, b_im)
        c_rows.append((c_re, c_im))

        def powers(steps):
            m = jnp.exp(steps * z_re)
            return m * jnp.cos(steps * z_im), m * jnp.sin(steps * z_im)

        j = lax.broadcasted_iota(jnp.int32, (3 * SUBLANES, LANES), 0).astype(F32)
        p_re, p_im = powers(j)
        row = lambda a, k: a[k:k + 1, :]
        w_in.append([cmul(bb_re, bb_im, row(p_re, k), row(p_im, k)) for k in range(S5_Q)])
        c_out.append([cmul(c_re, c_im, row(p_re, k), row(p_im, k)) for k in range(S5_Q + 1)])

        i8 = lax.broadcasted_iota(jnp.int32, (SUBLANES, LANES), 0)
        order = i8 if d == 0 else (SUBLANES - 1) - i8
        apow = powers((order * S5_Q).astype(F32))
        bc = lambda r: powers(jnp.full((SUBLANES, LANES), r * S5_Q, F32))

        def doubling(r):
            inside = i8 >= r if d == 0 else i8 < SUBLANES - r
            return tuple(jnp.where(inside, a, 0.0) for a in bc(r))

        scan_consts.append((apow, [doubling(1), doubling(2), doubling(4)], bc(SUBLANES)))

    for e in range(2):
        sel = lambda a: jnp.where(in_group[e], a, 0.0)
        for s in range(S5_Q):
            fr, fi = w_in[0][S5_Q - 1 - s]
            br, bi = w_in[1][s]
            r0 = e * S5_K + s * B_CH
            bst_ref[r0:r0 + B_CH, :] = jnp.concatenate(
                [sel(fr), sel(fi), sel(br), sel(bi)], axis=1).astype(BF16)

    for e in range(2):
        sel = lambda a: jnp.where(in_group[e], a, 0.0)
        for t in range(S5_Q):
            fr, fi = c_out[0][t + 1]
            br, bi = c_out[1][S5_Q - t]
            cst_ref[e, t * B_CH:(t + 1) * B_CH, :] = jnp.concatenate(
                [sel(fr), sel(-fi), sel(br), sel(-bi)], axis=1).astype(BF16)

    for a in range(N_LAGS + 1):
        lag = a - (S5_Q - 1)
        f = w_in[0][lag] if 0 <= lag < S5_Q else (zero, zero)
        b = w_in[1][-lag] if -S5_Q < lag <= 0 else (zero, zero)
        lhs_ref[a * B_CH:(a + 1) * B_CH, :] = jnp.concatenate([f[0], f[1], b[0], b[1]], axis=1)
    cq_rows = []
    for e in range(2):
        sel = lambda a: jnp.where(in_group[e], a, 0.0)
        (fr, fi), (br, bi) = c_rows
        cq_rows.append(jnp.concatenate([sel(fr), sel(-fi), sel(br), sel(-bi)], axis=1))
    cq = jnp.concatenate(cq_rows + [jnp.zeros((LANES - 2 * B_CH, 4 * LANES), F32)], axis=0)
    kst = lax.dot_general(lhs_ref[...], cq, (((1,), (1,)), ((), ())),
                          preferred_element_type=F32, precision=HIGHEST)
    for e in range(2):
        g_cols = []
        for col in range((N_LAGS + 1) // GROUPS_PER_VREG):
            pieces = [(kst[(col * GROUPS_PER_VREG + k) * B_CH:(col * GROUPS_PER_VREG + k + 1) * B_CH, :], e)
                      for k in range(GROUPS_PER_VREG)]
            g_cols.append(_move_blocks(pieces, blk))
        for s in range(S5_Q):
            c0, sh = divmod((S5_Q - 1 - s) * B_CH, LANES)
            if sh == 0:
                cols = g_cols[c0:c0 + 2]
            else:
                rolled = [pltpu.roll(g_cols[c0 + k], LANES - sh, 1) for k in range(3)]
                cols = [jnp.where(lane < LANES - sh, rolled[k], rolled[k + 1]) for k in range(2)]
            toep_ref[e, s * B_CH:(s + 1) * B_CH, :] = jnp.concatenate(cols, axis=1).astype(BF16)

    x0 = xg_ref[0]
    x1 = xg_ref[1]
    sl_ref[...] = jnp.dot(jnp.concatenate([x0, x1], axis=1), bst_ref[...], preferred_element_type=F32)

    i8 = lax.broadcasted_iota(jnp.int32, (SUBLANES, LANES), 0)

    def scan_block(p, carry, consts, fwd):
        (ap_re, ap_im), doubling, (a8_re, a8_im) = consts
        p_re, p_im = p

        def earlier(a, r):
            return pltpu.roll(a, r if fwd else SUBLANES - r, 0)

        for r, (a_re, a_im) in zip((1, 2, 4), doubling):
            m_re, m_im = cmul(a_re, a_im, earlier(p_re, r), earlier(p_im, r))
            p_re, p_im = p_re + m_re, p_im + m_im
        c_re, c_im = carry
        e_re, e_im = cmul(ap_re, ap_im, c_re, c_im)
        first = i8 == (0 if fwd else SUBLANES - 1)
        entering = (jnp.where(first, e_re, earlier(p_re, 1) + e_re),
                    jnp.where(first, e_im, earlier(p_im, 1) + e_im))
        last = SUBLANES - 1 if fwd else 0
        n_re, n_im = cmul(a8_re, a8_im, c_re, c_im)
        bcast = lambda a: jnp.broadcast_to(a[last:last + 1, :], (SUBLANES, LANES))
        return entering, (bcast(p_re) + n_re, bcast(p_im) + n_im)

    z8 = jnp.zeros((SUBLANES, LANES), F32)
    carry = [[(z8, z8), (z8, z8)] for _ in range(BATCH)]
    for j in range(BLOCKS_PER_SAMPLE):
        jf = BLOCKS_PER_SAMPLE - CTX_BLOCKS + j if j < CTX_BLOCKS else j - CTX_BLOCKS
        jb = BLOCKS_PER_SAMPLE - 1 - j
        for b in range(BATCH):
            for d, jd in enumerate((jf, jb)):
                row0 = (b * BLOCKS_PER_SAMPLE + jd) * SUBLANES
                rows, col = slice(row0, row0 + SUBLANES), d * 2 * LANES
                p = (sl_ref[rows, col:col + LANES], sl_ref[rows, col + LANES:col + 2 * LANES])
                entering, carry[b][d] = scan_block(p, carry[b][d], scan_consts[d], d == 0)
                sl_ref[rows, col:col + LANES] = entering[0]
                sl_ref[rows, col + LANES:col + 2 * LANES] = entering[1]

    sp = sl_ref[...].astype(BF16)
    nt = (((1,), (1,)), ((), ()))
    yg_ref[0] = (jnp.dot(x0, toep_ref[0], preferred_element_type=F32)
                 + lax.dot_general(sp, cst_ref[0], nt, preferred_element_type=F32)).astype(BF16)
    yg_ref[1] = (jnp.dot(x1, toep_ref[1], preferred_element_type=F32)
                 + lax.dot_general(sp, cst_ref[1], nt, preferred_element_type=F32)).astype(BF16)


def _s5_params(lam_re, lam_im, log_dt, b_re, b_im, c_re, c_im):
    def rows(a):
        return a.astype(F32).reshape(2, N_PAIRS, 1, 2 * B_STATE).transpose(1, 0, 2, 3)

    dt = jnp.broadcast_to(log_dt[..., None], lam_re.shape)
    pad = jnp.zeros((N_PAIRS, 2, SUBLANES - 3, LANES), F32)
    prow = jnp.concatenate([rows(lam_re), rows(lam_im), rows(dt), pad], axis=2)

    def mats(a, channel_axis):
        a = a.astype(F32)
        if channel_axis == 3:
            a = a.transpose(0, 1, 3, 2)
        a = a.reshape(2, N_PAIRS, 2, B_CH, B_STATE).transpose(1, 0, 3, 2, 4)
        return a.reshape(N_PAIRS, 2, B_CH, 2 * B_STATE)

    pmat = jnp.stack([mats(b_re, 3), mats(b_im, 3), mats(c_re, 2), mats(c_im, 2)], axis=2)
    return prow, pmat


def _s5(xg, prow, pmat):
    n = S5_PAIRS_PER_STEP
    return pl.pallas_call(
        _s5_kernel,
        grid=(N_PAIRS // S5_PAIRS_PER_STEP,),
        in_specs=[
            pl.BlockSpec((2 * n, S5_ROWS, S5_K), lambda q: (q, 0, 0)),
            pl.BlockSpec((n, 2, SUBLANES, LANES), lambda q: (q, 0, 0, 0)),
            pl.BlockSpec((n, 2, 4, B_CH, LANES), lambda q: (q, 0, 0, 0, 0)),
        ],
        out_specs=pl.BlockSpec((2 * n, S5_ROWS, S5_K), lambda q: (q, 0, 0)),
        out_shape=jax.ShapeDtypeStruct((B_GROUPS, S5_ROWS, S5_K), BF16),
        scratch_shapes=[
            pltpu.VMEM((n, S5_ROWS, 4 * LANES), F32),
            pltpu.VMEM((n, 2 * S5_K, 4 * LANES), BF16),
            pltpu.VMEM((2 * n, S5_K, S5_K), BF16),
            pltpu.VMEM((2 * n, S5_K, 4 * LANES), BF16),
            pltpu.VMEM((n, (N_LAGS + 1) * B_CH, 4 * LANES), F32),
        ],
        compiler_params=_params("parallel"),
        name="s5_chunked",
    )(xg, prow, pmat)


def _even_out_kernel(*refs, split):
    read_x, refs = _residual_reader(refs, split)
    ya_ref, xs_ref, gb_ref, yg_ref, mod_ref, d_ref, wglu_ref, bglu_ref, wout_ref, perm_ref, o_ref = refs
    blk = _lane_block(S5_Q)
    for sub in range(BIG_TILE // RELAYOUT_ROWS):
        rows = slice(sub * RELAYOUT_ROWS, (sub + 1) * RELAYOUT_ROWS)
        chunks = slice(sub * S5_Q, (sub + 1) * S5_Q)
        gt = mod_ref[sub * RELAYOUT_ROWS // ROW_TILE][:, 2 * D_MODEL:]
        steps = [[None] * (B_WIDTH // LANES) for _ in range(S5_Q)]
        for col in range(B_WIDTH // LANES):
            for j in range(S5_K // LANES):
                groups = [yg_ref[col * GROUPS_PER_VREG + k, chunks, j * LANES:(j + 1) * LANES].astype(F32)
                          for k in range(GROUPS_PER_VREG)]
                for b, a in enumerate(_block_transpose(groups, blk)):
                    steps[j * GROUPS_PER_VREG + b][col] = a
        ys = jnp.dot(perm_ref[...],
                     jnp.concatenate([jnp.concatenate(s, axis=1) for s in steps], axis=0).astype(BF16),
                     preferred_element_type=F32)

        y = ys + d_ref[...] * xs_ref[rows, :]
        y = jax.nn.gelu(y)
        t = jnp.dot(y.astype(BF16), wglu_ref[...], preferred_element_type=F32) + bglu_ref[...]
        y = y * jax.nn.sigmoid(t)
        gb = gb_ref[rows, :].astype(F32)
        yb = (y * (gb * jax.nn.sigmoid(gb))).astype(BF16)
        mix = (jnp.dot(ya_ref[rows, :], wout_ref[0:A_WIDTH, :], preferred_element_type=F32)
               + jnp.dot(yb, wout_ref[A_WIDTH:, :], preferred_element_type=F32))
        o_ref[rows, :] = read_x(rows) + gt * mix


def _even_out(i, ya, xs, gb, yg, xc, mod_p, d_skip, w_glu, b_glu, w_out, perm):
    x_args, x_specs = _residual_operands(xc)
    return pl.pallas_call(
        functools.partial(_even_out_kernel, split=isinstance(xc, tuple)),
        grid=(N_BIG,),
        in_specs=[
            *x_specs, _tile(A_WIDTH), _tile(B_WIDTH), _tile(B_WIDTH), _GROUP_SPEC, _MOD_SPEC,
            _full(1, B_WIDTH), _layer(i, B_WIDTH, B_WIDTH), _full(1, B_WIDTH), _layer(i, D_MODEL, D_MODEL),
            _full(RELAYOUT_ROWS, RELAYOUT_ROWS),
        ],
        out_specs=_tile(D_MODEL),
        out_shape=jax.ShapeDtypeStruct((BATCH * NTOK, D_MODEL), F32),
        compiler_params=_params("parallel"),
        name="even_out",
    )(*x_args, ya, xs, gb, yg, mod_p, d_skip, w_glu, b_glu, w_out, perm)


_HEAD_ORDER = np.concatenate([np.arange(0, 32), np.arange(64, 96), np.arange(32, 64), np.arange(96, 128)])


def _rope_tables():
    t = np.arange(SEQ)
    freqs = ROPE_THETA ** (-np.arange(ROPE_PAIRS, dtype=np.float64) / ROPE_PAIRS)
    ang = np.concatenate([(t // GRID_W)[:, None] * freqs, (t % GRID_W)[:, None] * freqs], axis=1)
    cos = np.concatenate([np.cos(ang), np.cos(ang)], axis=1)
    sin = np.concatenate([-np.sin(ang), np.sin(ang)], axis=1)
    pad = lambda a, v: np.concatenate([a, np.full((CTX_LEN, HEAD_DIM), v)], axis=0).astype(np.float32)
    return jnp.asarray(pad(cos, 1.0)), jnp.asarray(pad(sin, 0.0))


def _permute_heads(w_in, q_g, k_g):
    n_heads = N_Q + N_KV
    cols = (np.arange(n_heads)[:, None] * HEAD_DIM + _HEAD_ORDER[None, :]).reshape(-1)
    cols = np.concatenate([cols, np.arange(n_heads * HEAD_DIM, ODD_IN)])
    return w_in[..., cols], q_g[..., _HEAD_ORDER], k_g[..., _HEAD_ORDER]


def _odd_in_kernel(x_ref, mod_ref, g_ref, win_ref, qg_ref, kg_ref, *refs):
    tabs, (q_ref, k_ref, v_ref, gate_ref) = refs[:2 * PIECES], refs[2 * PIECES:]
    ones = jnp.ones((ROW_TILE, HEAD_DIM), BF16)
    for piece in range(PIECES):
        rows = slice(piece * ROW_TILE, (piece + 1) * ROW_TILE)
        h = _prologue(x_ref[rows, :], mod_ref[piece], g_ref[...])
        z = jnp.dot(h.astype(BF16), win_ref[...], preferred_element_type=F32)
        cos = tabs[2 * piece][...]
        sin = tabs[2 * piece + 1][...]

        def norm_rope(xh, gain, scale):
            xn = (xh * lax.rsqrt(jnp.mean(xh * xh, axis=-1, keepdims=True) + EPS)) * gain
            out = xn * cos + pltpu.roll(xn, HEAD_DIM // 2, 1) * sin
            return (out * scale).astype(BF16) if scale is not None else out.astype(BF16)

        for hq in range(N_Q):
            q_ref[rows, hq * HEAD_DIM:(hq + 1) * HEAD_DIM] = norm_rope(
                z[:, hq * HEAD_DIM:(hq + 1) * HEAD_DIM], qg_ref[...], Q_SCALE)
        for hk in range(N_KV):
            lo = C_WIDTH + hk * HEAD_DIM
            k_ref[rows, hk * HEAD_DIM:(hk + 1) * HEAD_DIM] = norm_rope(
                z[:, lo:lo + HEAD_DIM], kg_ref[...], None)
            lo = C_WIDTH + KV_WIDTH + hk * HEAD_DIM
            v_ref[rows, hk * V_EXT:hk * V_EXT + HEAD_DIM] = z[:, lo:lo + HEAD_DIM].astype(BF16)
            v_ref[rows, hk * V_EXT + HEAD_DIM:(hk + 1) * V_EXT] = ones
        gate_ref[rows, :] = z[:, C_WIDTH + 2 * KV_WIDTH:].astype(BF16)


def _odd_in(i, xc, mod_p, norm_g, w_in, q_g, k_g, cos, sin):
    tabs, tab_specs = [], []
    for p in range(PIECES):
        spec = pl.BlockSpec((ROW_TILE, HEAD_DIM), lambda t, p=p: ((t * PIECES + p) % N_TILES, 0))
        tabs += [cos, sin]
        tab_specs += [spec, spec]
    return pl.pallas_call(
        _odd_in_kernel,
        grid=(N_BIG,),
        in_specs=[
            _tile(D_MODEL), _MOD_SPEC, _full(1, D_MODEL), _layer(i, D_MODEL, ODD_IN),
            _full(1, HEAD_DIM), _full(1, HEAD_DIM), *tab_specs,
        ],
        out_specs=[_tile(C_WIDTH), _tile(KV_WIDTH), _tile(N_KV * V_EXT), _tile(C_WIDTH)],
        out_shape=[
            jax.ShapeDtypeStruct((BATCH * NTOK, C_WIDTH), BF16),
            jax.ShapeDtypeStruct((BATCH * NTOK, KV_WIDTH), BF16),
            jax.ShapeDtypeStruct((BATCH * NTOK, N_KV * V_EXT), BF16),
            jax.ShapeDtypeStruct((BATCH * NTOK, C_WIDTH), BF16),
        ],
        compiler_params=_params("parallel"),
        name="odd_in",
    )(xc, mod_p, norm_g, w_in, q_g, k_g, *tabs)


_NT = (((1,), (1,)), ((), ()))


def _softmax_weights(q, k):
    s = lax.dot_general(q, k, _NT, preferred_element_type=F32)
    return jnp.exp2(s - jnp.max(s, axis=-1, keepdims=True)).astype(BF16)


def _weighted_values(p, v):
    ov = jnp.dot(p, v, preferred_element_type=F32)
    return (ov[:, :HEAD_DIM] / ov[:, HEAD_DIM:HEAD_DIM + 1]).astype(BF16)


def _attn_lat_kernel(q_ref, k_ref, v_ref, o_ref, p_ref):
    @pl.when(pl.program_id(0) == 0)
    def _():
        p_ref[...] = jnp.ones(p_ref.shape, BF16)

    k = k_ref[0]
    v = v_ref[0]
    for h in range(Q_PER_KV):
        cols = slice(h * HEAD_DIM, (h + 1) * HEAD_DIM)
        for r in range(ATTN_TILE // ATTN_UNIT):
            rows = slice(r * ATTN_UNIT, (r + 1) * ATTN_UNIT)
            p_new = _softmax_weights(q_ref[0, rows, cols], k)
            o_ref[0, rows, cols] = _weighted_values(p_ref[h, rows, :], v)
            p_ref[h, rows, :] = p_new


def _attention_latent(q, k, v):
    n_tiles = SEQ // ATTN_TILE
    n_units = BATCH * N_KV * n_tiles
    width = Q_PER_KV * HEAD_DIM

    def unit(u):
        return u // (N_KV * n_tiles), (u // n_tiles) % N_KV, u % n_tiles

    def q_map(t):
        b, g, i = unit(jnp.minimum(t, n_units - 1))
        return b, i, g

    def k_map(t):
        b, g, _ = unit(jnp.minimum(t, n_units - 1))
        return b, 0, g

    def v_map(t):
        b, g, _ = unit(jnp.maximum(t - 1, 0))
        return b, 0, g

    def o_map(t):
        b, g, i = unit(jnp.maximum(t - 1, 0))
        return b, i, g

    return pl.pallas_call(
        _attn_lat_kernel,
        grid=(n_units + 1,),
        in_specs=[
            pl.BlockSpec((1, ATTN_TILE, width), q_map),
            pl.BlockSpec((1, NTOK, HEAD_DIM), k_map),
            pl.BlockSpec((1, NTOK, V_EXT), v_map),
        ],
        out_specs=pl.BlockSpec((1, ATTN_TILE, width), o_map),
        out_shape=jax.ShapeDtypeStruct((BATCH, SEQ, C_WIDTH), BF16),
        scratch_shapes=[pltpu.VMEM((Q_PER_KV, ATTN_TILE, NTOK), BF16)],
        compiler_params=_params("arbitrary"),
        name="attention_latent",
    )(q, k, v)


def _attn_ctx_kernel(q_ref, k_ref, v_ref, o_ref):
    for h in range(N_Q):
        g = h // Q_PER_KV
        p = _softmax_weights(q_ref[0, :, h * HEAD_DIM:(h + 1) * HEAD_DIM],
                             k_ref[0, :, g * HEAD_DIM:(g + 1) * HEAD_DIM])
        o_ref[0, :, h * HEAD_DIM:(h + 1) * HEAD_DIM] = _weighted_values(
            p, v_ref[0, :, g * V_EXT:(g + 1) * V_EXT])


def _attention_context(q, k, v):
    spec = lambda w: pl.BlockSpec((1, CTX_LEN, w), lambda b: (b, CTX_TILE, 0))
    return pl.pallas_call(
        _attn_ctx_kernel,
        grid=(BATCH,),
        in_specs=[spec(C_WIDTH), spec(KV_WIDTH), spec(N_KV * V_EXT)],
        out_specs=pl.BlockSpec((1, CTX_LEN, C_WIDTH), lambda b: (b, 0, 0)),
        out_shape=jax.ShapeDtypeStruct((BATCH, CTX_LEN, C_WIDTH), BF16),
        compiler_params=_params("parallel"),
        name="attention_context",
    )(q, k, v)


def _odd_out_kernel(*refs, final):
    if final:
        o_ref, gate_ref, x_ref, mod_ref, wout_ref, fg_ref, out_ref = refs
        attn = lambda piece: o_ref[piece * ROW_TILE:(piece + 1) * ROW_TILE, :]
    else:
        o_refs, (gate_ref, x_ref, mod_ref, wout_ref, fg_ref, out_ref) = refs[:2 * PIECES], refs[2 * PIECES:]

        def attn(piece):
            is_ctx = (pl.program_id(0) * PIECES + piece) % N_TILES == CTX_TILE
            return jnp.where(is_ctx, o_refs[2 * piece + 1][...], o_refs[2 * piece][...])

    for piece in range(PIECES):
        rows = slice(piece * ROW_TILE, (piece + 1) * ROW_TILE)
        g = gate_ref[rows, :].astype(F32)
        a = (attn(piece).astype(F32) * (g * jax.nn.sigmoid(g))).astype(BF16)
        gt = mod_ref[piece % mod_ref.shape[0]][:, 2 * D_MODEL:]
        x = x_ref[rows, :] + gt * jnp.dot(a, wout_ref[...], preferred_element_type=F32)
        if final:
            x = (x * lax.rsqrt(jnp.mean(x * x, axis=-1, keepdims=True) + EPS)) * fg_ref[...]
        out_ref[rows, :] = x


def _odd_out(i, o_lat, o_ctx, gate, xc, mod_p, w_out, final_g):
    o_args, o_specs = [], []
    for p in range(PIECES):
        sample = lambda t, p=p: (t * PIECES + p) // N_TILES
        lat_tile = lambda t, p=p: jnp.minimum((t * PIECES + p) % N_TILES, LAT_TILES - 1)
        o_args += [o_lat, o_ctx]
        o_specs += [
            pl.BlockSpec((None, ROW_TILE, C_WIDTH), lambda t, s=sample, i=lat_tile: (s(t), i(t), 0)),
            pl.BlockSpec((None, CTX_LEN, C_WIDTH), lambda t, s=sample: (s(t), 0, 0)),
        ]
    return pl.pallas_call(
        functools.partial(_odd_out_kernel, final=False),
        grid=(N_BIG,),
        in_specs=[*o_specs, _tile(C_WIDTH), _tile(D_MODEL), _MOD_SPEC,
                  _layer(i, D_MODEL, D_MODEL), _full(1, D_MODEL)],
        out_specs=_tile(D_MODEL),
        out_shape=jax.ShapeDtypeStruct((BATCH * NTOK, D_MODEL), F32),
        compiler_params=_params("parallel"),
        name="odd_out",
    )(*o_args, gate, xc, mod_p, w_out, final_g)


def _odd_out_final(i, o, gate, xc, mod_l, w_out, final_g):
    tile = lambda w: pl.BlockSpec((None, BIG_TILE, w), lambda b, j: (b, j, 0))
    full = lambda *s: pl.BlockSpec(s, lambda b, j: (0,) * len(s))
    per_sample = lambda a, w: a.reshape(BATCH, NTOK, w)
    return pl.pallas_call(
        functools.partial(_odd_out_kernel, final=True),
        grid=(BATCH, SEQ // BIG_TILE),
        in_specs=[tile(C_WIDTH), tile(C_WIDTH), tile(D_MODEL),
                  pl.BlockSpec((1, 1, 3 * D_MODEL), lambda b, j: (b, 0, 0)),
                  _layer(i, D_MODEL, D_MODEL), full(1, D_MODEL)],
        out_specs=tile(D_MODEL),
        out_shape=jax.ShapeDtypeStruct((BATCH, SEQ, D_MODEL), F32),
        compiler_params=_params("parallel", "parallel"),
        name="odd_out_final",
    )(o, per_sample(gate, C_WIDTH), per_sample(xc, D_MODEL), mod_l, w_out, final_g)


def kernel(x, c, ctx, c_ctx, norm_g, w_mod, b_mod, we_in, we_out, gm_v_g, gm_w_s, gm_b_s,
           s5_lam_re, s5_lam_im, s5_log_dt, s5_b_re, s5_b_im, s5_c_re, s5_c_im, s5_d,
           s5_w_glu, s5_b_glu, wo_in, wo_out, q_norm_g, k_norm_g, final_g):
    cond = jnp.concatenate([c, c_ctx[None], jnp.zeros((MOD_ROWS - BATCH - 1, D_MODEL), F32)], axis=0)
    mods = _modulation(cond, w_mod, b_mod).reshape(DEPTH, MOD_ROWS, 1, 3 * D_MODEL)
    xc = (x, ctx)
    piece_rows = np.where(np.arange(N_TILES)[None, :] == CTX_TILE, CTX_ROW, np.arange(BATCH)[:, None]).reshape(-1)
    cos, sin = _rope_tables()
    perm_in = _chunk_perm(SUB_TILE)
    perm_out = _chunk_perm(RELAYOUT_ROWS).T
    row = lambda a: a.reshape(1, -1)
    per_sample = lambda a: a.reshape(BATCH, NTOK, a.shape[-1])
    we_in, we_out, w_glu, w_s, wo_out = (w.astype(BF16) for w in (we_in, we_out, s5_w_glu, gm_w_s, wo_out))

    for layer in range(DEPTH):
        i = layer // 2
        mod_p = mods[layer][piece_rows]
        if layer % 2 == 0:
            b_full = jnp.repeat(gm_b_s[i].T, A_GROUP_W, axis=1)
            ya, xs, gb, xg = _even_in(i, xc, mod_p, row(norm_g[layer]), we_in, row(gm_v_g[i]), w_s,
                                      b_full, perm_in)
            prow, pmat = _s5_params(s5_lam_re[i], s5_lam_im[i], s5_log_dt[i], s5_b_re[i],
                                    s5_b_im[i], s5_c_re[i], s5_c_im[i])
            yg = _s5(xg, prow, pmat)
            xc = _even_out(i, ya, xs, gb, yg, xc, mod_p, row(s5_d[i]), w_glu, row(s5_b_glu[i]), we_out,
                           perm_out)
        else:
            w_in, q_g, k_g = _permute_heads(wo_in[i], q_norm_g[i], k_norm_g[i])
            q, k, v, gate = _odd_in(0, xc, mod_p, row(norm_g[layer]), w_in.astype(BF16)[None], row(q_g),
                                    row(k_g), cos, sin)
            q, k, v = per_sample(q), per_sample(k), per_sample(v)
            o = _attention_latent(q, k, v)
            if layer == DEPTH - 1:
                return _odd_out_final(i, o, gate, xc, mods[layer], wo_out, row(final_g))
            xc = _odd_out(i, o, _attention_context(q, k, v), gate, xc, mod_p, wo_out, row(final_g))
```
